```python
import math
import jax
import jax.numpy as jnp
from jax import lax
import numpy as np

D_MODEL = 4096
BATCH = 4
SEQ = 4096
DEPTH = 2

GRID_W = 64
CTX_LEN = 256
RMS_EPS = 1e-6
N_MOD = 6

A_HEAD_DIM = 128
A_WIDTH = D_MODEL // 2
A_HEADS = A_WIDTH // A_HEAD_DIM
A_CHUNK = 64

B_WIDTH = D_MODEL // 4
B_GROUPS = 8
B_GROUP_DIM = B_WIDTH // B_GROUPS
B_CHUNK = 128

C_WIDTH = D_MODEL // 4
C_POS_BANDS = 16
C_POS_DIM = 1 + 2 * C_POS_BANDS
C_FILTER_HIDDEN = 64
C_MIN_DECAY = math.log(1e-2) / 1.5
C_MAX_DECAY = math.log(1e-2) / 0.3

N_BRANCH = 3

P_HEADS = 8
P_KEYS = 128
P_EXPERTS = P_KEYS * P_KEYS
P_QUERY_DIM = 256
P_HALF = P_QUERY_DIM // 2
P_TOPK = 16
P_TOKEN_BLOCK = 128

OFF_K = 0
OFF_V = OFF_K + A_WIDTH
OFF_BA = OFF_V + A_WIDTH
OFF_Q = OFF_BA + 4 * A_HEADS
OFF_OG = OFF_Q + A_WIDTH
OFF_BU = OFF_OG + A_WIDTH
OFF_BV = OFF_BU + B_WIDTH
OFF_CX = OFF_BV + B_WIDTH
OFF_GATE = OFF_CX + 3 * C_WIDTH
IN_COLS = OFF_GATE + N_BRANCH * D_MODEL

kernel_name = 'hybrid_flow_deltanet_gmlp_hyena_peer'


def rmsnorm(x):
    x32 = x.astype(jnp.float32)
    return (x32 * lax.rsqrt(jnp.mean(x32 * x32, axis=-1, keepdims=True) + RMS_EPS)).astype(x.dtype)


def l2norm(x):
    return x * lax.rsqrt(jnp.sum(x * x, axis=-1, keepdims=True) + RMS_EPS)


def grid_conv(u, w, rows, cols):
    b, l, ch = u.shape
    out = lax.conv_general_dilated(
        u.reshape(b, rows, cols, ch), w[:, :, None, :].astype(u.dtype),
        window_strides=(1, 1), padding='SAME',
        dimension_numbers=('NHWC', 'HWIO', 'NHWC'), feature_group_count=ch)
    return out.reshape(b, l, ch)


def delta_rule_chunked(q, k, v, log_alpha, beta, s0):
    b, l, h, _ = k.shape
    dv = v.shape[-1]
    n = l // A_CHUNK

    def to_chunks(t):
        t = t.reshape((b, n, A_CHUNK, h) + t.shape[3:])
        return jnp.moveaxis(t, (1, 3), (0, 2))

    kc, vc, bc = to_chunks(k), to_chunks(v), to_chunks(beta)
    g = jnp.cumsum(to_chunks(log_alpha), axis=-1)
    idx = jnp.arange(A_CHUNK)
    lower = idx[:, None] >= idx[None, :]
    diff = g[..., :, None] - g[..., None, :]
    decay = jnp.where(lower, jnp.exp(jnp.where(lower, diff, 0.0)), 0.0)
    kk = jnp.einsum('nbhik,nbhjk->nbhij', kc, kc)
    a_mat = jnp.where(idx[:, None] > idx[None, :], kk * decay * bc[..., None], 0.0)
    rhs = jnp.concatenate([vc, kc * jnp.exp(g)[..., None]], axis=-1) * bc[..., None]
    sol = lax.linalg.triangular_solve(a_mat + jnp.eye(A_CHUNK, dtype=kc.dtype), rhs,
                                      left_side=True, lower=True, unit_diagonal=True)
    u_c, w_c = sol[..., :dv], sol[..., dv:]
    g_last = g[..., -1]
    k_end = kc * jnp.exp(g_last[..., None] - g)[..., None]

    def advance(s, u_i, w_i, k_i, gl):
        v_new = u_i - jnp.einsum('bhck,bhkv->bhcv', w_i, s)
        s_next = s * jnp.exp(gl)[..., None, None] + jnp.einsum('bhck,bhcv->bhkv', k_i, v_new)
        return v_new, s_next

    if q is None:
        def state_step(s, inp):
            _, s_next = advance(s, *inp)
            return s_next, None
        s_fin, _ = lax.scan(state_step, s0, (u_c, w_c, k_end, g_last))
        return None, s_fin

    qc = to_chunks(q)
    q_dec = qc * jnp.exp(g)[..., None]
    qk = jnp.einsum('nbhik,nbhjk->nbhij', qc, kc) * decay

    def out_step(s, inp):
        u_i, w_i, k_i, gl, q_i, qk_i = inp
        v_new, s_next = advance(s, u_i, w_i, k_i, gl)
        o = jnp.einsum('bhck,bhkv->bhcv', q_i, s) + jnp.einsum('bhij,bhjv->bhiv', qk_i, v_new)
        return s_next, o

    s_fin, o = lax.scan(out_step, s0, (u_c, w_c, k_end, g_last, q_dec, qk))
    o = jnp.moveaxis(o, (0, 2), (1, 3)).reshape(b, l, h, dv)
    return o, s_fin


def rev(t):
    return jnp.flip(t, axis=1)


def bidir_delta(z, a_conv, a_log, a_dt_bias, rows, cols, s0_f, s0_b, with_out):
    b, l, _ = z.shape
    kv = z[..., OFF_K:OFF_BA]
    src = jnp.concatenate([kv, z[..., OFF_Q:OFF_OG]], axis=-1) if with_out else kv
    act = jax.nn.silu(grid_conv(src, a_conv[..., :src.shape[-1]], rows, cols)).astype(jnp.float32)
    act = act.reshape(b, l, -1, A_HEADS, A_HEAD_DIM)
    k = l2norm(act[:, :, 0])
    v = act[:, :, 1]
    ba = z[..., OFF_BA:OFF_Q].astype(jnp.float32).reshape(b, l, 4, A_HEADS)
    beta = jax.nn.sigmoid(ba[:, :, 0:2])
    la = -jnp.exp(a_log.astype(jnp.float32)) * jax.nn.softplus(ba[:, :, 2:4] + a_dt_bias.astype(jnp.float32))
    if with_out:
        q = l2norm(act[:, :, 2]) * (A_HEAD_DIM ** -0.5)
        o_f, s_f = delta_rule_chunked(q, k, v, la[:, :, 0], beta[:, :, 0], s0_f)
        o_b, s_b = delta_rule_chunked(rev(q), rev(k), rev(v), rev(la[:, :, 1]), rev(beta[:, :, 1]), s0_b)
        return o_f + rev(o_b), s_f, s_b
    _, s_f = delta_rule_chunked(None, k, v, la[:, :, 0], beta[:, :, 0], s0_f)
    _, s_b = delta_rule_chunked(None, rev(k), rev(v), rev(la[:, :, 1]), rev(beta[:, :, 1]), s0_b)
    return None, s_f, s_b


def chunk_gmlp(z, w_s, b_s):
    b, l, _ = z.shape
    u = jax.nn.gelu(z[..., OFF_BU:OFF_BV])
    v32 = jax.nn.gelu(z[..., OFF_BV:OFF_CX]).astype(jnp.float32)
    mu = jnp.mean(v32, axis=-1, keepdims=True)
    var = jnp.mean(jnp.square(v32 - mu), axis=-1, keepdims=True)
    vn = ((v32 - mu) * lax.rsqrt(var + RMS_EPS)).astype(z.dtype)
    vn = vn.reshape(b, l // B_CHUNK, B_CHUNK, B_GROUPS, B_GROUP_DIM)
    mixed = jnp.einsum('gpq,bnqgc->bnpgc', w_s, vn) + b_s.T[:, :, None]
    return u * mixed.reshape(b, l, B_WIDTH)


def implicit_filters(l, fw1, fb1, freq1, fw2, fb2, freq2, fw3, fb3):
    f32 = jnp.float32
    pos = jnp.arange(l, dtype=f32)
    t = jnp.linspace(0.0, 1.0, l, dtype=f32)[:, None]
    bands = jnp.linspace(1e-4, C_POS_BANDS - 1, C_POS_BANDS, dtype=f32)
    ang = (2.0 * math.pi / l) * pos[:, None] * bands
    feats = jnp.concatenate([t, jnp.cos(ang), -jnp.sin(ang)], axis=-1)
    hid = jnp.sin(freq1.astype(f32) * (feats @ fw1.astype(f32) + fb1.astype(f32)))
    hid = jnp.sin(freq2.astype(f32) * (hid @ fw2.astype(f32) + fb2.astype(f32)))
    filt = (hid @ fw3.astype(f32) + fb3.astype(f32)).reshape(l, 2, C_WIDTH)
    rate = jnp.abs(jnp.linspace(C_MIN_DECAY, C_MAX_DECAY, C_WIDTH, dtype=f32))
    filt = filt * jnp.exp(-t[:, :, None] * rate)
    filt = filt * lax.rsqrt(jnp.sum(filt * filt, axis=0, keepdims=True) + RMS_EPS)
    return filt[:, 0], filt[:, 1]


def bidir_long_conv(u, h_f, h_b):
    l, ch = h_f.shape
    h_full = jnp.concatenate([h_f[:1] + h_b[:1], h_f[1:], jnp.zeros((1, ch), h_f.dtype), h_b[:0:-1]], axis=0)
    spec = jnp.fft.rfft(u, n=2 * l, axis=1) * jnp.fft.rfft(h_full, n=2 * l, axis=0)[None]
    return jnp.fft.irfft(spec, n=2 * l, axis=1)[:, :l]


def hyena_branch(z, conv_w, skip, fw1, fb1, freq1, fw2, fb2, freq2, fw3, fb3, rows, cols):
    b, l, _ = z.shape
    proj = grid_conv(z[..., OFF_CX:OFF_GATE], conv_w, rows, cols).astype(jnp.float32)
    x0, x1, v = jnp.split(proj, 3, axis=-1)
    h_f, h_b = implicit_filters(l, fw1, fb1, freq1, fw2, fb2, freq2, fw3, fb3)
    u = x1 * v
    y = bidir_long_conv(u, h_f, h_b) + skip.astype(jnp.float32) * u
    return (x0 * y).astype(z.dtype)


def token_mixer(xn, rows, cols, s0_f, s0_b, lp):
    b, l, _ = xn.shape
    z = xn @ lp['w_in']
    o_a, s_f, s_b = bidir_delta(z, lp['a_conv'], lp['a_log'], lp['a_dt_bias'], rows, cols, s0_f, s0_b, True)
    o_a = o_a * lax.rsqrt(jnp.mean(o_a * o_a, axis=-1, keepdims=True) + RMS_EPS) * lp['a_norm'].astype(jnp.float32)
    o_a = (o_a.reshape(b, l, A_WIDTH) * jax.nn.silu(z[..., OFF_OG:OFF_BU].astype(jnp.float32))).astype(xn.dtype)
    o_b = chunk_gmlp(z, lp['b_ws'], lp['b_bs'])
    o_c = hyena_branch(z, lp['c_conv'], lp['c_skip'], lp['c_fw1'], lp['c_fb1'], lp['c_freq1'],
                       lp['c_fw2'], lp['c_fb2'], lp['c_freq2'], lp['c_fw3'], lp['c_fb3'], rows, cols)
    gates = jax.nn.sigmoid(z[..., OFF_GATE:].astype(jnp.float32)).astype(xn.dtype).reshape(b, l, N_BRANCH, D_MODEL)
    merged = (gates[:, :, 0] * (o_a @ lp['w_br_a'])
              + gates[:, :, 1] * (o_b @ lp['w_br_b'])
              + gates[:, :, 2] * (o_c @ lp['w_br_c']))
    return merged @ lp['w_out'], s_f, s_b


def peer(xn, wq, keys, u_tab, v_tab):
    b, l, d = xn.shape
    q = (xn @ wq).astype(jnp.float32).reshape(b, l, P_HEADS, 2, P_HALF)
    s = jnp.einsum('blhpc,hpkc->blhpk', q, keys.astype(jnp.float32))
    top_s, top_i = lax.top_k(s, P_TOPK)
    cand = top_s[..., 0, :, None] + top_s[..., 1, None, :]
    best_s, best_pos = lax.top_k(cand.reshape(b, l, P_HEADS, P_TOPK * P_TOPK), P_TOPK)
    i1 = jnp.take_along_axis(top_i[..., 0, :], best_pos // P_TOPK, axis=-1)
    i2 = jnp.take_along_axis(top_i[..., 1, :], best_pos % P_TOPK, axis=-1)
    expert = i1 * P_KEYS + i2
    gate = jax.nn.softmax(best_s, axis=-1).astype(xn.dtype)
    n_blk = (b * l) // P_TOKEN_BLOCK
    xs = xn.reshape(n_blk, P_TOKEN_BLOCK, d)
    es = expert.reshape(n_blk, P_TOKEN_BLOCK, P_HEADS * P_TOPK)
    gs = gate.reshape(n_blk, P_TOKEN_BLOCK, P_HEADS * P_TOPK)

    def block(args):
        xb, eb, gb = args
        act = jax.nn.gelu(jnp.einsum('tkd,td->tk', jnp.take(u_tab, eb, axis=0), xb))
        return jnp.einsum('tk,tkd->td', gb * act, jnp.take(v_tab, eb, axis=0))

    return lax.map(block, (xs, es, gs)).reshape(b, l, d)


def setup_inputs(seed: int = 0) -> dict:
    key = jax.random.key(seed)
    ks = jax.random.split(key, 32)
    f32 = jnp.float32
    d = D_MODEL

    def nrm(k, shape, scale):
        return jax.random.normal(k, shape, f32) * scale

    dt = jnp.exp(jax.random.uniform(ks[9], (DEPTH, 2, A_HEADS), f32, math.log(1e-3), math.log(1e-1)))
    return {
        'x': nrm(ks[0], (BATCH, SEQ, d), 1.0),
        'c': nrm(ks[1], (BATCH, d), 1.0),
        'ctx': nrm(ks[2], (BATCH, CTX_LEN, d), 1.0),
        'c_ctx': nrm(ks[3], (d,), 1.0),
        'ada_w': nrm(ks[4], (DEPTH, d, N_MOD * d), d ** -0.5),
        'ada_b': nrm(ks[5], (DEPTH, N_MOD * d), 0.02),
        'w_in': nrm(ks[6], (DEPTH, d, IN_COLS), d ** -0.5),
        'a_conv': nrm(ks[7], (DEPTH, 3, 3, 3 * A_WIDTH), 1.0 / 3.0),
        'a_log': jnp.log(jax.random.uniform(ks[8], (DEPTH, 2, A_HEADS), f32, 1.0, 16.0)),
        'a_dt_bias': dt + jnp.log(-jnp.expm1(-dt)),
        'a_norm': 1.0 + nrm(ks[10], (DEPTH, A_HEAD_DIM), 0.02),
        'b_ws': nrm(ks[11], (DEPTH, B_GROUPS, B_CHUNK, B_CHUNK), B_CHUNK ** -0.5),
        'b_bs': 1.0 + nrm(ks[12], (DEPTH, B_GROUPS, B_CHUNK), 0.02),
        'c_conv': nrm(ks[13], (DEPTH, 3, 3, 3 * C_WIDTH), 1.0 / 3.0),
        'c_fw1': nrm(ks[14], (DEPTH, C_POS_DIM, C_FILTER_HIDDEN), C_POS_DIM ** -0.5),
        'c_fb1': nrm(ks[15], (DEPTH, C_FILTER_HIDDEN), 0.1),
        'c_freq1': 1.0 + nrm(ks[16], (DEPTH, C_FILTER_HIDDEN), 0.1),
        'c_fw2': nrm(ks[17], (DEPTH, C_FILTER_HIDDEN, C_FILTER_HIDDEN), C_FILTER_HIDDEN ** -0.5),
        'c_fb2': nrm(ks[18], (DEPTH, C_FILTER_HIDDEN), 0.1),
        'c_freq2': 1.0 + nrm(ks[19], (DEPTH, C_FILTER_HIDDEN), 0.1),
        'c_fw3': nrm(ks[20], (DEPTH, C_FILTER_HIDDEN, 2 * C_WIDTH), C_FILTER_HIDDEN ** -0.5),
        'c_fb3': nrm(ks[21], (DEPTH, 2 * C_WIDTH), 0.02),
        'c_skip': nrm(ks[22], (DEPTH, C_WIDTH), 0.5),
        'w_br_a': nrm(ks[23], (DEPTH, A_WIDTH, d), A_WIDTH ** -0.5),
        'w_br_b': nrm(ks[24], (DEPTH, B_WIDTH, d), B_WIDTH ** -0.5),
        'w_br_c': nrm(ks[25], (DEPTH, C_WIDTH, d), C_WIDTH ** -0.5),
        'w_out': nrm(ks[26], (DEPTH, d, d), d ** -0.5),
        'p_wq': nrm(ks[27], (DEPTH, d, P_HEADS * P_QUERY_DIM), d ** -0.5),
        'p_keys': nrm(ks[28], (DEPTH, P_HEADS, 2, P_KEYS, P_HALF), P_HALF ** -0.5),
        'p_u': nrm(ks[29], (DEPTH, P_EXPERTS, d), d ** -0.5),
        'p_v': nrm(ks[30], (DEPTH, P_EXPERTS, d), P_HEADS ** -0.5),
        'final_norm': 1.0 + nrm(ks[31], (d,), 0.02),
    }


def reference(x, c, ctx, c_ctx, ada_w, ada_b, w_in, a_conv, a_log, a_dt_bias, a_norm,
              b_ws, b_bs, c_conv, c_fw1, c_fb1, c_freq1, c_fw2, c_fb2, c_freq2, c_fw3, c_fb3,
              c_skip, w_br_a, w_br_b, w_br_c, w_out, p_wq, p_keys, p_u, p_v, final_norm):
    b, l, d = x.shape
    rows = l // GRID_W
    l_ctx = ctx.shape[1]
    s_zero = jnp.zeros((b, A_HEADS, A_HEAD_DIM, A_HEAD_DIM), jnp.float32)
    h, hc = x, ctx
    for i in range(DEPTH):
        lp = {'w_in': w_in[i], 'a_conv': a_conv[i], 'a_log': a_log[i], 'a_dt_bias': a_dt_bias[i],
              'a_norm': a_norm[i], 'b_ws': b_ws[i], 'b_bs': b_bs[i], 'c_conv': c_conv[i],
              'c_skip': c_skip[i], 'c_fw1': c_fw1[i], 'c_fb1': c_fb1[i], 'c_freq1': c_freq1[i],
              'c_fw2': c_fw2[i], 'c_fb2': c_fb2[i], 'c_freq2': c_freq2[i], 'c_fw3': c_fw3[i],
              'c_fb3': c_fb3[i], 'w_br_a': w_br_a[i], 'w_br_b': w_br_b[i], 'w_br_c': w_br_c[i],
              'w_out': w_out[i]}
        mod = (jax.nn.silu(c) @ ada_w[i] + ada_b[i]).reshape(b, N_MOD, 1, d)
        mod_c = (jax.nn.silu(c_ctx) @ ada_w[i] + ada_b[i]).reshape(N_MOD, 1, 1, d)
        xnc = rmsnorm(hc) * (1.0 + mod_c[1]) + mod_c[0]
        if i == DEPTH - 1:
            z_c = xnc @ w_in[i][:, :OFF_Q]
            _, s_f, s_b = bidir_delta(z_c, a_conv[i], a_log[i], a_dt_bias[i], 1, l_ctx, s_zero, s_zero, False)
        else:
            out_c, s_f, s_b = token_mixer(xnc, 1, l_ctx, s_zero, s_zero, lp)
            hc = hc + mod_c[2] * out_c
            hc = hc + mod_c[5] * peer(rmsnorm(hc) * (1.0 + mod_c[4]) + mod_c[3], p_wq[i], p_keys[i], p_u[i], p_v[i])
        xn = rmsnorm(h) * (1.0 + mod[:, 1]) + mod[:, 0]
        out, _, _ = token_mixer(xn, rows, GRID_W, s_f, s_b, lp)
        h = h + mod[:, 2] * out
        h = h + mod[:, 5] * peer(rmsnorm(h) * (1.0 + mod[:, 4]) + mod[:, 3], p_wq[i], p_keys[i], p_u[i], p_v[i])
    return rmsnorm(h) * final_norm
```

```python
import functools
import math

import numpy as np
import jax
import jax.numpy as jnp
from jax import lax
from jax.experimental import pallas as pl
from jax.experimental.pallas import tpu as pltpu

F32 = jnp.float32
BF16 = jnp.bfloat16

V7X_LANES = 128
V7X_SUBLANES = 8
V7X_VMEM_LIMIT_BYTES = 56 * 1024 * 1024

GRID_W = 64
RMS_EPS = 1e-6
N_MOD = 6
A_HEAD_DIM = 128
A_CHUNK = 64
B_CHUNK = 128
C_POS_BANDS = 16
C_MIN_DECAY = math.log(1e-2) / 1.5
C_MAX_DECAY = math.log(1e-2) / 0.3
N_BRANCH = 3
P_KEYS = 128
P_TOPK = 16


def _cparams(*sem):
    return pltpu.CompilerParams(dimension_semantics=sem, vmem_limit_bytes=V7X_VMEM_LIMIT_BYTES)


def _mod_kernel(c_ref, w_ref, b_ref, o_ref):
    c = c_ref[...]
    a = (c * jax.nn.sigmoid(c)).astype(BF16)
    o_ref[...] = jnp.dot(a, w_ref[...].astype(BF16), preferred_element_type=F32) + b_ref[...]


def _mod_call(cvec, ada_w, ada_b):
    rows, d = cvec.shape
    n = ada_w.shape[1]
    tn = 1024
    return pl.pallas_call(
        _mod_kernel,
        grid=(n // tn,),
        in_specs=[pl.BlockSpec((rows, d), lambda j: (0, 0)),
                  pl.BlockSpec((d, tn), lambda j: (0, j)),
                  pl.BlockSpec((1, tn), lambda j: (0, j))],
        out_specs=pl.BlockSpec((rows, tn), lambda j: (0, j)),
        out_shape=jax.ShapeDtypeStruct((rows, n), F32),
        compiler_params=_cparams("parallel"),
        name="ada_mod",
    )(cvec, ada_w, ada_b.reshape(1, n))


def _norm_mod_kernel(*refs, with_delta, with_shift, emit_h):
    refs = list(refs)
    h = refs.pop(0)[0]
    if with_delta:
        d_ref, g_ref = refs.pop(0), refs.pop(0)
        h = h + g_ref[0] * d_ref[0]
    sc_ref = refs.pop(0)
    xn = h * lax.rsqrt(jnp.mean(h * h, axis=-1, keepdims=True) + RMS_EPS)
    if with_shift:
        xn = xn * (1.0 + sc_ref[0]) + refs.pop(0)[0]
    else:
        xn = xn * sc_ref[0]
    if emit_h:
        refs.pop(0)[0] = h
    refs.pop(0)[0] = xn.astype(refs[0].dtype)


def _norm_mod_call(h, scale, shift=None, delta=None, gate=None, emit_h=False, out_dtype=BF16, tl=256):
    b, l, d = h.shape
    tl = min(tl, l)
    tok = pl.BlockSpec((1, tl, d), lambda i, j: (i, j, 0))

    def vec(v):
        if v.shape[0] == 1:
            return pl.BlockSpec((1, 1, d), lambda i, j: (0, 0, 0))
        return pl.BlockSpec((1, 1, d), lambda i, j: (i, 0, 0))

    args, in_specs = [h], [tok]
    if delta is not None:
        args += [delta, gate]
        in_specs += [tok, vec(gate)]
    args.append(scale)
    in_specs.append(vec(scale))
    if shift is not None:
        args.append(shift)
        in_specs.append(vec(shift))
    out_shape, out_specs = [], []
    if emit_h:
        out_shape.append(jax.ShapeDtypeStruct((b, l, d), F32))
        out_specs.append(tok)
    out_shape.append(jax.ShapeDtypeStruct((b, l, d), out_dtype))
    out_specs.append(tok)
    res = pl.pallas_call(
        functools.partial(_norm_mod_kernel, with_delta=delta is not None, with_shift=shift is not None, emit_h=emit_h),
        grid=(b, l // tl),
        in_specs=in_specs, out_specs=tuple(out_specs), out_shape=tuple(out_shape),
        compiler_params=_cparams("parallel", "parallel"),
        name="norm_mod",
    )(*args)
    return (res[0], res[1]) if emit_h else (None, res[0])


def _mm_kernel(a_ref, w_ref, o_ref):
    o_ref[...] = jnp.dot(a_ref[...], w_ref[...], preferred_element_type=F32).astype(o_ref.dtype)


def _pick_tile(n, pref):
    t = min(pref, n)
    while n % t:
        t //= 2
    return t


def _matmul(a, w, out_dtype=F32, tm=512, tn=1024):
    m, k = a.shape
    n = w.shape[1]
    tm = _pick_tile(m, tm)
    tn = _pick_tile(n, tn)
    return pl.pallas_call(
        _mm_kernel,
        grid=(n // tn, m // tm),
        in_specs=[pl.BlockSpec((tm, k), lambda j, i: (i, 0)),
                  pl.BlockSpec((k, tn), lambda j, i: (0, j))],
        out_specs=pl.BlockSpec((tm, tn), lambda j, i: (i, j)),
        out_shape=jax.ShapeDtypeStruct((m, n), out_dtype),
        compiler_params=_cparams("parallel", "parallel"),
        name="matmul",
    )(a, w)


def _merge_kernel(oa_ref, ob_ref, oc_ref, wa_ref, wb_ref, wc_ref, ga_ref, gb_ref, gc_ref, o_ref):
    acc = jax.nn.sigmoid(ga_ref[...]) * jnp.dot(oa_ref[...], wa_ref[...], preferred_element_type=F32)
    acc += jax.nn.sigmoid(gb_ref[...]) * jnp.dot(ob_ref[...], wb_ref[...], preferred_element_type=F32)
    acc += jax.nn.sigmoid(gc_ref[...]) * jnp.dot(oc_ref[...], wc_ref[...], preferred_element_type=F32)
    o_ref[...] = acc.astype(o_ref.dtype)


def _merge_call(o_a, o_b, o_c, w_a, w_b, w_c, zg, tm=512, tn=512):
    n_tok = o_a.shape[0]
    d = w_a.shape[1]
    tm = _pick_tile(n_tok, tm)
    tn = _pick_tile(d, tn)
    nb = d // tn

    def act(o):
        return pl.BlockSpec((tm, o.shape[1]), lambda j, i: (i, 0))

    def wgt(w):
        return pl.BlockSpec((w.shape[0], tn), lambda j, i: (0, j))

    def gate(br):
        return pl.BlockSpec((tm, tn), lambda j, i: (i, j + br * nb))

    return pl.pallas_call(
        _merge_kernel,
        grid=(nb, n_tok // tm),
        in_specs=[act(o_a), act(o_b), act(o_c), wgt(w_a), wgt(w_b), wgt(w_c), gate(0), gate(1), gate(2)],
        out_specs=pl.BlockSpec((tm, tn), lambda j, i: (i, j)),
        out_shape=jax.ShapeDtypeStruct((n_tok, d), BF16),
        compiler_params=_cparams("parallel", "parallel"),
        name="branch_merge",
    )(o_a, o_b, o_c, w_a, w_b, w_c, zg, zg, zg)


def _topk_rows(s, payload, k):
    n, t = s.shape
    iota = lax.broadcasted_iota(jnp.int32, (n, t), 0).astype(F32)
    riota = lax.broadcasted_iota(jnp.int32, (k, t), 0)

    def body(r, carry):
        s, tv, tp = carry
        m = jnp.max(s, axis=0, keepdims=True)
        idx = jnp.min(jnp.where(s == m, iota, float(n)), axis=0, keepdims=True)
        sel = iota == idx
        p = jnp.max(jnp.where(sel, payload, -1.0), axis=0, keepdims=True)
        s = jnp.where(sel, -jnp.inf, s)
        tv = jnp.where(riota == r, m, tv)
        tp = jnp.where(riota == r, p, tp)
        return s, tv, tp

    _, tv, tp = lax.fori_loop(0, k, body, (s, jnp.zeros((k, t), F32), jnp.zeros((k, t), F32)))
    return tv, tp


def _peer_route_kernel(x_ref, wq_ref, keys_ref, e_ref, g_ref):
    t = x_ref.shape[0]
    q = jnp.dot(x_ref[...], wq_ref[...], preferred_element_type=F32).astype(BF16)
    half = q.shape[1] // 2
    key_iota = lax.broadcasted_iota(jnp.int32, (P_KEYS, t), 0).astype(F32)
    tops = []
    for p in range(2):
        s_t = lax.dot_general(keys_ref[0, p], q[:, p * half:(p + 1) * half],
                              (((1,), (1,)), ((), ())), preferred_element_type=F32)
        tops.append(_topk_rows(s_t, key_iota, P_TOPK))
    (s0, i0), (s1, i1) = tops
    cand = jnp.concatenate([jnp.broadcast_to(s0[a:a + 1], (P_TOPK, t)) + s1 for a in range(P_TOPK)], axis=0)
    cid = jnp.concatenate([jnp.broadcast_to(i0[a:a + 1], (P_TOPK, t)) * float(P_KEYS) + i1
                           for a in range(P_TOPK)], axis=0)
    best_s, best_e = _topk_rows(cand, cid, P_TOPK)
    ex = jnp.exp(best_s - jnp.max(best_s, axis=0, keepdims=True))
    g_ref[0] = ex / jnp.sum(ex, axis=0, keepdims=True)
    e_ref[0] = best_e.astype(jnp.int32)


def _peer_route_call(xn, wq, keys, tt=512):
    n_tok, d = xn.shape
    heads = keys.shape[0]
    qd = wq.shape[1] // heads
    tt = _pick_tile(n_tok, tt)
    return pl.pallas_call(
        _peer_route_kernel,
        grid=(n_tok // tt, heads),
        in_specs=[pl.BlockSpec((tt, d), lambda i, h: (i, 0)),
                  pl.BlockSpec((d, qd), lambda i, h: (0, h)),
                  pl.BlockSpec((1,) + keys.shape[1:], lambda i, h: (h, 0, 0, 0))],
        out_specs=(pl.BlockSpec((1, P_TOPK, tt), lambda i, h: (h, 0, i)),
                   pl.BlockSpec((1, P_TOPK, tt), lambda i, h: (h, 0, i))),
        out_shape=(jax.ShapeDtypeStruct((heads, P_TOPK, n_tok), jnp.int32),
                   jax.ShapeDtypeStruct((heads, P_TOPK, n_tok), F32)),
        compiler_params=_cparams("parallel", "parallel"),
        name="peer_route",
    )(xn, wq, keys)


def _peer_dense_kernel(x_ref, e_ref, g_ref, u_ref, v_ref, o_ref, gmat_ref, *, rows_per_step):
    t = x_ref.shape[0]
    j = pl.program_id(1)

    @pl.when(j == 0)
    def _build_gate_matrix():
        sub = lax.broadcasted_iota(jnp.int32, (P_KEYS, e_ref.shape[1]), 0)

        def per_token(tok, carry):
            e = e_ref[pl.ds(tok, 1), :]
            g = g_ref[pl.ds(tok, 1), :]
            g_hi = g.astype(BF16).astype(F32)
            g_lo = g - g_hi
            hit_a = sub == (e >> 7)
            hit_b = sub == (e & (P_KEYS - 1))
            a_mat = jnp.concatenate([jnp.where(hit_a, g_hi, 0.0), jnp.where(hit_a, g_lo, 0.0)], axis=1).astype(BF16)
            b_one = jnp.where(hit_b, 1.0, 0.0)
            b_mat = jnp.concatenate([b_one, b_one], axis=1).astype(BF16)
            gm = lax.dot_general(a_mat, b_mat, (((1,), (1,)), ((), ())), preferred_element_type=F32)
            gmat_ref[pl.ds(pl.multiple_of(tok * P_KEYS, P_KEYS), P_KEYS), :] = gm
            return carry

        lax.fori_loop(0, t, per_token, 0)
        o_ref[...] = jnp.zeros_like(o_ref)

    s = lax.dot_general(x_ref[...], u_ref[...], (((1,), (1,)), ((), ())), preferred_element_type=F32)
    act = jax.nn.gelu(s)
    gsel = jnp.concatenate(
        [gmat_ref[pl.ds(j * rows_per_step + r, t, stride=P_KEYS), :] for r in range(rows_per_step)], axis=1)
    w = (gsel * act).astype(BF16)
    o_ref[...] += jnp.dot(w, v_ref[...], preferred_element_type=F32)


def _peer_dense_call(xn, e_nat, g_nat, u_tab, v_tab, tt=256, rows_per_step=2):
    n_tok, d = xn.shape
    n_exp = u_tab.shape[0]
    r = e_nat.shape[1]
    tt = _pick_tile(n_tok, tt)
    eb = rows_per_step * P_KEYS
    return pl.pallas_call(
        functools.partial(_peer_dense_kernel, rows_per_step=rows_per_step),
        grid=(n_tok // tt, n_exp // eb),
        in_specs=[pl.BlockSpec((tt, d), lambda i, j: (i, 0)),
                  pl.BlockSpec((tt, r), lambda i, j: (i, 0)),
                  pl.BlockSpec((tt, r), lambda i, j: (i, 0)),
                  pl.BlockSpec((eb, d), lambda i, j: (j, 0)),
                  pl.BlockSpec((eb, d), lambda i, j: (j, 0))],
        out_specs=pl.BlockSpec((tt, d), lambda i, j: (i, 0)),
        out_shape=jax.ShapeDtypeStruct((n_tok, d), F32),
        scratch_shapes=[pltpu.VMEM((tt * P_KEYS, P_KEYS), F32)],
        compiler_params=_cparams("parallel", "arbitrary"),
        name="peer_dense",
    )(xn, e_nat, g_nat, u_tab, v_tab)


def _peer(xn, wq, keys, u_tab, v_tab):
    e_t, g_t = _peer_route_call(xn, wq, keys)
    heads = keys.shape[0]
    n_tok = xn.shape[0]
    e_nat = e_t.reshape(heads * P_TOPK, n_tok).T
    g_nat = g_t.reshape(heads * P_TOPK, n_tok).T
    return _peer_dense_call(xn, e_nat, g_nat, u_tab, v_tab)


def _split_bf16(x):
    hi = x.astype(BF16)
    lo = (x - hi.astype(F32)).astype(BF16)
    return hi, lo


def _dot3(a, b):
    ah, al = _split_bf16(a)
    bh, bl = _split_bf16(b)
    return (jnp.dot(ah, bh, preferred_element_type=F32) + jnp.dot(ah, bl, preferred_element_type=F32)
            + jnp.dot(al, bh, preferred_element_type=F32))


_CONV_FILL_ROWS = 512


def _conv_pad_rows(cols):
    return -(-(cols + 1) // V7X_SUBLANES) * V7X_SUBLANES


def _conv_fill(x_ref, x0_s, xm_s, xq_s, *, l, cols, pad):
    tc = x0_s.shape[1]
    zeros = jnp.zeros((pad, tc), F32)
    for s in (x0_s, xm_s, xq_s):
        s[0:pad, :] = zeros
        s[pad + l:pad + l + pad, :] = zeros
    step = min(_CONV_FILL_ROWS, l)
    for s0 in range(0, l, step):
        x0_s[pad + s0:pad + s0 + step, :] = x_ref[s0:s0 + step, :]
    col = lax.broadcasted_iota(jnp.int32, (step, tc), 0) & (cols - 1)
    for s0 in range(0, l, step):
        xm_s[pad + s0:pad + s0 + step, :] = jnp.where(col >= 1, x0_s[pad + s0 - 1:pad + s0 - 1 + step, :], 0.0)
        xq_s[pad + s0:pad + s0 + step, :] = jnp.where(col <= cols - 2, x0_s[pad + s0 + 1:pad + s0 + 1 + step, :], 0.0)


def _conv_rows(x0_s, xm_s, xq_s, w_ref, t0, n, *, rows, cols, pad):
    acc = None
    for dr in ((-1, 0, 1) if rows > 1 else (0,)):
        base = pl.multiple_of(t0 + pad + dr * cols, V7X_SUBLANES)
        for dc, src in ((-1, xm_s), (0, x0_s), (1, xq_s)):
            tap = (dr + 1) * 3 + dc + 1
            term = src[pl.ds(base, n), :] * w_ref[tap:tap + 1, :]
            acc = term if acc is None else acc + term
    return acc


_CONV_CHUNK = 128


def _conv_a_kernel(z_ref, w_ref, o_ref, x0_s, xm_s, xq_s, *, l, rows, cols, pad, blocks_per_part):
    _conv_fill(z_ref.at[0], x0_s, xm_s, xq_s, l=l, cols=cols, pad=pad)
    part = pl.program_id(1) // blocks_per_part
    use_norm = part != 1
    post = jnp.where(part == 2, A_HEAD_DIM ** -0.5, 1.0)
    tc = o_ref.shape[2]

    def chunk(i, carry):
        t0 = pl.multiple_of(i * _CONV_CHUNK, _CONV_CHUNK)
        y = _conv_rows(x0_s, xm_s, xq_s, w_ref, t0, _CONV_CHUNK, rows=rows, cols=cols, pad=pad)
        y = y * jax.nn.sigmoid(y)
        outs = []
        for h0 in range(0, tc, A_HEAD_DIM):
            yh = y[:, h0:h0 + A_HEAD_DIM]
            inv = lax.rsqrt(jnp.sum(yh * yh, axis=-1, keepdims=True) + RMS_EPS) * post
            outs.append(yh * jnp.where(use_norm, inv, 1.0))
        o_ref[0, pl.ds(t0, _CONV_CHUNK), :] = jnp.concatenate(outs, axis=1) if len(outs) > 1 else outs[0]
        return carry

    lax.fori_loop(0, l // _CONV_CHUNK, chunk, 0)


def _conv_a_call(z_kvq, w9, rows, cols, tc=256):
    b, l, ch = z_kvq.shape
    pad = _conv_pad_rows(cols)
    scr = pltpu.VMEM((l + 2 * pad, tc), F32)
    return pl.pallas_call(
        functools.partial(_conv_a_kernel, l=l, rows=rows, cols=cols, pad=pad, blocks_per_part=ch // 3 // tc),
        grid=(b, ch // tc),
        in_specs=[pl.BlockSpec((1, l, tc), lambda i, j: (i, 0, j)),
                  pl.BlockSpec((9, tc), lambda i, j: (0, j))],
        out_specs=pl.BlockSpec((1, l, tc), lambda i, j: (i, 0, j)),
        out_shape=jax.ShapeDtypeStruct((b, l, ch), F32),
        scratch_shapes=[scr, scr, scr],
        compiler_params=_cparams("parallel", "parallel"),
        name="conv_a",
    )(z_kvq, w9)


def _conv_c_kernel(z0_ref, z1_ref, z2_ref, w0_ref, w1_ref, w2_ref, x0_ref, u_ref, *scr, l, rows, cols, pad):
    parts = ((z0_ref, w0_ref, scr[0:3]), (z1_ref, w1_ref, scr[3:6]), (z2_ref, w2_ref, scr[6:9]))
    for z_ref, _, s in parts:
        _conv_fill(z_ref.at[0], *s, l=l, cols=cols, pad=pad)

    def chunk(i, carry):
        t0 = pl.multiple_of(i * _CONV_CHUNK, _CONV_CHUNK)
        y = [_conv_rows(*s, w_ref, t0, _CONV_CHUNK, rows=rows, cols=cols, pad=pad) for _, w_ref, s in parts]
        x0_ref[0, pl.ds(t0, _CONV_CHUNK), :] = y[0]
        u_ref[0, pl.ds(t0, _CONV_CHUNK), :] = y[1] * y[2]
        return carry

    lax.fori_loop(0, l // _CONV_CHUNK, chunk, 0)


def _conv_c_call(z_c, w9, rows, cols, tc=128):
    b, l, ch = z_c.shape
    cw = ch // 3
    nb = cw // tc
    pad = _conv_pad_rows(cols)
    scr = pltpu.VMEM((l + 2 * pad, tc), F32)
    zspec = [pl.BlockSpec((1, l, tc), functools.partial(lambda i, j, p: (i, 0, j + p * nb), p=p)) for p in range(3)]
    wspec = [pl.BlockSpec((9, tc), functools.partial(lambda i, j, p: (0, j + p * nb), p=p)) for p in range(3)]
    ospec = pl.BlockSpec((1, l, tc), lambda i, j: (i, 0, j))
    return pl.pallas_call(
        functools.partial(_conv_c_kernel, l=l, rows=rows, cols=cols, pad=pad),
        grid=(b, nb),
        in_specs=zspec + wspec,
        out_specs=(ospec, ospec),
        out_shape=(jax.ShapeDtypeStruct((b, l, cw), F32), jax.ShapeDtypeStruct((b, l, cw), F32)),
        scratch_shapes=[scr] * 9,
        compiler_params=_cparams("parallel", "parallel"),
        name="conv_c",
    )(z_c, z_c, z_c, w9, w9, w9)


def _gate_prep_kernel(z_ref, alog_ref, dtb_ref, o_ref, *, l, heads):
    lanes = z_ref.shape[2]
    ii = lax.broadcasted_iota(jnp.int32, (A_CHUNK, A_CHUNK), 0)
    jj = lax.broadcasted_iota(jnp.int32, (A_CHUNK, A_CHUNK), 1)
    lower = jnp.where(ii >= jj, 1.0, 0.0)
    upper = jnp.where(ii <= jj, 1.0, 0.0)
    lane = lax.broadcasted_iota(jnp.int32, (A_CHUNK, lanes), 1)

    def chunk(n, carry):
        r0 = pl.multiple_of(n * A_CHUNK, A_CHUNK)
        z = z_ref[0, pl.ds(r0, A_CHUNK), :]
        beta = jax.nn.sigmoid(z)
        x = z + dtb_ref[...]
        softplus = jnp.maximum(x, 0.0) + jnp.log(1.0 + jnp.exp(-jnp.abs(x)))
        la = -jnp.exp(alog_ref[...]) * softplus
        pre = _dot3(lower, la)
        suf = _dot3(upper, la)
        o_ref[0, pl.ds(r0, A_CHUNK), :] = jnp.where(lane < 2 * heads, beta, jnp.where(lane < 3 * heads, pre, suf))
        return carry

    lax.fori_loop(0, l // A_CHUNK, chunk, 0)


def _gate_prep_call(z_ba, a_log, a_dt_bias):
    b, l, lanes = z_ba.shape
    heads = a_log.shape[1]
    pad = lambda p: jnp.pad(p.reshape(1, 2 * heads).astype(F32), ((0, 0), (2 * heads, lanes - 4 * heads)))
    return pl.pallas_call(
        functools.partial(_gate_prep_kernel, l=l, heads=heads),
        grid=(b,),
        in_specs=[pl.BlockSpec((1, l, lanes), lambda i: (i, 0, 0)),
                  pl.BlockSpec((1, lanes), lambda i: (0, 0)),
                  pl.BlockSpec((1, lanes), lambda i: (0, 0))],
        out_specs=pl.BlockSpec((1, l, lanes), lambda i: (i, 0, 0)),
        out_shape=jax.ShapeDtypeStruct((b, l, lanes), F32),
        compiler_params=_cparams("parallel"),
        name="gate_prep",
    )(z_ba, pad(a_log), pad(a_dt_bias))


_TRI_BASE_LOG2 = 3


def _unit_tri_inverse(a):
    c = a.shape[0]
    ii = lax.broadcasted_iota(jnp.int32, (c, c), 0)
    jj = lax.broadcasted_iota(jnp.int32, (c, c), 1)
    eye = jnp.where(ii == jj, 1.0, 0.0)
    p = -jnp.where((ii >> _TRI_BASE_LOG2) == (jj >> _TRI_BASE_LOG2), a, 0.0)
    t = eye + p
    span = 2
    while span < (1 << _TRI_BASE_LOG2):
        p = _dot3(p, p)
        t = t + _dot3(t, p)
        span *= 2
    log2 = _TRI_BASE_LOG2
    while (1 << log2) < c:
        pair = jnp.where((ii >> (log2 + 1)) == (jj >> (log2 + 1)), a, 0.0)
        cross = jnp.where((ii >> log2) == (jj >> log2), 0.0, pair)
        t = t - _dot3(t, _dot3(cross, t))
        log2 += 1
    return t


_NT = (((1,), (1,)), ((), ()))
_TN = (((0,), (0,)), ((), ()))


def _delta_kernel(k_ref, v_ref, q_ref, col_ref, row_ref, s0_ref, og_ref, an_ref, o_ref, sfin_ref,
                  u_s, w_s, ke_s, qd_s, qk_s, eg_s, st_s, oacc_s, *, l):
    c = A_CHUNK
    dk = A_HEAD_DIM
    n_chunks = l // c
    ii = lax.broadcasted_iota(jnp.int32, (c, c), 0)
    jj = lax.broadcasted_iota(jnp.int32, (c, c), 1)
    incl = (ii >= jj, ii <= jj)
    strict = (ii > jj, ii < jj)

    def pass1(n, carry):
        r0 = pl.multiple_of(n * c, c)
        kc = k_ref[0, pl.ds(r0, c), :]
        vc = v_ref[0, pl.ds(r0, c), :]
        qc = q_ref[0, pl.ds(r0, c), :]
        kb = kc.astype(BF16)
        kk = lax.dot_general(kb, kb, _NT, preferred_element_type=F32)
        qk = lax.dot_general(qc.astype(BF16), kb, _NT, preferred_element_type=F32)
        cols = col_ref[0, 0, pl.ds(r0, c), :]
        rows = row_ref[0, 0, n]
        for d in range(2):
            beta = cols[:, d:d + 1]
            gcol = cols[:, 2 + d:3 + d]
            grow = rows[d:d + 1, :]
            diff = gcol - grow
            dec = jnp.where(incl[d], jnp.exp(jnp.where(incl[d], diff, 0.0)), 0.0)
            a = jnp.where(strict[d], kk * dec * beta, 0.0)
            t = _unit_tri_inverse(a)
            eg = jnp.exp(gcol)
            rhs = jnp.concatenate([vc * beta, kc * (beta * eg)], axis=1)
            sol = _dot3(t, rhs)
            glast = gcol[c - 1:c, :] if d == 0 else gcol[0:1, :]
            u_s[d, pl.ds(r0, c), :] = sol[:, :dk]
            w_s[d, pl.ds(r0, c), :] = sol[:, dk:].astype(BF16)
            ke_s[d, pl.ds(r0, c), :] = (kc * jnp.exp(glast - gcol)).astype(BF16)
            qd_s[d, pl.ds(r0, c), :] = (qc * eg).astype(BF16)
            qk_s[d, pl.ds(r0, c), :] = (qk * dec).astype(BF16)
            eg_s[d * n_chunks + n] = jnp.broadcast_to(jnp.exp(glast), (V7X_SUBLANES, dk))
        return carry

    lax.fori_loop(0, n_chunks, pass1, 0)

    st_s[...] = s0_ref[0, 0]
    oacc_s[...] = jnp.zeros_like(oacc_s)

    def pass2(n, carry):
        for d in range(2):
            ch = n if d == 0 else n_chunks - 1 - n
            r0 = pl.multiple_of(ch * c, c)
            s = st_s[d]
            lhs = jnp.concatenate([w_s[d, pl.ds(r0, c), :], qd_s[d, pl.ds(r0, c), :]], axis=0)
            ws = jnp.dot(lhs, s.astype(BF16), preferred_element_type=F32)
            v_new = u_s[d, pl.ds(r0, c), :] - ws[:c]
            vb = v_new.astype(BF16)
            o = ws[c:] + jnp.dot(qk_s[d, pl.ds(r0, c), :], vb, preferred_element_type=F32)
            oacc_s[pl.ds(r0, c), :] += o
            upd = lax.dot_general(ke_s[d, pl.ds(r0, c), :], vb, _TN, preferred_element_type=F32)
            st_s[d] = s * eg_s[d * n_chunks + ch][0:1, :] + upd
        return carry

    lax.fori_loop(0, n_chunks, pass2, 0)
    sfin_ref[0, 0] = st_s[...]

    rows_out = min(256, l)

    def finish(i, carry):
        r0 = pl.multiple_of(i * rows_out, rows_out)
        o = oacc_s[pl.ds(r0, rows_out), :]
        o = o * lax.rsqrt(jnp.mean(o * o, axis=-1, keepdims=True) + RMS_EPS) * an_ref[...]
        og = og_ref[0, pl.ds(r0, rows_out), :]
        o_ref[0, pl.ds(r0, rows_out), :] = (o * (og * jax.nn.sigmoid(og))).astype(o_ref.dtype)
        return carry

    lax.fori_loop(0, l // rows_out, finish, 0)


def _delta_call(act, z_og, gates, s0, a_norm):
    b, l, ch = act.shape
    heads = ch // 3 // A_HEAD_DIM
    dk = A_HEAD_DIM
    c = A_CHUNK
    n_chunks = l // c
    g4 = jnp.stack([gates[..., i * heads:(i + 1) * heads] for i in range(4)], axis=-1)
    col = jnp.pad(jnp.transpose(g4, (0, 2, 1, 3)), ((0, 0), (0, 0), (0, 0), (0, 4)))
    row = jnp.transpose(g4[..., 2:4].reshape(b, n_chunks, c, heads, 2), (0, 3, 1, 4, 2))
    row = jnp.pad(row, ((0, 0), (0, 0), (0, 0), (0, 6), (0, 0)))
    tok = lambda off: pl.BlockSpec((1, l, dk), functools.partial(lambda i, h, off: (i, 0, h + off), off=off))
    return pl.pallas_call(
        functools.partial(_delta_kernel, l=l),
        grid=(b, heads),
        in_specs=[tok(0), tok(heads), tok(2 * heads),
                  pl.BlockSpec((1, 1, l, 8), lambda i, h: (i, h, 0, 0)),
                  pl.BlockSpec((1, 1, n_chunks, 8, c), lambda i, h: (i, h, 0, 0, 0)),
                  pl.BlockSpec((1, 1, 2, dk, dk), lambda i, h: (i, h, 0, 0, 0)),
                  tok(0),
                  pl.BlockSpec((1, dk), lambda i, h: (0, 0))],
        out_specs=(tok(0), pl.BlockSpec((1, 1, 2, dk, dk), lambda i, h: (i, h, 0, 0, 0))),
        out_shape=(jax.ShapeDtypeStruct((b, l, heads * dk), BF16),
                   jax.ShapeDtypeStruct((b, heads, 2, dk, dk), F32)),
        scratch_shapes=[pltpu.VMEM((2, l, dk), F32), pltpu.VMEM((2, l, dk), BF16), pltpu.VMEM((2, l, dk), BF16),
                        pltpu.VMEM((2, l, dk), BF16), pltpu.VMEM((2, l, c), BF16),
                        pltpu.VMEM((2 * n_chunks, V7X_SUBLANES, dk), F32), pltpu.VMEM((2, dk, dk), F32),
                        pltpu.VMEM((l, dk), F32)],
        compiler_params=_cparams("parallel", "parallel"),
        name="delta_rule",
    )(act, act, act, col, row, s0, z_og, a_norm.reshape(1, dk).astype(F32))


def _gmlp_kernel(zu_ref, zv_ref, ws_ref, bs_ref, o_ref):
    groups = ws_ref.shape[0]
    gd = zu_ref.shape[1] // groups
    u = jax.nn.gelu(zu_ref[...])
    v = jax.nn.gelu(zv_ref[...])
    mu = jnp.mean(v, axis=-1, keepdims=True)
    var = jnp.mean(jnp.square(v - mu), axis=-1, keepdims=True)
    vn = ((v - mu) * lax.rsqrt(var + RMS_EPS)).astype(BF16)
    mixed = [jnp.dot(ws_ref[g], vn[:, g * gd:(g + 1) * gd], preferred_element_type=F32) + bs_ref[:, g:g + 1]
             for g in range(groups)]
    o_ref[...] = (u * jnp.concatenate(mixed, axis=1)).astype(o_ref.dtype)


def _gmlp_call(z_b, w_s, b_s):
    n_tok, two_w = z_b.shape
    bw = two_w // 2
    groups, p, _ = w_s.shape
    return pl.pallas_call(
        _gmlp_kernel,
        grid=(n_tok // p,),
        in_specs=[pl.BlockSpec((p, bw), lambda i: (i, 0)),
                  pl.BlockSpec((p, bw), lambda i: (i, 1)),
                  pl.BlockSpec((groups, p, p), lambda i: (0, 0, 0)),
                  pl.BlockSpec((p, groups), lambda i: (0, 0))],
        out_specs=pl.BlockSpec((p, bw), lambda i: (i, 0)),
        out_shape=jax.ShapeDtypeStruct((n_tok, bw), BF16),
        compiler_params=_cparams("parallel"),
        name="gmlp",
    )(z_b, z_b, w_s, b_s.T.astype(F32))


_FEAT_PAD = 64


def _filter_feats(l):
    pos = np.arange(l, dtype=np.float32)
    t = np.linspace(0.0, 1.0, l, dtype=np.float32)[:, None]
    bands = np.linspace(1e-4, C_POS_BANDS - 1, C_POS_BANDS, dtype=np.float32)
    ang = np.float32(2.0 * math.pi / l) * pos[:, None] * bands
    feats = np.concatenate([t, np.cos(ang), -np.sin(ang)], axis=-1).astype(np.float32)
    return np.pad(feats, ((0, 0), (0, _FEAT_PAD - feats.shape[1])))


def _filter_kernel(feat_ref, w1_ref, b1_ref, f1_ref, w2_ref, b2_ref, f2_ref, w3_ref, b3_ref, rate_ref, o_ref):
    feats = feat_ref[...]
    hid = jnp.sin(f1_ref[...] * (_dot3(feats, w1_ref[...]) + b1_ref[...]))
    hid = jnp.sin(f2_ref[...] * (_dot3(hid, w2_ref[...]) + b2_ref[...]))
    filt = _dot3(hid, w3_ref[...]) + b3_ref[...]
    filt = filt * jnp.exp(-feats[:, 0:1] * rate_ref[...])
    o_ref[...] = filt * lax.rsqrt(jnp.sum(filt * filt, axis=0, keepdims=True) + RMS_EPS)


def _filter_call(l, fw1, fb1, freq1, fw2, fb2, freq2, fw3, fb3, tc=256):
    hidden = fw1.shape[1]
    two_c = fw3.shape[1]
    cw = two_c // 2
    feats = jnp.asarray(_filter_feats(l))
    w1 = jnp.pad(fw1.astype(F32), ((0, _FEAT_PAD - fw1.shape[0]), (0, 0)))
    rate = np.abs(np.linspace(C_MIN_DECAY, C_MAX_DECAY, cw, dtype=np.float32))
    rate = jnp.asarray(np.concatenate([rate, rate]).reshape(1, two_c))
    row = lambda v: v.reshape(1, -1).astype(F32)
    full = lambda shape: pl.BlockSpec(shape, lambda j: (0, 0))
    return pl.pallas_call(
        _filter_kernel,
        grid=(two_c // tc,),
        in_specs=[full((l, _FEAT_PAD)), full((_FEAT_PAD, hidden)), full((1, hidden)), full((1, hidden)),
                  full((hidden, hidden)), full((1, hidden)), full((1, hidden)),
                  pl.BlockSpec((hidden, tc), lambda j: (0, j)), pl.BlockSpec((1, tc), lambda j: (0, j)),
                  pl.BlockSpec((1, tc), lambda j: (0, j))],
        out_specs=pl.BlockSpec((l, tc), lambda j: (0, j)),
        out_shape=jax.ShapeDtypeStruct((l, two_c), F32),
        compiler_params=_cparams("parallel"),
        name="hyena_filter",
    )(feats, w1, row(fb1), row(freq1), fw2.astype(F32), row(fb2), row(freq2), fw3.astype(F32), row(fb3), rate)


def _dft_sizes(l):
    n1 = 64 if l >= 4096 else 32
    return n1, (2 * l) // n1


def _hi_lo(x):
    x = jnp.asarray(x, F32)
    hi = x.astype(BF16)
    return hi, (x - hi.astype(F32)).astype(BF16)


def _dft_consts(l):
    n1, n2 = _dft_sizes(l)
    n = n1 * n2
    half = n1 // 2
    k1 = np.arange(n1)[None, :, None]
    m1 = np.arange(half)[None, None, :]
    m2 = np.arange(n2)[:, None, None]
    f1 = np.exp(-2j * np.pi * (m1 * k1 / n1 + m2 * k1 / n))
    fwd = np.concatenate([f1.real, f1.imag], axis=1)
    fh, fl = _hi_lo(fwd)
    fwd1 = jnp.concatenate([fh, fh, fl], axis=2)
    rt = np.transpose(f1.real, (0, 2, 1)) / n
    it = np.transpose(f1.imag, (0, 2, 1)) / n
    (rh, rl), (ih, il) = _hi_lo(rt), _hi_lo(it)
    inv1 = jnp.concatenate([rh, ih, rh, ih, rl, il], axis=2)
    kk = np.arange(n2)
    f2 = np.exp(-2j * np.pi * np.outer(kk, kk) / n2)
    m = np.block([[f2.real, -f2.imag], [f2.imag, f2.real]])
    mi = np.block([[f2.real, f2.imag], [-f2.imag, f2.real]])
    (mh, ml), (mih, mil) = _hi_lo(m), _hi_lo(mi)
    return fwd1, jnp.concatenate([mh, mh, ml], axis=1), jnp.concatenate([mih, mih, mil], axis=1), inv1


def _stack3(x):
    hi, lo = _split_bf16(x)
    return jnp.concatenate([hi, lo, hi], axis=0)


def _dft_level1(x_ref, fwd1_ref, ar_s, ai_s, n1, n2):
    half = n1 // 2

    def body(j, carry):
        xs = x_ref[pl.ds(j, half, stride=n2), :]
        a = jnp.dot(fwd1_ref[j], _stack3(xs), preferred_element_type=F32)
        ar_s[pl.ds(j, n1, stride=n2), :] = a[:n1]
        ai_s[pl.ds(j, n1, stride=n2), :] = a[n1:]
        return carry

    lax.fori_loop(0, n2, body, 0)


def _dft_level2(ar_s, ai_s, m3_ref, k1, n2):
    r0 = pl.multiple_of(k1 * n2, n2)
    blk = jnp.concatenate([ar_s[pl.ds(r0, n2), :], ai_s[pl.ds(r0, n2), :]], axis=0)
    x = jnp.dot(m3_ref[...], _stack3(blk), preferred_element_type=F32)
    return x[:n2], x[n2:]


def _spectrum_kernel(hf_ref, hb_ref, fwd1_ref, m3_ref, o_ref, ar_s, ai_s, *, n1, n2):
    for which, h_ref in enumerate((hf_ref, hb_ref)):
        _dft_level1(h_ref, fwd1_ref, ar_s, ai_s, n1, n2)

        def body(k1, carry):
            xr, xi = _dft_level2(ar_s, ai_s, m3_ref, k1, n2)
            rows = pl.ds(pl.multiple_of(k1 * n2, n2), n2)
            if which == 0:
                o_ref[0, rows, :] = xr
                o_ref[1, rows, :] = xi
            else:
                o_ref[0, rows, :] += xr
                o_ref[1, rows, :] -= xi
            return carry

        lax.fori_loop(0, n1, body, 0)


def _spectrum_call(filt, consts, tc=128):
    l, two_c = filt.shape
    cw = two_c // 2
    nb = cw // tc
    n1, n2 = _dft_sizes(l)
    fwd1, m3, _, _ = consts
    return pl.pallas_call(
        functools.partial(_spectrum_kernel, n1=n1, n2=n2),
        grid=(nb,),
        in_specs=[pl.BlockSpec((l, tc), lambda j: (0, j)),
                  pl.BlockSpec((l, tc), lambda j: (0, j + nb)),
                  pl.BlockSpec(fwd1.shape, lambda j: (0, 0, 0)),
                  pl.BlockSpec(m3.shape, lambda j: (0, 0))],
        out_specs=pl.BlockSpec((2, 2 * l, tc), lambda j: (0, 0, j)),
        out_shape=jax.ShapeDtypeStruct((2, 2 * l, cw), F32),
        scratch_shapes=[pltpu.VMEM((2 * l, tc), F32), pltpu.VMEM((2 * l, tc), F32)],
        compiler_params=_cparams("parallel"),
        name="hyena_spectrum",
    )(filt, filt, fwd1, m3)


def _longconv_kernel(u_ref, x0_ref, h_ref, skip_ref, fwd1_ref, m3_ref, mi3_ref, inv1_ref, o_ref,
                     ar_s, ai_s, y_s, *, l, n1, n2):
    half = n1 // 2
    _dft_level1(u_ref.at[0], fwd1_ref, ar_s, ai_s, n1, n2)

    def per_k1(k1, carry):
        xr, xi = _dft_level2(ar_s, ai_s, m3_ref, k1, n2)
        rows = pl.ds(pl.multiple_of(k1 * n2, n2), n2)
        hr = h_ref[0, rows, :]
        hi = h_ref[1, rows, :]
        y = jnp.concatenate([xr * hr - xi * hi, xr * hi + xi * hr], axis=0)
        b = jnp.dot(mi3_ref[...], _stack3(y), preferred_element_type=F32)
        ar_s[rows, :] = b[:n2]
        ai_s[rows, :] = b[n2:]
        return carry

    lax.fori_loop(0, n1, per_k1, 0)

    def per_n2(j, carry):
        br = ar_s[pl.ds(j, n1, stride=n2), :]
        bi = ai_s[pl.ds(j, n1, stride=n2), :]
        (brh, brl), (bih, bil) = _split_bf16(br), _split_bf16(bi)
        rhs = jnp.concatenate([brh, bih, brl, bil, brh, bih], axis=0)
        y_s[pl.ds(j, half, stride=n2), :] = jnp.dot(inv1_ref[j], rhs, preferred_element_type=F32)
        return carry

    lax.fori_loop(0, n2, per_n2, 0)

    rows_out = min(512, l)

    def finish(i, carry):
        rows = pl.ds(pl.multiple_of(i * rows_out, rows_out), rows_out)
        u = u_ref[0, rows, :]
        o_ref[0, rows, :] = (x0_ref[0, rows, :] * (y_s[rows, :] + skip_ref[...] * u)).astype(o_ref.dtype)
        return carry

    lax.fori_loop(0, l // rows_out, finish, 0)


def _longconv_call(u, x0, spec, skip, consts, tc=128):
    b, l, cw = u.shape
    n1, n2 = _dft_sizes(l)
    fwd1, m3, mi3, inv1 = consts
    tok = pl.BlockSpec((1, l, tc), lambda j, i: (i, 0, j))
    const = lambda a: pl.BlockSpec(a.shape, lambda j, i: (0,) * a.ndim)
    return pl.pallas_call(
        functools.partial(_longconv_kernel, l=l, n1=n1, n2=n2),
        grid=(cw // tc, b),
        in_specs=[tok, tok,
                  pl.BlockSpec((2, 2 * l, tc), lambda j, i: (0, 0, j)),
                  pl.BlockSpec((1, tc), lambda j, i: (0, j)),
                  const(fwd1), const(m3), const(mi3), const(inv1)],
        out_specs=tok,
        out_shape=jax.ShapeDtypeStruct((b, l, cw), BF16),
        scratch_shapes=[pltpu.VMEM((2 * l, tc), F32), pltpu.VMEM((2 * l, tc), F32), pltpu.VMEM((l, tc), F32)],
        compiler_params=_cparams("parallel", "parallel"),
        name="hyena_longconv",
    )(u, x0, spec, skip.reshape(1, cw).astype(F32), fwd1, m3, mi3, inv1)


def _layer_weights(i, p):
    w_in = p["w_in"][i]
    aw, bw, cw = p["w_br_a"].shape[1], p["w_br_b"].shape[1], p["w_br_c"].shape[1]
    heads = p["a_log"].shape[2]
    off_ba = 2 * aw
    off_q = off_ba + 4 * heads
    off_og = off_q + aw
    off_bu = off_og + aw
    off_cx = off_bu + 2 * bw
    off_gate = off_cx + 3 * cw
    cast = lambda w: w.astype(BF16)
    return {
        "w_kvq": cast(jnp.concatenate([w_in[:, :off_ba], w_in[:, off_q:off_og]], axis=1)),
        "w_ba": cast(jnp.pad(w_in[:, off_ba:off_q], ((0, 0), (0, V7X_LANES - 4 * heads)))),
        "w_og": cast(w_in[:, off_og:off_bu]),
        "w_b": cast(w_in[:, off_bu:off_cx]),
        "w_c": cast(w_in[:, off_cx:off_gate]),
        "w_gate": cast(w_in[:, off_gate:]),
        "w_br_a": cast(p["w_br_a"][i]), "w_br_b": cast(p["w_br_b"][i]), "w_br_c": cast(p["w_br_c"][i]),
        "w_out": cast(p["w_out"][i]),
        "a_conv": p["a_conv"][i].reshape(9, -1), "c_conv": p["c_conv"][i].reshape(9, -1),
        "a_log": p["a_log"][i], "a_dt_bias": p["a_dt_bias"][i], "a_norm": p["a_norm"][i],
        "b_ws": cast(p["b_ws"][i]), "b_bs": p["b_bs"][i], "c_skip": p["c_skip"][i],
        "filter": tuple(p[k][i] for k in ("c_fw1", "c_fb1", "c_freq1", "c_fw2", "c_fb2", "c_freq2", "c_fw3", "c_fb3")),
        "p_wq": cast(p["p_wq"][i]), "p_keys": cast(p["p_keys"][i]), "p_u": cast(p["p_u"][i]), "p_v": cast(p["p_v"][i]),
    }


def _delta_branch(xf, b, l, rows, cols, s0, lw):
    z_kvq = _matmul(xf, lw["w_kvq"]).reshape(b, l, -1)
    z_og = _matmul(xf, lw["w_og"]).reshape(b, l, -1)
    z_ba = _matmul(xf, lw["w_ba"]).reshape(b, l, -1)
    act = _conv_a_call(z_kvq, lw["a_conv"], rows, cols)
    gates = _gate_prep_call(z_ba, lw["a_log"], lw["a_dt_bias"])
    o_a, s_fin = _delta_call(act, z_og, gates, s0, lw["a_norm"])
    return o_a.reshape(b * l, -1), s_fin


def _token_mixer(xn, rows, cols, s0, lw):
    b, l, d = xn.shape
    xf = xn.reshape(b * l, d)
    o_a, s_fin = _delta_branch(xf, b, l, rows, cols, s0, lw)
    o_b = _gmlp_call(_matmul(xf, lw["w_b"]), lw["b_ws"], lw["b_bs"])
    x0, u = _conv_c_call(_matmul(xf, lw["w_c"]).reshape(b, l, -1), lw["c_conv"], rows, cols)
    consts = _dft_consts(l)
    spec = _spectrum_call(_filter_call(l, *lw["filter"]), consts)
    o_c = _longconv_call(u, x0, spec, lw["c_skip"], consts).reshape(b * l, -1)
    merged = _merge_call(o_a, o_b, o_c, lw["w_br_a"], lw["w_br_b"], lw["w_br_c"], _matmul(xf, lw["w_gate"]))
    return _matmul(merged, lw["w_out"]).reshape(b, l, d), s_fin


def _peer_layer(xn, lw):
    b, l, d = xn.shape
    return _peer(xn.reshape(b * l, d), lw["p_wq"], lw["p_keys"], lw["p_u"], lw["p_v"]).reshape(b, l, d)


def kernel(x, c, ctx, c_ctx, ada_w, ada_b, w_in, a_conv, a_log, a_dt_bias, a_norm, b_ws, b_bs, c_conv, c_fw1, c_fb1,
           c_freq1, c_fw2, c_fb2, c_freq2, c_fw3, c_fb3, c_skip, w_br_a, w_br_b, w_br_c, w_out, p_wq, p_keys, p_u,
           p_v, final_norm):
    params = dict(w_in=w_in, a_conv=a_conv, a_log=a_log, a_dt_bias=a_dt_bias, a_norm=a_norm, b_ws=b_ws, b_bs=b_bs,
                  c_conv=c_conv, c_fw1=c_fw1, c_fb1=c_fb1, c_freq1=c_freq1, c_fw2=c_fw2, c_fb2=c_fb2, c_freq2=c_freq2,
                  c_fw3=c_fw3, c_fb3=c_fb3, c_skip=c_skip, w_br_a=w_br_a, w_br_b=w_br_b, w_br_c=w_br_c, w_out=w_out,
                  p_wq=p_wq, p_keys=p_keys, p_u=p_u, p_v=p_v)
    b, l, d = x.shape
    depth = w_in.shape[0]
    rows = l // GRID_W
    l_ctx = ctx.shape[1]
    heads = a_log.shape[2]
    s_zero = jnp.zeros((b, heads, 2, A_HEAD_DIM, A_HEAD_DIM), F32)
    cvec = jnp.concatenate([c, c_ctx[None, :], jnp.zeros((V7X_SUBLANES - b - 1, d), F32)], axis=0)

    h, h_pending = x, None
    hc, hc_pending = ctx, None
    for i in range(depth):
        lw = _layer_weights(i, params)
        mod_all = _mod_call(cvec, ada_w[i], ada_b[i])
        mod = mod_all[:b].reshape(b, N_MOD, 1, d)
        mod_c = mod_all[b].reshape(N_MOD, 1, 1, d)

        def norm(stream, pending, scale, shift):
            if pending is None:
                return stream, _norm_mod_call(stream, scale, shift)[1]
            return _norm_mod_call(stream, scale, shift, delta=pending[0], gate=pending[1], emit_h=True)

        hc, xnc = norm(hc, hc_pending, mod_c[1], mod_c[0])
        if i == depth - 1:
            _, s_ctx = _delta_branch(xnc.reshape(b * l_ctx, d), b, l_ctx, 1, l_ctx, s_zero, lw)
        else:
            out_c, s_ctx = _token_mixer(xnc, 1, l_ctx, s_zero, lw)
            hc, xnc2 = norm(hc, (out_c, mod_c[2]), mod_c[4], mod_c[3])
            hc_pending = (_peer_layer(xnc2, lw), mod_c[5])
        h, xn = norm(h, h_pending, mod[:, 1], mod[:, 0])
        out, _ = _token_mixer(xn, rows, GRID_W, s_ctx, lw)
        h, xn2 = norm(h, (out, mod[:, 2]), mod[:, 4], mod[:, 3])
        h_pending = (_peer_layer(xn2, lw), mod[:, 5])
    return _norm_mod_call(h, final_norm.reshape(1, 1, d), delta=h_pending[0], gate=h_pending[1], out_dtype=F32)[1]
```

```python
import functools
import math

import numpy as np
import jax
import jax.numpy as jnp
from jax import lax
from jax.experimental import pallas as pl
from jax.experimental.pallas import tpu as pltpu

F32 = jnp.float32
BF16 = jnp.bfloat16

V7X_LANES = 128
V7X_SUBLANES = 8
V7X_VMEM_LIMIT_BYTES = 56 * 1024 * 1024

GRID_W = 64
RMS_EPS = 1e-6
N_MOD = 6
A_HEAD_DIM = 128
A_CHUNK = 64
B_CHUNK = 128
C_POS_BANDS = 16
C_MIN_DECAY = math.log(1e-2) / 1.5
C_MAX_DECAY = math.log(1e-2) / 0.3
N_BRANCH = 3
P_KEYS = 128
P_TOPK = 16


def _cparams(*sem):
    return pltpu.CompilerParams(dimension_semantics=sem, vmem_limit_bytes=V7X_VMEM_LIMIT_BYTES)


def _mod_kernel(c_ref, w_ref, b_ref, o_ref):
    c = c_ref[...]
    a = (c * jax.nn.sigmoid(c)).astype(BF16)
    o_ref[...] = jnp.dot(a, w_ref[...].astype(BF16), preferred_element_type=F32) + b_ref[...]


def _mod_call(cvec, ada_w, ada_b):
    rows, d = cvec.shape
    n = ada_w.shape[1]
    tn = 1024
    return pl.pallas_call(
        _mod_kernel,
        grid=(n // tn,),
        in_specs=[pl.BlockSpec((rows, d), lambda j: (0, 0)),
                  pl.BlockSpec((d, tn), lambda j: (0, j)),
                  pl.BlockSpec((1, tn), lambda j: (0, j))],
        out_specs=pl.BlockSpec((rows, tn), lambda j: (0, j)),
        out_shape=jax.ShapeDtypeStruct((rows, n), F32),
        compiler_params=_cparams("parallel"),
        name="ada_mod",
    )(cvec, ada_w, ada_b.reshape(1, n))


def _norm_mod_kernel(*refs, with_delta, with_shift, emit_h):
    refs = list(refs)
    h = refs.pop(0)[0]
    if with_delta:
        d_ref, g_ref = refs.pop(0), refs.pop(0)
        h = h + g_ref[0] * d_ref[0]
    sc_ref = refs.pop(0)
    xn = h * lax.rsqrt(jnp.mean(h * h, axis=-1, keepdims=True) + RMS_EPS)
    if with_shift:
        xn = xn * (1.0 + sc_ref[0]) + refs.pop(0)[0]
    else:
        xn = xn * sc_ref[0]
    if emit_h:
        refs.pop(0)[0] = h
    refs.pop(0)[0] = xn.astype(refs[0].dtype)


def _norm_mod_call(h, scale, shift=None, delta=None, gate=None, emit_h=False, out_dtype=BF16, tl=256):
    b, l, d = h.shape
    tl = min(tl, l)
    tok = pl.BlockSpec((1, tl, d), lambda i, j: (i, j, 0))

    def vec(v):
        if v.shape[0] == 1:
            return pl.BlockSpec((1, 1, d), lambda i, j: (0, 0, 0))
        return pl.BlockSpec((1, 1, d), lambda i, j: (i, 0, 0))

    args, in_specs = [h], [tok]
    if delta is not None:
        args += [delta, gate]
        in_specs += [tok, vec(gate)]
    args.append(scale)
    in_specs.append(vec(scale))
    if shift is not None:
        args.append(shift)
        in_specs.append(vec(shift))
    out_shape, out_specs = [], []
    if emit_h:
        out_shape.append(jax.ShapeDtypeStruct((b, l, d), F32))
        out_specs.append(tok)
    out_shape.append(jax.ShapeDtypeStruct((b, l, d), out_dtype))
    out_specs.append(tok)
    res = pl.pallas_call(
        functools.partial(_norm_mod_kernel, with_delta=delta is not None, with_shift=shift is not None, emit_h=emit_h),
        grid=(b, l // tl),
        in_specs=in_specs, out_specs=tuple(out_specs), out_shape=tuple(out_shape),
        compiler_params=_cparams("parallel", "parallel"),
        name="norm_mod",
    )(*args)
    return (res[0], res[1]) if emit_h else (None, res[0])


def _mm_kernel(a_ref, w_ref, o_ref):
    o_ref[...] = jnp.dot(a_ref[...], w_ref[...], preferred_element_type=F32).astype(o_ref.dtype)


def _pick_tile(n, pref):
    t = min(pref, n)
    while n % t:
        t //= 2
    return t


def _matmul(a, w, out_dtype=F32, tm=512, tn=1024):
    m, k = a.shape
    n = w.shape[1]
    tm = _pick_tile(m, tm)
    tn = _pick_tile(n, tn)
    return pl.pallas_call(
        _mm_kernel,
        grid=(n // tn, m // tm),
        in_specs=[pl.BlockSpec((tm, k), lambda j, i: (i, 0)),
                  pl.BlockSpec((k, tn), lambda j, i: (0, j))],
        out_specs=pl.BlockSpec((tm, tn), lambda j, i: (i, j)),
        out_shape=jax.ShapeDtypeStruct((m, n), out_dtype),
        compiler_params=_cparams("parallel", "parallel"),
        name="matmul",
    )(a, w)


def _merge_kernel(oa_ref, ob_ref, oc_ref, wa_ref, wb_ref, wc_ref, ga_ref, gb_ref, gc_ref, o_ref):
    acc = jax.nn.sigmoid(ga_ref[...]) * jnp.dot(oa_ref[...], wa_ref[...], preferred_element_type=F32)
    acc += jax.nn.sigmoid(gb_ref[...]) * jnp.dot(ob_ref[...], wb_ref[...], preferred_element_type=F32)
    acc += jax.nn.sigmoid(gc_ref[...]) * jnp.dot(oc_ref[...], wc_ref[...], preferred_element_type=F32)
    o_ref[...] = acc.astype(o_ref.dtype)


def _merge_call(o_a, o_b, o_c, w_a, w_b, w_c, zg, tm=512, tn=512):
    n_tok = o_a.shape[0]
    d = w_a.shape[1]
    tm = _pick_tile(n_tok, tm)
    tn = _pick_tile(d, tn)
    nb = d // tn

    def act(o):
        return pl.BlockSpec((tm, o.shape[1]), lambda j, i: (i, 0))

    def wgt(w):
        return pl.BlockSpec((w.shape[0], tn), lambda j, i: (0, j))

    def gate(br):
        return pl.BlockSpec((tm, tn), lambda j, i: (i, j + br * nb))

    return pl.pallas_call(
        _merge_kernel,
        grid=(nb, n_tok // tm),
        in_specs=[act(o_a), act(o_b), act(o_c), wgt(w_a), wgt(w_b), wgt(w_c), gate(0), gate(1), gate(2)],
        out_specs=pl.BlockSpec((tm, tn), lambda j, i: (i, j)),
        out_shape=jax.ShapeDtypeStruct((n_tok, d), BF16),
        compiler_params=_cparams("parallel", "parallel"),
        name="branch_merge",
    )(o_a, o_b, o_c, w_a, w_b, w_c, zg, zg, zg)


def _topk_rows(s, payload, k):
    n, t = s.shape
    iota = lax.broadcasted_iota(jnp.int32, (n, t), 0).astype(F32)
    riota = lax.broadcasted_iota(jnp.int32, (k, t), 0)

    def body(r, carry):
        s, tv, tp = carry
        m = jnp.max(s, axis=0, keepdims=True)
        idx = jnp.min(jnp.where(s == m, iota, float(n)), axis=0, keepdims=True)
        sel = iota == idx
        p = idx if payload is None else jnp.max(jnp.where(sel, payload, -1.0), axis=0, keepdims=True)
        s = jnp.where(sel, -jnp.inf, s)
        tv = jnp.where(riota == r, m, tv)
        tp = jnp.where(riota == r, p, tp)
        return s, tv, tp

    _, tv, tp = lax.fori_loop(0, k, body, (s, jnp.zeros((k, t), F32), jnp.zeros((k, t), F32)))
    return tv, tp


def _peer_route_kernel(x_ref, wq_ref, keys_ref, e_ref, g_ref):
    t = x_ref.shape[0]
    q = jnp.dot(x_ref[...], wq_ref[...], preferred_element_type=F32).astype(BF16)
    half = q.shape[1] // 2
    tops = []
    for p in range(2):
        s_t = lax.dot_general(keys_ref[0, p], q[:, p * half:(p + 1) * half],
                              (((1,), (1,)), ((), ())), preferred_element_type=F32)
        tops.append(_topk_rows(s_t, None, P_TOPK))
    (s0, i0), (s1, i1) = tops
    cand, cid = [], []
    for a in range(P_TOPK):
        n_b = P_TOPK // (a + 1)
        rows = -(-n_b // V7X_SUBLANES) * V7X_SUBLANES
        keep = lax.broadcasted_iota(jnp.int32, (rows, t), 0) < n_b
        cand.append(jnp.where(keep, jnp.broadcast_to(s0[a:a + 1], (rows, t)) + s1[:rows], -jnp.inf))
        cid.append(jnp.broadcast_to(i0[a:a + 1], (rows, t)) * float(P_KEYS) + i1[:rows])
    best_s, best_e = _topk_rows(jnp.concatenate(cand, axis=0), jnp.concatenate(cid, axis=0), P_TOPK)
    ex = jnp.exp(best_s - jnp.max(best_s, axis=0, keepdims=True))
    g_ref[0] = ex / jnp.sum(ex, axis=0, keepdims=True)
    e_ref[0] = best_e.astype(jnp.int32)


def _peer_route_call(xn, wq, keys, tt=512):
    n_tok, d = xn.shape
    heads = keys.shape[0]
    qd = wq.shape[1] // heads
    tt = _pick_tile(n_tok, tt)
    return pl.pallas_call(
        _peer_route_kernel,
        grid=(n_tok // tt, heads),
        in_specs=[pl.BlockSpec((tt, d), lambda i, h: (i, 0)),
                  pl.BlockSpec((d, qd), lambda i, h: (0, h)),
                  pl.BlockSpec((1,) + keys.shape[1:], lambda i, h: (h, 0, 0, 0))],
        out_specs=(pl.BlockSpec((1, P_TOPK, tt), lambda i, h: (h, 0, i)),
                   pl.BlockSpec((1, P_TOPK, tt), lambda i, h: (h, 0, i))),
        out_shape=(jax.ShapeDtypeStruct((heads, P_TOPK, n_tok), jnp.int32),
                   jax.ShapeDtypeStruct((heads, P_TOPK, n_tok), F32)),
        compiler_params=_cparams("parallel", "parallel"),
        name="peer_route",
    )(xn, wq, keys)


_PEER_GROUP = 16
_PEER_STAGE_PITCH = P_KEYS + V7X_SUBLANES


def _peer_dense_kernel(x_ref, e_ref, g_ref, u_ref, v_ref, o_ref, gmat_ref, stage_ref, w_ref, *, eb, n_blocks):
    t = x_ref.shape[0]
    j = pl.program_id(1)
    cur = j % 2

    @pl.when(j == 0)
    def _build_gate_matrix():
        o_ref[...] = jnp.zeros_like(o_ref)
        w_ref[1] = jnp.zeros(w_ref.shape[1:], w_ref.dtype)
        sub = lax.broadcasted_iota(jnp.int32, (P_KEYS, e_ref.shape[1]), 0)

        def per_group(grp, carry):
            t0 = pl.multiple_of(grp * _PEER_GROUP, _PEER_GROUP)
            for s in range(_PEER_GROUP):
                e = e_ref[pl.ds(t0 + s, 1), :]
                g = g_ref[pl.ds(t0 + s, 1), :]
                hit_a = sub == (e >> 7)
                hit_b = sub == (e & (P_KEYS - 1))
                a_mat = jnp.where(hit_a, g, 0.0).astype(BF16)
                b_mat = jnp.where(hit_b, 1.0, 0.0).astype(BF16)
                stage_ref[s * _PEER_STAGE_PITCH:s * _PEER_STAGE_PITCH + P_KEYS, :] = _mm_nt(a_mat, b_mat)
            for i1 in range(P_KEYS):
                gmat_ref[pl.ds(t0, _PEER_GROUP), i1 * P_KEYS:(i1 + 1) * P_KEYS] = (
                    stage_ref[pl.ds(i1, _PEER_GROUP, stride=_PEER_STAGE_PITCH), :].astype(BF16))
            return carry

        lax.fori_loop(0, t // _PEER_GROUP, per_group, 0)

    o_ref[...] += jnp.dot(w_ref[1 - cur], v_ref[...], preferred_element_type=F32)
    blk = jnp.minimum(j, n_blocks - 1)
    gsel = gmat_ref[:, pl.ds(pl.multiple_of(blk * eb, eb), eb)]
    w_ref[cur] = (gsel.astype(F32) * jax.nn.gelu(_mm_nt(x_ref[...], u_ref[...]))).astype(BF16)


def _peer_dense_call(xn, e_nat, g_nat, u_tab, v_tab, tt=512, eb=256):
    n_tok, d = xn.shape
    n_exp = u_tab.shape[0]
    r = e_nat.shape[1]
    tt = _pick_tile(n_tok, tt)
    n_blocks = n_exp // eb
    return pl.pallas_call(
        functools.partial(_peer_dense_kernel, eb=eb, n_blocks=n_blocks),
        grid=(n_tok // tt, n_blocks + 1),
        in_specs=[pl.BlockSpec((tt, d), lambda i, j: (i, 0)),
                  pl.BlockSpec((tt, r), lambda i, j: (i, 0)),
                  pl.BlockSpec((tt, r), lambda i, j: (i, 0)),
                  pl.BlockSpec((eb, d), lambda i, j: (jnp.minimum(j, n_blocks - 1), 0)),
                  pl.BlockSpec((eb, d), lambda i, j: (jnp.maximum(j - 1, 0), 0))],
        out_specs=pl.BlockSpec((tt, d), lambda i, j: (i, 0)),
        out_shape=jax.ShapeDtypeStruct((n_tok, d), F32),
        scratch_shapes=[pltpu.VMEM((tt, n_exp), BF16), pltpu.VMEM((_PEER_GROUP * _PEER_STAGE_PITCH, P_KEYS), F32),
                        pltpu.VMEM((2, tt, eb), BF16)],
        compiler_params=_cparams("parallel", "arbitrary"),
        name="peer_dense",
    )(xn, e_nat, g_nat, u_tab, v_tab)


def _peer(xn, wq, keys, u_tab, v_tab):
    e_t, g_t = _peer_route_call(xn, wq, keys)
    heads = keys.shape[0]
    n_tok = xn.shape[0]
    e_nat = e_t.reshape(heads * P_TOPK, n_tok).T
    g_nat = g_t.reshape(heads * P_TOPK, n_tok).T
    return _peer_dense_call(xn, e_nat, g_nat, u_tab, v_tab)


def _split_bf16(x):
    hi = x.astype(BF16)
    lo = (x - hi.astype(F32)).astype(BF16)
    return hi, lo


def _mm(a, b):
    if a.ndim == 3:
        return lax.dot_general(a, b, (((2,), (1,)), ((0,), (0,))), preferred_element_type=F32)
    return jnp.dot(a, b, preferred_element_type=F32)


def _mm_nt(a, b):
    if a.ndim == 3:
        return lax.dot_general(a, b, (((2,), (2,)), ((0,), (0,))), preferred_element_type=F32)
    return lax.dot_general(a, b, (((1,), (1,)), ((), ())), preferred_element_type=F32)


def _dot3(a, b):
    ah, al = _split_bf16(a)
    bh, bl = _split_bf16(b)
    return _mm(ah, bh) + _mm(ah, bl) + _mm(al, bh)


_CONV_FILL_ROWS = 512


def _conv_pad_rows(cols):
    return -(-(cols + 1) // V7X_SUBLANES) * V7X_SUBLANES


def _conv_fill(x_ref, x0_s, xm_s, xq_s, *, l, cols, pad):
    tc = x0_s.shape[1]
    zeros = jnp.zeros((pad, tc), F32)
    for s in (x0_s, xm_s, xq_s):
        s[0:pad, :] = zeros
        s[pad + l:pad + l + pad, :] = zeros
    step = min(_CONV_FILL_ROWS, l)
    for s0 in range(0, l, step):
        x0_s[pad + s0:pad + s0 + step, :] = x_ref[s0:s0 + step, :]
    col = lax.broadcasted_iota(jnp.int32, (step, tc), 0) & (cols - 1)
    for s0 in range(0, l, step):
        xm_s[pad + s0:pad + s0 + step, :] = jnp.where(col >= 1, x0_s[pad + s0 - 1:pad + s0 - 1 + step, :], 0.0)
        xq_s[pad + s0:pad + s0 + step, :] = jnp.where(col <= cols - 2, x0_s[pad + s0 + 1:pad + s0 + 1 + step, :], 0.0)


def _conv_rows(x0_s, xm_s, xq_s, w_ref, t0, n, *, rows, cols, pad):
    acc = None
    for dr in ((-1, 0, 1) if rows > 1 else (0,)):
        base = pl.multiple_of(t0 + pad + dr * cols, V7X_SUBLANES)
        for dc, src in ((-1, xm_s), (0, x0_s), (1, xq_s)):
            tap = (dr + 1) * 3 + dc + 1
            term = src[pl.ds(base, n), :] * w_ref[tap:tap + 1, :]
            acc = term if acc is None else acc + term
    return acc


_CONV_CHUNK = 128


def _conv_a_kernel(z_ref, w_ref, o_ref, x0_s, xm_s, xq_s, *, l, rows, cols, pad, blocks_per_part):
    _conv_fill(z_ref.at[0], x0_s, xm_s, xq_s, l=l, cols=cols, pad=pad)
    part = pl.program_id(1) // blocks_per_part
    use_norm = part != 1
    post = jnp.where(part == 2, A_HEAD_DIM ** -0.5, 1.0)
    tc = o_ref.shape[2]

    def chunk(i, carry):
        t0 = pl.multiple_of(i * _CONV_CHUNK, _CONV_CHUNK)
        y = _conv_rows(x0_s, xm_s, xq_s, w_ref, t0, _CONV_CHUNK, rows=rows, cols=cols, pad=pad)
        y = y * jax.nn.sigmoid(y)
        outs = []
        for h0 in range(0, tc, A_HEAD_DIM):
            yh = y[:, h0:h0 + A_HEAD_DIM]
            inv = lax.rsqrt(jnp.sum(yh * yh, axis=-1, keepdims=True) + RMS_EPS) * post
            outs.append(yh * jnp.where(use_norm, inv, 1.0))
        o_ref[0, pl.ds(t0, _CONV_CHUNK), :] = jnp.concatenate(outs, axis=1) if len(outs) > 1 else outs[0]
        return carry

    lax.fori_loop(0, l // _CONV_CHUNK, chunk, 0)


def _conv_a_call(z_kvq, w9, rows, cols, tc=256):
    b, l, ch = z_kvq.shape
    pad = _conv_pad_rows(cols)
    scr = pltpu.VMEM((l + 2 * pad, tc), F32)
    return pl.pallas_call(
        functools.partial(_conv_a_kernel, l=l, rows=rows, cols=cols, pad=pad, blocks_per_part=ch // 3 // tc),
        grid=(b, ch // tc),
        in_specs=[pl.BlockSpec((1, l, tc), lambda i, j: (i, 0, j)),
                  pl.BlockSpec((9, tc), lambda i, j: (0, j))],
        out_specs=pl.BlockSpec((1, l, tc), lambda i, j: (i, 0, j)),
        out_shape=jax.ShapeDtypeStruct((b, l, ch), F32),
        scratch_shapes=[scr, scr, scr],
        compiler_params=_cparams("parallel", "parallel"),
        name="conv_a",
    )(z_kvq, w9)


def _conv_c_kernel(z0_ref, z1_ref, z2_ref, w0_ref, w1_ref, w2_ref, x0_ref, u_ref, *scr, l, rows, cols, pad):
    parts = ((z0_ref, w0_ref, scr[0:3]), (z1_ref, w1_ref, scr[3:6]), (z2_ref, w2_ref, scr[6:9]))
    for z_ref, _, s in parts:
        _conv_fill(z_ref.at[0], *s, l=l, cols=cols, pad=pad)

    def chunk(i, carry):
        t0 = pl.multiple_of(i * _CONV_CHUNK, _CONV_CHUNK)
        y = [_conv_rows(*s, w_ref, t0, _CONV_CHUNK, rows=rows, cols=cols, pad=pad) for _, w_ref, s in parts]
        x0_ref[0, pl.ds(t0, _CONV_CHUNK), :] = y[0]
        u_ref[0, pl.ds(t0, _CONV_CHUNK), :] = y[1] * y[2]
        return carry

    lax.fori_loop(0, l // _CONV_CHUNK, chunk, 0)


def _conv_c_call(z_c, w9, rows, cols, tc=128):
    b, l, ch = z_c.shape
    cw = ch // 3
    nb = cw // tc
    pad = _conv_pad_rows(cols)
    scr = pltpu.VMEM((l + 2 * pad, tc), F32)
    zspec = [pl.BlockSpec((1, l, tc), functools.partial(lambda i, j, p: (i, 0, j + p * nb), p=p)) for p in range(3)]
    wspec = [pl.BlockSpec((9, tc), functools.partial(lambda i, j, p: (0, j + p * nb), p=p)) for p in range(3)]
    ospec = pl.BlockSpec((1, l, tc), lambda i, j: (i, 0, j))
    return pl.pallas_call(
        functools.partial(_conv_c_kernel, l=l, rows=rows, cols=cols, pad=pad),
        grid=(b, nb),
        in_specs=zspec + wspec,
        out_specs=(ospec, ospec),
        out_shape=(jax.ShapeDtypeStruct((b, l, cw), F32), jax.ShapeDtypeStruct((b, l, cw), F32)),
        scratch_shapes=[scr] * 9,
        compiler_params=_cparams("parallel", "parallel"),
        name="conv_c",
    )(z_c, z_c, z_c, w9, w9, w9)


def _gate_prep_kernel(z_ref, alog_ref, dtb_ref, o_ref, *, l, heads):
    lanes = z_ref.shape[2]
    ii = lax.broadcasted_iota(jnp.int32, (A_CHUNK, A_CHUNK), 0)
    jj = lax.broadcasted_iota(jnp.int32, (A_CHUNK, A_CHUNK), 1)
    lower = jnp.where(ii >= jj, 1.0, 0.0)
    upper = jnp.where(ii <= jj, 1.0, 0.0)
    lane = lax.broadcasted_iota(jnp.int32, (A_CHUNK, lanes), 1)

    def chunk(n, carry):
        r0 = pl.multiple_of(n * A_CHUNK, A_CHUNK)
        z = z_ref[0, pl.ds(r0, A_CHUNK), :]
        beta = jax.nn.sigmoid(z)
        x = z + dtb_ref[...]
        softplus = jnp.maximum(x, 0.0) + jnp.log(1.0 + jnp.exp(-jnp.abs(x)))
        la = -jnp.exp(alog_ref[...]) * softplus
        pre = _dot3(lower, la)
        suf = _dot3(upper, la)
        o_ref[0, pl.ds(r0, A_CHUNK), :] = jnp.where(lane < 2 * heads, beta, jnp.where(lane < 3 * heads, pre, suf))
        return carry

    lax.fori_loop(0, l // A_CHUNK, chunk, 0)


def _gate_prep_call(z_ba, a_log, a_dt_bias):
    b, l, lanes = z_ba.shape
    heads = a_log.shape[1]
    pad = lambda p: jnp.pad(p.reshape(1, 2 * heads).astype(F32), ((0, 0), (2 * heads, lanes - 4 * heads)))
    return pl.pallas_call(
        functools.partial(_gate_prep_kernel, l=l, heads=heads),
        grid=(b,),
        in_specs=[pl.BlockSpec((1, l, lanes), lambda i: (i, 0, 0)),
                  pl.BlockSpec((1, lanes), lambda i: (0, 0)),
                  pl.BlockSpec((1, lanes), lambda i: (0, 0))],
        out_specs=pl.BlockSpec((1, l, lanes), lambda i: (i, 0, 0)),
        out_shape=jax.ShapeDtypeStruct((b, l, lanes), F32),
        compiler_params=_cparams("parallel"),
        name="gate_prep",
    )(z_ba, pad(a_log), pad(a_dt_bias))


_TRI_BASE_LOG2 = 3


def _unit_tri_inverse(a):
    c = a.shape[-1]
    ii = lax.broadcasted_iota(jnp.int32, a.shape, a.ndim - 2)
    jj = lax.broadcasted_iota(jnp.int32, a.shape, a.ndim - 1)
    eye = jnp.where(ii == jj, 1.0, 0.0)
    p = -jnp.where((ii >> _TRI_BASE_LOG2) == (jj >> _TRI_BASE_LOG2), a, 0.0)
    t = eye + p
    span = 2
    while span < (1 << _TRI_BASE_LOG2):
        p = _dot3(p, p)
        t = t + _dot3(t, p)
        span *= 2
    log2 = _TRI_BASE_LOG2
    while (1 << log2) < c:
        pair = jnp.where((ii >> (log2 + 1)) == (jj >> (log2 + 1)), a, 0.0)
        cross = jnp.where((ii >> log2) == (jj >> log2), 0.0, pair)
        t = t - _dot3(t, _dot3(cross, t))
        log2 += 1
    return t


def _mm_tn(a, b):
    return lax.dot_general(a, b, (((1,), (1,)), ((0,), (0,))), preferred_element_type=F32)


_DELTA_GROUP = 4


def _delta_kernel(k_ref, v_ref, q_ref, col_ref, row_ref, s0_ref, og_ref, an_ref, o_ref, sfin_ref,
                  km_s, nm_s, p_s, r_s, eg_s, sall_s, st_s, *, l):
    c = A_CHUNK
    dk = A_HEAD_DIM
    n_chunks = l // c
    g = min(_DELTA_GROUP, n_chunks)
    shape = (2 * g, c, c)
    ii = lax.broadcasted_iota(jnp.int32, shape, 1)
    jj = lax.broadcasted_iota(jnp.int32, shape, 2)
    lag = jnp.where(lax.broadcasted_iota(jnp.int32, shape, 0) >= g, jj - ii, ii - jj)
    incl = lag >= 0
    strict = lag > 0
    two = lambda x: jnp.concatenate([x, x], axis=0)

    def pass1(gi, carry):
        rows_blk = pl.ds(pl.multiple_of(gi * (g * c), g * c), g * c)
        chunks = pl.ds(gi * g, g)
        kc = k_ref[0, rows_blk, :].reshape(g, c, dk)
        vc = v_ref[0, rows_blk, :].reshape(g, c, dk)
        qc = q_ref[0, rows_blk, :].reshape(g, c, dk)
        cols = col_ref[0, 0, rows_blk, :].reshape(g, c, 8)
        rws = row_ref[0, 0, chunks]
        beta = jnp.concatenate([cols[:, :, 0:1], cols[:, :, 1:2]], axis=0)
        gcol = jnp.concatenate([cols[:, :, 2:3], cols[:, :, 3:4]], axis=0)
        grow = jnp.concatenate([rws[:, 0:1, :], rws[:, 1:2, :]], axis=0)
        dec = jnp.where(incl, jnp.exp(jnp.where(incl, gcol - grow, 0.0)), 0.0)
        kb = kc.astype(BF16)
        kk = two(_mm_nt(kb, kb))
        qk = two(_mm_nt(qc.astype(BF16), kb))
        t = _unit_tri_inverse(jnp.where(strict, kk * dec * beta, 0.0))
        eg = jnp.exp(gcol)
        k2, v2, q2 = two(kc), two(vc), two(qc)
        sol = _dot3(t, jnp.concatenate([v2 * beta, k2 * (beta * eg)], axis=2))
        ub = sol[:, :, :dk].astype(BF16)
        wb = sol[:, :, dk:].astype(BF16)
        glast = jnp.concatenate([gcol[:g, c - 1:c, :], gcol[g:, 0:1, :]], axis=0)
        ke = (k2 * jnp.exp(glast - gcol)).astype(BF16)
        qkd = (qk * dec).astype(BF16)
        km = _mm_tn(ke, wb)
        nm = _mm_tn(ke, ub)
        pm = q2 * eg - _mm(qkd, wb)
        rm = _mm(qkd, ub)
        eglast = jnp.exp(glast)
        for d in range(2):
            sl = slice(d * g, (d + 1) * g)
            km_s[d, chunks] = km[sl].astype(BF16)
            nm_s[d, chunks] = nm[sl]
            p_s[d, rows_blk, :] = pm[sl].reshape(g * c, dk).astype(BF16)
            r_s[d, rows_blk, :] = rm[sl].reshape(g * c, dk)
            eg_s[d, chunks] = jnp.broadcast_to(eglast[sl], (g, V7X_SUBLANES, dk))
        return carry

    lax.fori_loop(0, n_chunks // g, pass1, 0)

    st_s[...] = s0_ref[0, 0]

    def pass2(n, carry):
        for d, ch in ((0, n), (1, n_chunks - 1 - n)):
            s = st_s[d]
            sb = s.astype(BF16)
            sall_s[d, ch] = sb
            st_s[d] = s * eg_s[d, ch][0:1, :] - jnp.dot(km_s[d, ch], sb, preferred_element_type=F32) + nm_s[d, ch]
        return carry

    lax.fori_loop(0, n_chunks, pass2, 0)
    sfin_ref[0, 0] = st_s[...]

    def pass3(gi, carry):
        rows_blk = pl.ds(pl.multiple_of(gi * (g * c), g * c), g * c)
        chunks = pl.ds(gi * g, g)
        o = None
        for d in range(2):
            term = _mm(p_s[d, rows_blk, :].reshape(g, c, dk), sall_s[d, chunks]) + r_s[d, rows_blk, :].reshape(g, c, dk)
            o = term if o is None else o + term
        o = o.reshape(g * c, dk)
        o = o * lax.rsqrt(jnp.mean(o * o, axis=-1, keepdims=True) + RMS_EPS) * an_ref[...]
        og = og_ref[0, rows_blk, :]
        o_ref[0, rows_blk, :] = (o * (og * jax.nn.sigmoid(og))).astype(o_ref.dtype)
        return carry

    lax.fori_loop(0, n_chunks // g, pass3, 0)


def _delta_call(act, z_og, gates, s0, a_norm):
    b, l, ch = act.shape
    heads = ch // 3 // A_HEAD_DIM
    dk = A_HEAD_DIM
    c = A_CHUNK
    n_chunks = l // c
    g4 = jnp.stack([gates[..., i * heads:(i + 1) * heads] for i in range(4)], axis=-1)
    col = jnp.pad(jnp.transpose(g4, (0, 2, 1, 3)), ((0, 0), (0, 0), (0, 0), (0, 4)))
    row = jnp.transpose(g4[..., 2:4].reshape(b, n_chunks, c, heads, 2), (0, 3, 1, 4, 2))
    row = jnp.pad(row, ((0, 0), (0, 0), (0, 0), (0, 6), (0, 0)))
    tok = lambda off: pl.BlockSpec((1, l, dk), functools.partial(lambda i, h, off: (i, 0, h + off), off=off))
    return pl.pallas_call(
        functools.partial(_delta_kernel, l=l),
        grid=(b, heads),
        in_specs=[tok(0), tok(heads), tok(2 * heads),
                  pl.BlockSpec((1, 1, l, 8), lambda i, h: (i, h, 0, 0)),
                  pl.BlockSpec((1, 1, n_chunks, 8, c), lambda i, h: (i, h, 0, 0, 0)),
                  pl.BlockSpec((1, 1, 2, dk, dk), lambda i, h: (i, h, 0, 0, 0)),
                  tok(0),
                  pl.BlockSpec((1, dk), lambda i, h: (0, 0))],
        out_specs=(tok(0), pl.BlockSpec((1, 1, 2, dk, dk), lambda i, h: (i, h, 0, 0, 0))),
        out_shape=(jax.ShapeDtypeStruct((b, l, heads * dk), BF16),
                   jax.ShapeDtypeStruct((b, heads, 2, dk, dk), F32)),
        scratch_shapes=[pltpu.VMEM((2, n_chunks, dk, dk), BF16), pltpu.VMEM((2, n_chunks, dk, dk), F32),
                        pltpu.VMEM((2, l, dk), BF16), pltpu.VMEM((2, l, dk), F32),
                        pltpu.VMEM((2, n_chunks, V7X_SUBLANES, dk), F32),
                        pltpu.VMEM((2, n_chunks, dk, dk), BF16), pltpu.VMEM((2, dk, dk), F32)],
        compiler_params=_cparams("parallel", "parallel"),
        name="delta_rule",
    )(act, act, act, col, row, s0, z_og, a_norm.reshape(1, dk).astype(F32))


def _gmlp_kernel(zu_ref, zv_ref, ws_ref, bs_ref, o_ref):
    groups = ws_ref.shape[0]
    gd = zu_ref.shape[1] // groups
    u = jax.nn.gelu(zu_ref[...])
    v = jax.nn.gelu(zv_ref[...])
    mu = jnp.mean(v, axis=-1, keepdims=True)
    var = jnp.mean(jnp.square(v - mu), axis=-1, keepdims=True)
    vn = ((v - mu) * lax.rsqrt(var + RMS_EPS)).astype(BF16)
    mixed = [jnp.dot(ws_ref[g], vn[:, g * gd:(g + 1) * gd], preferred_element_type=F32) + bs_ref[:, g:g + 1]
             for g in range(groups)]
    o_ref[...] = (u * jnp.concatenate(mixed, axis=1)).astype(o_ref.dtype)


def _gmlp_call(z_b, w_s, b_s):
    n_tok, two_w = z_b.shape
    bw = two_w // 2
    groups, p, _ = w_s.shape
    return pl.pallas_call(
        _gmlp_kernel,
        grid=(n_tok // p,),
        in_specs=[pl.BlockSpec((p, bw), lambda i: (i, 0)),
                  pl.BlockSpec((p, bw), lambda i: (i, 1)),
                  pl.BlockSpec((groups, p, p), lambda i: (0, 0, 0)),
                  pl.BlockSpec((p, groups), lambda i: (0, 0))],
        out_specs=pl.BlockSpec((p, bw), lambda i: (i, 0)),
        out_shape=jax.ShapeDtypeStruct((n_tok, bw), BF16),
        compiler_params=_cparams("parallel"),
        name="gmlp",
    )(z_b, z_b, w_s, b_s.T.astype(F32))


_FEAT_PAD = 64


def _filter_feats(l):
    pos = np.arange(l, dtype=np.float32)
    t = np.linspace(0.0, 1.0, l, dtype=np.float32)[:, None]
    bands = np.linspace(1e-4, C_POS_BANDS - 1, C_POS_BANDS, dtype=np.float32)
    ang = np.float32(2.0 * math.pi / l) * pos[:, None] * bands
    feats = np.concatenate([t, np.cos(ang), -np.sin(ang)], axis=-1).astype(np.float32)
    return np.pad(feats, ((0, 0), (0, _FEAT_PAD - feats.shape[1])))


def _filter_kernel(feat_ref, w1_ref, b1_ref, f1_ref, w2_ref, b2_ref, f2_ref, w3_ref, b3_ref, rate_ref, o_ref):
    feats = feat_ref[...]
    hid = jnp.sin(f1_ref[...] * (_dot3(feats, w1_ref[...]) + b1_ref[...]))
    hid = jnp.sin(f2_ref[...] * (_dot3(hid, w2_ref[...]) + b2_ref[...]))
    filt = _dot3(hid, w3_ref[...]) + b3_ref[...]
    filt = filt * jnp.exp(-feats[:, 0:1] * rate_ref[...])
    o_ref[...] = filt * lax.rsqrt(jnp.sum(filt * filt, axis=0, keepdims=True) + RMS_EPS)


def _filter_call(l, fw1, fb1, freq1, fw2, fb2, freq2, fw3, fb3, tc=256):
    hidden = fw1.shape[1]
    two_c = fw3.shape[1]
    cw = two_c // 2
    feats = jnp.asarray(_filter_feats(l))
    w1 = jnp.pad(fw1.astype(F32), ((0, _FEAT_PAD - fw1.shape[0]), (0, 0)))
    rate = np.abs(np.linspace(C_MIN_DECAY, C_MAX_DECAY, cw, dtype=np.float32))
    rate = jnp.asarray(np.concatenate([rate, rate]).reshape(1, two_c))
    row = lambda v: v.reshape(1, -1).astype(F32)
    full = lambda shape: pl.BlockSpec(shape, lambda j: (0, 0))
    return pl.pallas_call(
        _filter_kernel,
        grid=(two_c // tc,),
        in_specs=[full((l, _FEAT_PAD)), full((_FEAT_PAD, hidden)), full((1, hidden)), full((1, hidden)),
                  full((hidden, hidden)), full((1, hidden)), full((1, hidden)),
                  pl.BlockSpec((hidden, tc), lambda j: (0, j)), pl.BlockSpec((1, tc), lambda j: (0, j)),
                  pl.BlockSpec((1, tc), lambda j: (0, j))],
        out_specs=pl.BlockSpec((l, tc), lambda j: (0, j)),
        out_shape=jax.ShapeDtypeStruct((l, two_c), F32),
        compiler_params=_cparams("parallel"),
        name="hyena_filter",
    )(feats, w1, row(fb1), row(freq1), fw2.astype(F32), row(fb2), row(freq2), fw3.astype(F32), row(fb3), rate)


def _dft_sizes(l):
    n1 = 64 if l >= 4096 else 32
    return n1, (2 * l) // n1


def _hi_lo(x):
    x = jnp.asarray(x, F32)
    hi = x.astype(BF16)
    return hi, (x - hi.astype(F32)).astype(BF16)


def _dft_consts(l):
    n1, n2 = _dft_sizes(l)
    n = n1 * n2
    half = n1 // 2
    k1 = np.arange(n1)[None, :, None]
    m1 = np.arange(half)[None, None, :]
    m2 = np.arange(n2)[:, None, None]
    f1 = np.exp(-2j * np.pi * (m1 * k1 / n1 + m2 * k1 / n))
    fwd = np.concatenate([f1.real, f1.imag], axis=1)
    fh, fl = _hi_lo(fwd)
    fwd1 = jnp.concatenate([fh, fh, fl], axis=2)
    rt = np.transpose(f1.real, (0, 2, 1)) / n
    it = np.transpose(f1.imag, (0, 2, 1)) / n
    (rh, rl), (ih, il) = _hi_lo(rt), _hi_lo(it)
    inv1 = jnp.concatenate([rh, ih, rh, ih, rl, il], axis=2)
    kk = np.arange(n2)
    f2 = np.exp(-2j * np.pi * np.outer(kk, kk) / n2)
    m = np.block([[f2.real, -f2.imag], [f2.imag, f2.real]])
    mi = np.block([[f2.real, f2.imag], [-f2.imag, f2.real]])
    (mh, ml), (mih, mil) = _hi_lo(m), _hi_lo(mi)
    return fwd1, jnp.concatenate([mh, mh, ml], axis=1), jnp.concatenate([mih, mih, mil], axis=1), inv1


_DFT_UNROLL = 4


def _stack3(x):
    hi, lo = _split_bf16(x)
    return jnp.concatenate([hi, lo, hi], axis=0)


def _dft_level1(x_ref, fwd1_ref, ar_s, ai_s, n1, n2):
    half = n1 // 2

    def body(j, carry):
        xs = x_ref[pl.ds(j, half, stride=n2), :]
        a = jnp.dot(fwd1_ref[j], _stack3(xs), preferred_element_type=F32)
        ar_s[pl.ds(j, n1, stride=n2), :] = a[:n1]
        ai_s[pl.ds(j, n1, stride=n2), :] = a[n1:]
        return carry

    lax.fori_loop(0, n2, body, 0, unroll=_DFT_UNROLL)


def _pair_rows(pair, n2):
    return pl.ds(pl.multiple_of(pair * (2 * n2), 2 * n2), 2 * n2)


def _side_by_side(x, n2):
    return jnp.concatenate([x[:n2], x[n2:]], axis=1)


def _stacked(x, tc):
    return jnp.concatenate([x[:, :tc], x[:, tc:]], axis=0)


def _dft_level2(ar_s, ai_s, m3_ref, pair, n2):
    rows = _pair_rows(pair, n2)
    blk = jnp.concatenate([_side_by_side(ar_s[rows, :], n2), _side_by_side(ai_s[rows, :], n2)], axis=0)
    x = jnp.dot(m3_ref[...], _stack3(blk), preferred_element_type=F32)
    return x[:n2], x[n2:]


def _spectrum_kernel(hf_ref, hb_ref, fwd1_ref, m3_ref, o_ref, ar_s, ai_s, *, n1, n2):
    tc = o_ref.shape[2]
    for which, h_ref in enumerate((hf_ref, hb_ref)):
        _dft_level1(h_ref, fwd1_ref, ar_s, ai_s, n1, n2)

        def body(pair, carry):
            xr, xi = _dft_level2(ar_s, ai_s, m3_ref, pair, n2)
            rows = _pair_rows(pair, n2)
            if which == 0:
                o_ref[0, rows, :] = _stacked(xr, tc)
                o_ref[1, rows, :] = _stacked(xi, tc)
            else:
                o_ref[0, rows, :] += _stacked(xr, tc)
                o_ref[1, rows, :] -= _stacked(xi, tc)
            return carry

        lax.fori_loop(0, n1 // 2, body, 0, unroll=_DFT_UNROLL // 2)


def _spectrum_call(filt, consts, tc=128):
    l, two_c = filt.shape
    cw = two_c // 2
    nb = cw // tc
    n1, n2 = _dft_sizes(l)
    fwd1, m3, _, _ = consts
    return pl.pallas_call(
        functools.partial(_spectrum_kernel, n1=n1, n2=n2),
        grid=(nb,),
        in_specs=[pl.BlockSpec((l, tc), lambda j: (0, j)),
                  pl.BlockSpec((l, tc), lambda j: (0, j + nb)),
                  pl.BlockSpec(fwd1.shape, lambda j: (0, 0, 0)),
                  pl.BlockSpec(m3.shape, lambda j: (0, 0))],
        out_specs=pl.BlockSpec((2, 2 * l, tc), lambda j: (0, 0, j)),
        out_shape=jax.ShapeDtypeStruct((2, 2 * l, cw), F32),
        scratch_shapes=[pltpu.VMEM((2 * l, tc), F32), pltpu.VMEM((2 * l, tc), F32)],
        compiler_params=_cparams("parallel"),
        name="hyena_spectrum",
    )(filt, filt, fwd1, m3)


def _longconv_kernel(u_ref, x0_ref, h_ref, skip_ref, fwd1_ref, m3_ref, mi3_ref, inv1_ref, o_ref,
                     ar_s, ai_s, y_s, *, l, n1, n2):
    half = n1 // 2
    _dft_level1(u_ref.at[0], fwd1_ref, ar_s, ai_s, n1, n2)

    tc = o_ref.shape[2]

    def per_k1_pair(pair, carry):
        xr, xi = _dft_level2(ar_s, ai_s, m3_ref, pair, n2)
        rows = _pair_rows(pair, n2)
        hr = _side_by_side(h_ref[0, rows, :], n2)
        hi = _side_by_side(h_ref[1, rows, :], n2)
        y = jnp.concatenate([xr * hr - xi * hi, xr * hi + xi * hr], axis=0)
        b = jnp.dot(mi3_ref[...], _stack3(y), preferred_element_type=F32)
        ar_s[rows, :] = _stacked(b[:n2], tc)
        ai_s[rows, :] = _stacked(b[n2:], tc)
        return carry

    lax.fori_loop(0, n1 // 2, per_k1_pair, 0, unroll=_DFT_UNROLL // 2)

    def per_n2(j, carry):
        br = ar_s[pl.ds(j, n1, stride=n2), :]
        bi = ai_s[pl.ds(j, n1, stride=n2), :]
        (brh, brl), (bih, bil) = _split_bf16(br), _split_bf16(bi)
        rhs = jnp.concatenate([brh, bih, brl, bil, brh, bih], axis=0)
        y_s[pl.ds(j, half, stride=n2), :] = jnp.dot(inv1_ref[j], rhs, preferred_element_type=F32)
        return carry

    lax.fori_loop(0, n2, per_n2, 0, unroll=_DFT_UNROLL)

    rows_out = min(512, l)

    def finish(i, carry):
        rows = pl.ds(pl.multiple_of(i * rows_out, rows_out), rows_out)
        u = u_ref[0, rows, :]
        o_ref[0, rows, :] = (x0_ref[0, rows, :] * (y_s[rows, :] + skip_ref[...] * u)).astype(o_ref.dtype)
        return carry

    lax.fori_loop(0, l // rows_out, finish, 0)


def _longconv_call(u, x0, spec, skip, consts, tc=128):
    b, l, cw = u.shape
    n1, n2 = _dft_sizes(l)
    fwd1, m3, mi3, inv1 = consts
    tok = pl.BlockSpec((1, l, tc), lambda j, i: (i, 0, j))
    const = lambda a: pl.BlockSpec(a.shape, lambda j, i: (0,) * a.ndim)
    return pl.pallas_call(
        functools.partial(_longconv_kernel, l=l, n1=n1, n2=n2),
        grid=(cw // tc, b),
        in_specs=[tok, tok,
                  pl.BlockSpec((2, 2 * l, tc), lambda j, i: (0, 0, j)),
                  pl.BlockSpec((1, tc), lambda j, i: (0, j)),
                  const(fwd1), const(m3), const(mi3), const(inv1)],
        out_specs=tok,
        out_shape=jax.ShapeDtypeStruct((b, l, cw), BF16),
        scratch_shapes=[pltpu.VMEM((2 * l, tc), F32), pltpu.VMEM((2 * l, tc), F32), pltpu.VMEM((l, tc), F32)],
        compiler_params=_cparams("parallel", "parallel"),
        name="hyena_longconv",
    )(u, x0, spec, skip.reshape(1, cw).astype(F32), fwd1, m3, mi3, inv1)


def _layer_weights(i, p):
    w_in = p["w_in"][i]
    aw, bw, cw = p["w_br_a"].shape[1], p["w_br_b"].shape[1], p["w_br_c"].shape[1]
    heads = p["a_log"].shape[2]
    off_ba = 2 * aw
    off_q = off_ba + 4 * heads
    off_og = off_q + aw
    off_bu = off_og + aw
    off_cx = off_bu + 2 * bw
    off_gate = off_cx + 3 * cw
    cast = lambda w: w.astype(BF16)
    return {
        "w_kvq": cast(jnp.concatenate([w_in[:, :off_ba], w_in[:, off_q:off_og]], axis=1)),
        "w_ba": cast(jnp.pad(w_in[:, off_ba:off_q], ((0, 0), (0, V7X_LANES - 4 * heads)))),
        "w_og": cast(w_in[:, off_og:off_bu]),
        "w_b": cast(w_in[:, off_bu:off_cx]),
        "w_c": cast(w_in[:, off_cx:off_gate]),
        "w_gate": cast(w_in[:, off_gate:]),
        "w_br_a": cast(p["w_br_a"][i]), "w_br_b": cast(p["w_br_b"][i]), "w_br_c": cast(p["w_br_c"][i]),
        "w_out": cast(p["w_out"][i]),
        "a_conv": p["a_conv"][i].reshape(9, -1), "c_conv": p["c_conv"][i].reshape(9, -1),
        "a_log": p["a_log"][i], "a_dt_bias": p["a_dt_bias"][i], "a_norm": p["a_norm"][i],
        "b_ws": cast(p["b_ws"][i]), "b_bs": p["b_bs"][i], "c_skip": p["c_skip"][i],
        "filter": tuple(p[k][i] for k in ("c_fw1", "c_fb1", "c_freq1", "c_fw2", "c_fb2", "c_freq2", "c_fw3", "c_fb3")),
        "p_wq": cast(p["p_wq"][i]), "p_keys": cast(p["p_keys"][i]), "p_u": cast(p["p_u"][i]), "p_v": cast(p["p_v"][i]),
    }


def _delta_branch(xf, b, l, rows, cols, s0, lw):
    z_kvq = _matmul(xf, lw["w_kvq"]).reshape(b, l, -1)
    z_og = _matmul(xf, lw["w_og"]).reshape(b, l, -1)
    z_ba = _matmul(xf, lw["w_ba"]).reshape(b, l, -1)
    act = _conv_a_call(z_kvq, lw["a_conv"], rows, cols)
    gates = _gate_prep_call(z_ba, lw["a_log"], lw["a_dt_bias"])
    o_a, s_fin = _delta_call(act, z_og, gates, s0, lw["a_norm"])
    return o_a.reshape(b * l, -1), s_fin


def _token_mixer(xn, rows, cols, s0, lw):
    b, l, d = xn.shape
    xf = xn.reshape(b * l, d)
    o_a, s_fin = _delta_branch(xf, b, l, rows, cols, s0, lw)
    o_b = _gmlp_call(_matmul(xf, lw["w_b"]), lw["b_ws"], lw["b_bs"])
    x0, u = _conv_c_call(_matmul(xf, lw["w_c"]).reshape(b, l, -1), lw["c_conv"], rows, cols)
    consts = _dft_consts(l)
    spec = _spectrum_call(_filter_call(l, *lw["filter"]), consts)
    o_c = _longconv_call(u, x0, spec, lw["c_skip"], consts).reshape(b * l, -1)
    merged = _merge_call(o_a, o_b, o_c, lw["w_br_a"], lw["w_br_b"], lw["w_br_c"], _matmul(xf, lw["w_gate"]))
    return _matmul(merged, lw["w_out"]).reshape(b, l, d), s_fin


def _peer_layer(xn, lw):
    b, l, d = xn.shape
    return _peer(xn.reshape(b * l, d), lw["p_wq"], lw["p_keys"], lw["p_u"], lw["p_v"]).reshape(b, l, d)


def kernel(x, c, ctx, c_ctx, ada_w, ada_b, w_in, a_conv, a_log, a_dt_bias, a_norm, b_ws, b_bs, c_conv, c_fw1, c_fb1,
           c_freq1, c_fw2, c_fb2, c_freq2, c_fw3, c_fb3, c_skip, w_br_a, w_br_b, w_br_c, w_out, p_wq, p_keys, p_u,
           p_v, final_norm):
    params = dict(w_in=w_in, a_conv=a_conv, a_log=a_log, a_dt_bias=a_dt_bias, a_norm=a_norm, b_ws=b_ws, b_bs=b_bs,
                  c_conv=c_conv, c_fw1=c_fw1, c_fb1=c_fb1, c_freq1=c_freq1, c_fw2=c_fw2, c_fb2=c_fb2, c_freq2=c_freq2,
                  c_fw3=c_fw3, c_fb3=c_fb3, c_skip=c_skip, w_br_a=w_br_a, w_br_b=w_br_b, w_br_c=w_br_c, w_out=w_out,
                  p_wq=p_wq, p_keys=p_keys, p_u=p_u, p_v=p_v)
    b, l, d = x.shape
    depth = w_in.shape[0]
    rows = l // GRID_W
    l_ctx = ctx.shape[1]
    heads = a_log.shape[2]
    s_zero = jnp.zeros((b, heads, 2, A_HEAD_DIM, A_HEAD_DIM), F32)
    cvec = jnp.concatenate([c, c_ctx[None, :], jnp.zeros((V7X_SUBLANES - b - 1, d), F32)], axis=0)

    h, h_pending = x, None
    hc, hc_pending = ctx, None
    for i in range(depth):
        lw = _layer_weights(i, params)
        mod_all = _mod_call(cvec, ada_w[i], ada_b[i])
        mod = mod_all[:b].reshape(b, N_MOD, 1, d)
        mod_c = mod_all[b].reshape(N_MOD, 1, 1, d)

        def norm(stream, pending, scale, shift):
            if pending is None:
                return stream, _norm_mod_call(stream, scale, shift)[1]
            return _norm_mod_call(stream, scale, shift, delta=pending[0], gate=pending[1], emit_h=True)

        hc, xnc = norm(hc, hc_pending, mod_c[1], mod_c[0])
        if i == depth - 1:
            _, s_ctx = _delta_branch(xnc.reshape(b * l_ctx, d), b, l_ctx, 1, l_ctx, s_zero, lw)
        else:
            out_c, s_ctx = _token_mixer(xnc, 1, l_ctx, s_zero, lw)
            hc, xnc2 = norm(hc, (out_c, mod_c[2]), mod_c[4], mod_c[3])
            hc_pending = (_peer_layer(xnc2, lw), mod_c[5])
        h, xn = norm(h, h_pending, mod[:, 1], mod[:, 0])
        out, _ = _token_mixer(xn, rows, GRID_W, s_ctx, lw)
        h, xn2 = norm(h, (out, mod[:, 2]), mod[:, 4], mod[:, 3])
        h_pending = (_peer_layer(xn2, lw), mod[:, 5])
    return _norm_mod_call(h, final_norm.reshape(1, 1, d), delta=h_pending[0], gate=h_pending[1], out_dtype=F32)[1]
```

```python
import functools
import math

import numpy as np
import jax
import jax.numpy as jnp
from jax import lax
from jax.experimental import pallas as pl
from jax.experimental.pallas import tpu as pltpu

F32 = jnp.float32
BF16 = jnp.bfloat16

V7X_LANES = 128
V7X_SUBLANES = 8
V7X_VMEM_LIMIT_BYTES = 56 * 1024 * 1024

GRID_W = 64
RMS_EPS = 1e-6
N_MOD = 6
A_HEAD_DIM = 128
A_CHUNK = 64
B_CHUNK = 128
C_POS_BANDS = 16
C_MIN_DECAY = math.log(1e-2) / 1.5
C_MAX_DECAY = math.log(1e-2) / 0.3
N_BRANCH = 3
P_KEYS = 128
P_TOPK = 16


def _cparams(*sem):
    return pltpu.CompilerParams(dimension_semantics=sem, vmem_limit_bytes=V7X_VMEM_LIMIT_BYTES)


def _mod_kernel(c_ref, w_ref, b_ref, o_ref):
    c = c_ref[...]
    a = (c * jax.nn.sigmoid(c)).astype(BF16)
    o_ref[...] = jnp.dot(a, w_ref[...].astype(BF16), preferred_element_type=F32) + b_ref[...]


def _mod_call(cvec, ada_w, ada_b):
    rows, d = cvec.shape
    n = ada_w.shape[1]
    tn = 1024
    return pl.pallas_call(
        _mod_kernel,
        grid=(n // tn,),
        in_specs=[pl.BlockSpec((rows, d), lambda j: (0, 0)),
                  pl.BlockSpec((d, tn), lambda j: (0, j)),
                  pl.BlockSpec((1, tn), lambda j: (0, j))],
        out_specs=pl.BlockSpec((rows, tn), lambda j: (0, j)),
        out_shape=jax.ShapeDtypeStruct((rows, n), F32),
        compiler_params=_cparams("parallel"),
        name="ada_mod",
    )(cvec, ada_w, ada_b.reshape(1, n))


def _norm_mod_kernel(*refs, with_delta, with_shift, emit_h):
    refs = list(refs)
    h = refs.pop(0)[0]
    if with_delta:
        d_ref, g_ref = refs.pop(0), refs.pop(0)
        h = h + g_ref[0] * d_ref[0]
    sc_ref = refs.pop(0)
    xn = h * lax.rsqrt(jnp.mean(h * h, axis=-1, keepdims=True) + RMS_EPS)
    if with_shift:
        xn = xn * (1.0 + sc_ref[0]) + refs.pop(0)[0]
    else:
        xn = xn * sc_ref[0]
    if emit_h:
        refs.pop(0)[0] = h
    refs.pop(0)[0] = xn.astype(refs[0].dtype)


def _norm_mod_call(h, scale, shift=None, delta=None, gate=None, emit_h=False, out_dtype=BF16, tl=256):
    b, l, d = h.shape
    tl = min(tl, l)
    tok = pl.BlockSpec((1, tl, d), lambda i, j: (i, j, 0))

    def vec(v):
        if v.shape[0] == 1:
            return pl.BlockSpec((1, 1, d), lambda i, j: (0, 0, 0))
        return pl.BlockSpec((1, 1, d), lambda i, j: (i, 0, 0))

    args, in_specs = [h], [tok]
    if delta is not None:
        args += [delta, gate]
        in_specs += [tok, vec(gate)]
    args.append(scale)
    in_specs.append(vec(scale))
    if shift is not None:
        args.append(shift)
        in_specs.append(vec(shift))
    out_shape, out_specs = [], []
    if emit_h:
        out_shape.append(jax.ShapeDtypeStruct((b, l, d), F32))
        out_specs.append(tok)
    out_shape.append(jax.ShapeDtypeStruct((b, l, d), out_dtype))
    out_specs.append(tok)
    res = pl.pallas_call(
        functools.partial(_norm_mod_kernel, with_delta=delta is not None, with_shift=shift is not None, emit_h=emit_h),
        grid=(b, l // tl),
        in_specs=in_specs, out_specs=tuple(out_specs), out_shape=tuple(out_shape),
        compiler_params=_cparams("parallel", "parallel"),
        name="norm_mod",
    )(*args)
    return (res[0], res[1]) if emit_h else (None, res[0])


def _mm_kernel(a_ref, w_ref, o_ref):
    o_ref[...] = jnp.dot(a_ref[...], w_ref[...], preferred_element_type=F32).astype(o_ref.dtype)


def _pick_tile(n, pref):
    t = min(pref, n)
    while n % t:
        t //= 2
    return t


def _matmul(a, w, out_dtype=F32, tm=512, tn=1024):
    m, k = a.shape
    n = w.shape[1]
    tm = _pick_tile(m, tm)
    tn = _pick_tile(n, tn)
    return pl.pallas_call(
        _mm_kernel,
        grid=(n // tn, m // tm),
        in_specs=[pl.BlockSpec((tm, k), lambda j, i: (i, 0)),
                  pl.BlockSpec((k, tn), lambda j, i: (0, j))],
        out_specs=pl.BlockSpec((tm, tn), lambda j, i: (i, j)),
        out_shape=jax.ShapeDtypeStruct((m, n), out_dtype),
        compiler_params=_cparams("parallel", "parallel"),
        name="matmul",
    )(a, w)


def _merge_kernel(oa_ref, ob_ref, oc_ref, wa_ref, wb_ref, wc_ref, ga_ref, gb_ref, gc_ref, o_ref):
    acc = jax.nn.sigmoid(ga_ref[...]) * jnp.dot(oa_ref[...], wa_ref[...], preferred_element_type=F32)
    acc += jax.nn.sigmoid(gb_ref[...]) * jnp.dot(ob_ref[...], wb_ref[...], preferred_element_type=F32)
    acc += jax.nn.sigmoid(gc_ref[...]) * jnp.dot(oc_ref[...], wc_ref[...], preferred_element_type=F32)
    o_ref[...] = acc.astype(o_ref.dtype)


def _merge_call(o_a, o_b, o_c, w_a, w_b, w_c, zg, tm=512, tn=512):
    n_tok = o_a.shape[0]
    d = w_a.shape[1]
    tm = _pick_tile(n_tok, tm)
    tn = _pick_tile(d, tn)
    nb = d // tn

    def act(o):
        return pl.BlockSpec((tm, o.shape[1]), lambda j, i: (i, 0))

    def wgt(w):
        return pl.BlockSpec((w.shape[0], tn), lambda j, i: (0, j))

    def gate(br):
        return pl.BlockSpec((tm, tn), lambda j, i: (i, j + br * nb))

    return pl.pallas_call(
        _merge_kernel,
        grid=(nb, n_tok // tm),
        in_specs=[act(o_a), act(o_b), act(o_c), wgt(w_a), wgt(w_b), wgt(w_c), gate(0), gate(1), gate(2)],
        out_specs=pl.BlockSpec((tm, tn), lambda j, i: (i, j)),
        out_shape=jax.ShapeDtypeStruct((n_tok, d), BF16),
        compiler_params=_cparams("parallel", "parallel"),
        name="branch_merge",
    )(o_a, o_b, o_c, w_a, w_b, w_c, zg, zg, zg)


def _topk_rows(s, payload, k):
    n, t = s.shape
    iota = lax.broadcasted_iota(jnp.int32, (n, t), 0).astype(F32)
    riota = lax.broadcasted_iota(jnp.int32, (k, t), 0)

    def body(r, carry):
        s, tv, tp = carry
        m = jnp.max(s, axis=0, keepdims=True)
        idx = jnp.min(jnp.where(s == m, iota, float(n)), axis=0, keepdims=True)
        sel = iota == idx
        p = idx if payload is None else jnp.max(jnp.where(sel, payload, -1.0), axis=0, keepdims=True)
        s = jnp.where(sel, -jnp.inf, s)
        tv = jnp.where(riota == r, m, tv)
        tp = jnp.where(riota == r, p, tp)
        return s, tv, tp

    _, tv, tp = lax.fori_loop(0, k, body, (s, jnp.zeros((k, t), F32), jnp.zeros((k, t), F32)))
    return tv, tp


def _peer_route_kernel(x_ref, wq_ref, keys_ref, e_ref, g_ref):
    t = x_ref.shape[0]
    q = jnp.dot(x_ref[...], wq_ref[...], preferred_element_type=F32).astype(BF16)
    half = q.shape[1] // 2
    tops = []
    for p in range(2):
        s_t = lax.dot_general(keys_ref[0, p], q[:, p * half:(p + 1) * half],
                              (((1,), (1,)), ((), ())), preferred_element_type=F32)
        tops.append(_topk_rows(s_t, None, P_TOPK))
    (s0, i0), (s1, i1) = tops
    cand, cid = [], []
    for a in range(P_TOPK):
        n_b = P_TOPK // (a + 1)
        rows = -(-n_b // V7X_SUBLANES) * V7X_SUBLANES
        keep = lax.broadcasted_iota(jnp.int32, (rows, t), 0) < n_b
        cand.append(jnp.where(keep, jnp.broadcast_to(s0[a:a + 1], (rows, t)) + s1[:rows], -jnp.inf))
        cid.append(jnp.broadcast_to(i0[a:a + 1], (rows, t)) * float(P_KEYS) + i1[:rows])
    best_s, best_e = _topk_rows(jnp.concatenate(cand, axis=0), jnp.concatenate(cid, axis=0), P_TOPK)
    ex = jnp.exp(best_s - jnp.max(best_s, axis=0, keepdims=True))
    g_ref[0] = ex / jnp.sum(ex, axis=0, keepdims=True)
    e_ref[0] = best_e.astype(jnp.int32)


def _peer_route_call(xn, wq, keys, tt=512):
    n_tok, d = xn.shape
    heads = keys.shape[0]
    qd = wq.shape[1] // heads
    tt = _pick_tile(n_tok, tt)
    return pl.pallas_call(
        _peer_route_kernel,
        grid=(n_tok // tt, heads),
        in_specs=[pl.BlockSpec((tt, d), lambda i, h: (i, 0)),
                  pl.BlockSpec((d, qd), lambda i, h: (0, h)),
                  pl.BlockSpec((1,) + keys.shape[1:], lambda i, h: (h, 0, 0, 0))],
        out_specs=(pl.BlockSpec((1, P_TOPK, tt), lambda i, h: (h, 0, i)),
                   pl.BlockSpec((1, P_TOPK, tt), lambda i, h: (h, 0, i))),
        out_shape=(jax.ShapeDtypeStruct((heads, P_TOPK, n_tok), jnp.int32),
                   jax.ShapeDtypeStruct((heads, P_TOPK, n_tok), F32)),
        compiler_params=_cparams("parallel", "parallel"),
        name="peer_route",
    )(xn, wq, keys)


_PEER_GROUP = 16
_PEER_STAGE_PITCH = P_KEYS + V7X_SUBLANES


def _peer_dense_kernel(x_ref, e_ref, g_ref, u_ref, v_ref, o_ref, gmat_ref, stage_ref, w_ref, *, eb, n_blocks):
    t = x_ref.shape[0]
    j = pl.program_id(1)
    cur = j % 2

    @pl.when(j == 0)
    def _build_gate_matrix():
        o_ref[...] = jnp.zeros_like(o_ref)
        w_ref[1] = jnp.zeros(w_ref.shape[1:], w_ref.dtype)
        sub = lax.broadcasted_iota(jnp.int32, (P_KEYS, e_ref.shape[1]), 0)

        def per_group(grp, carry):
            t0 = pl.multiple_of(grp * _PEER_GROUP, _PEER_GROUP)
            for s in range(_PEER_GROUP):
                e = e_ref[pl.ds(t0 + s, 1), :]
                g = g_ref[pl.ds(t0 + s, 1), :]
                hit_a = sub == (e >> 7)
                hit_b = sub == (e & (P_KEYS - 1))
                a_mat = jnp.where(hit_a, g, 0.0).astype(BF16)
                b_mat = jnp.where(hit_b, 1.0, 0.0).astype(BF16)
                stage_ref[s * _PEER_STAGE_PITCH:s * _PEER_STAGE_PITCH + P_KEYS, :] = _mm_nt(a_mat, b_mat)
            for i1 in range(P_KEYS):
                gmat_ref[pl.ds(t0, _PEER_GROUP), i1 * P_KEYS:(i1 + 1) * P_KEYS] = (
                    stage_ref[pl.ds(i1, _PEER_GROUP, stride=_PEER_STAGE_PITCH), :].astype(BF16))
            return carry

        lax.fori_loop(0, t // _PEER_GROUP, per_group, 0)

    o_ref[...] += jnp.dot(w_ref[1 - cur], v_ref[...], preferred_element_type=F32)
    blk = jnp.minimum(j, n_blocks - 1)
    gsel = gmat_ref[:, pl.ds(pl.multiple_of(blk * eb, eb), eb)]
    w_ref[cur] = (gsel.astype(F32) * jax.nn.gelu(_mm_nt(x_ref[...], u_ref[...]))).astype(BF16)


def _peer_dense_call(xn, e_nat, g_nat, u_tab, v_tab, tt=512, eb=256):
    n_tok, d = xn.shape
    n_exp = u_tab.shape[0]
    r = e_nat.shape[1]
    tt = _pick_tile(n_tok, tt)
    n_blocks = n_exp // eb
    return pl.pallas_call(
        functools.partial(_peer_dense_kernel, eb=eb, n_blocks=n_blocks),
        grid=(n_tok // tt, n_blocks + 1),
        in_specs=[pl.BlockSpec((tt, d), lambda i, j: (i, 0)),
                  pl.BlockSpec((tt, r), lambda i, j: (i, 0)),
                  pl.BlockSpec((tt, r), lambda i, j: (i, 0)),
                  pl.BlockSpec((eb, d), lambda i, j: (jnp.minimum(j, n_blocks - 1), 0)),
                  pl.BlockSpec((eb, d), lambda i, j: (jnp.maximum(j - 1, 0), 0))],
        out_specs=pl.BlockSpec((tt, d), lambda i, j: (i, 0)),
        out_shape=jax.ShapeDtypeStruct((n_tok, d), F32),
        scratch_shapes=[pltpu.VMEM((tt, n_exp), BF16), pltpu.VMEM((_PEER_GROUP * _PEER_STAGE_PITCH, P_KEYS), F32),
                        pltpu.VMEM((2, tt, eb), BF16)],
        compiler_params=_cparams("parallel", "arbitrary"),
        name="peer_dense",
    )(xn, e_nat, g_nat, u_tab, v_tab)


def _peer(xn, wq, keys, u_tab, v_tab):
    e_t, g_t = _peer_route_call(xn, wq, keys)
    heads = keys.shape[0]
    n_tok = xn.shape[0]
    e_nat = e_t.reshape(heads * P_TOPK, n_tok).T
    g_nat = g_t.reshape(heads * P_TOPK, n_tok).T
    return _peer_dense_call(xn, e_nat, g_nat, u_tab, v_tab)


def _split_bf16(x):
    hi = x.astype(BF16)
    lo = (x - hi.astype(F32)).astype(BF16)
    return hi, lo


def _mm(a, b):
    if a.ndim == 3:
        return lax.dot_general(a, b, (((2,), (1,)), ((0,), (0,))), preferred_element_type=F32)
    return jnp.dot(a, b, preferred_element_type=F32)


def _mm_nt(a, b):
    if a.ndim == 3:
        return lax.dot_general(a, b, (((2,), (2,)), ((0,), (0,))), preferred_element_type=F32)
    return lax.dot_general(a, b, (((1,), (1,)), ((), ())), preferred_element_type=F32)


def _dot3(a, b):
    ah, al = _split_bf16(a)
    bh, bl = _split_bf16(b)
    return _mm(ah, bh) + _mm(ah, bl) + _mm(al, bh)


_CONV_FILL_ROWS = 512


def _conv_pad_rows(cols):
    return -(-(cols + 1) // V7X_SUBLANES) * V7X_SUBLANES


def _conv_fill(x_ref, x0_s, xm_s, xq_s, *, l, cols, pad):
    tc = x0_s.shape[1]
    zeros = jnp.zeros((pad, tc), F32)
    for s in (x0_s, xm_s, xq_s):
        s[0:pad, :] = zeros
        s[pad + l:pad + l + pad, :] = zeros
    step = min(_CONV_FILL_ROWS, l)
    for s0 in range(0, l, step):
        x0_s[pad + s0:pad + s0 + step, :] = x_ref[s0:s0 + step, :]
    col = lax.broadcasted_iota(jnp.int32, (step, tc), 0) & (cols - 1)
    for s0 in range(0, l, step):
        xm_s[pad + s0:pad + s0 + step, :] = jnp.where(col >= 1, x0_s[pad + s0 - 1:pad + s0 - 1 + step, :], 0.0)
        xq_s[pad + s0:pad + s0 + step, :] = jnp.where(col <= cols - 2, x0_s[pad + s0 + 1:pad + s0 + 1 + step, :], 0.0)


def _conv_rows(x0_s, xm_s, xq_s, w_ref, t0, n, *, rows, cols, pad):
    acc = None
    for dr in ((-1, 0, 1) if rows > 1 else (0,)):
        base = pl.multiple_of(t0 + pad + dr * cols, V7X_SUBLANES)
        for dc, src in ((-1, xm_s), (0, x0_s), (1, xq_s)):
            tap = (dr + 1) * 3 + dc + 1
            term = src[pl.ds(base, n), :] * w_ref[tap:tap + 1, :]
            acc = term if acc is None else acc + term
    return acc


_CONV_CHUNK = 128


def _conv_a_kernel(z_ref, w_ref, o_ref, x0_s, xm_s, xq_s, *, l, rows, cols, pad, blocks_per_part):
    _conv_fill(z_ref.at[0], x0_s, xm_s, xq_s, l=l, cols=cols, pad=pad)
    part = pl.program_id(1) // blocks_per_part
    use_norm = part != 1
    post = jnp.where(part == 2, A_HEAD_DIM ** -0.5, 1.0)
    tc = o_ref.shape[2]

    def chunk(i, carry):
        t0 = pl.multiple_of(i * _CONV_CHUNK, _CONV_CHUNK)
        y = _conv_rows(x0_s, xm_s, xq_s, w_ref, t0, _CONV_CHUNK, rows=rows, cols=cols, pad=pad)
        y = y * jax.nn.sigmoid(y)
        outs = []
        for h0 in range(0, tc, A_HEAD_DIM):
            yh = y[:, h0:h0 + A_HEAD_DIM]
            inv = lax.rsqrt(jnp.sum(yh * yh, axis=-1, keepdims=True) + RMS_EPS) * post
            outs.append(yh * jnp.where(use_norm, inv, 1.0))
        o_ref[0, pl.ds(t0, _CONV_CHUNK), :] = jnp.concatenate(outs, axis=1) if len(outs) > 1 else outs[0]
        return carry

    lax.fori_loop(0, l // _CONV_CHUNK, chunk, 0)


def _conv_a_call(z_kvq, w9, rows, cols, tc=256):
    b, l, ch = z_kvq.shape
    pad = _conv_pad_rows(cols)
    scr = pltpu.VMEM((l + 2 * pad, tc), F32)
    return pl.pallas_call(
        functools.partial(_conv_a_kernel, l=l, rows=rows, cols=cols, pad=pad, blocks_per_part=ch // 3 // tc),
        grid=(b, ch // tc),
        in_specs=[pl.BlockSpec((1, l, tc), lambda i, j: (i, 0, j)),
                  pl.BlockSpec((9, tc), lambda i, j: (0, j))],
        out_specs=pl.BlockSpec((1, l, tc), lambda i, j: (i, 0, j)),
        out_shape=jax.ShapeDtypeStruct((b, l, ch), F32),
        scratch_shapes=[scr, scr, scr],
        compiler_params=_cparams("parallel", "parallel"),
        name="conv_a",
    )(z_kvq, w9)


def _conv_c_kernel(z0_ref, z1_ref, z2_ref, w0_ref, w1_ref, w2_ref, x0_ref, u_ref, *scr, l, rows, cols, pad):
    parts = ((z0_ref, w0_ref, scr[0:3]), (z1_ref, w1_ref, scr[3:6]), (z2_ref, w2_ref, scr[6:9]))
    for z_ref, _, s in parts:
        _conv_fill(z_ref.at[0], *s, l=l, cols=cols, pad=pad)

    def chunk(i, carry):
        t0 = pl.multiple_of(i * _CONV_CHUNK, _CONV_CHUNK)
        y = [_conv_rows(*s, w_ref, t0, _CONV_CHUNK, rows=rows, cols=cols, pad=pad) for _, w_ref, s in parts]
        x0_ref[0, pl.ds(t0, _CONV_CHUNK), :] = y[0]
        u_ref[0, pl.ds(t0, _CONV_CHUNK), :] = y[1] * y[2]
        return carry

    lax.fori_loop(0, l // _CONV_CHUNK, chunk, 0)


def _conv_c_call(z_c, w9, rows, cols, tc=128):
    b, l, ch = z_c.shape
    cw = ch // 3
    nb = cw // tc
    pad = _conv_pad_rows(cols)
    scr = pltpu.VMEM((l + 2 * pad, tc), F32)
    zspec = [pl.BlockSpec((1, l, tc), functools.partial(lambda i, j, p: (i, 0, j + p * nb), p=p)) for p in range(3)]
    wspec = [pl.BlockSpec((9, tc), functools.partial(lambda i, j, p: (0, j + p * nb), p=p)) for p in range(3)]
    ospec = pl.BlockSpec((1, l, tc), lambda i, j: (i, 0, j))
    return pl.pallas_call(
        functools.partial(_conv_c_kernel, l=l, rows=rows, cols=cols, pad=pad),
        grid=(b, nb),
        in_specs=zspec + wspec,
        out_specs=(ospec, ospec),
        out_shape=(jax.ShapeDtypeStruct((b, l, cw), F32), jax.ShapeDtypeStruct((b, l, cw), F32)),
        scratch_shapes=[scr] * 9,
        compiler_params=_cparams("parallel", "parallel"),
        name="conv_c",
    )(z_c, z_c, z_c, w9, w9, w9)


def _gate_prep_kernel(z_ref, alog_ref, dtb_ref, o_ref, *, l, heads):
    lanes = z_ref.shape[2]
    ii = lax.broadcasted_iota(jnp.int32, (A_CHUNK, A_CHUNK), 0)
    jj = lax.broadcasted_iota(jnp.int32, (A_CHUNK, A_CHUNK), 1)
    lower = jnp.where(ii >= jj, 1.0, 0.0)
    upper = jnp.where(ii <= jj, 1.0, 0.0)
    lane = lax.broadcasted_iota(jnp.int32, (A_CHUNK, lanes), 1)

    def chunk(n, carry):
        r0 = pl.multiple_of(n * A_CHUNK, A_CHUNK)
        z = z_ref[0, pl.ds(r0, A_CHUNK), :]
        beta = jax.nn.sigmoid(z)
        x = z + dtb_ref[...]
        softplus = jnp.maximum(x, 0.0) + jnp.log(1.0 + jnp.exp(-jnp.abs(x)))
        la = -jnp.exp(alog_ref[...]) * softplus
        pre = _dot3(lower, la)
        suf = _dot3(upper, la)
        o_ref[0, pl.ds(r0, A_CHUNK), :] = jnp.where(lane < 2 * heads, beta, jnp.where(lane < 3 * heads, pre, suf))
        return carry

    lax.fori_loop(0, l // A_CHUNK, chunk, 0)


def _gate_prep_call(z_ba, a_log, a_dt_bias):
    b, l, lanes = z_ba.shape
    heads = a_log.shape[1]
    pad = lambda p: jnp.pad(p.reshape(1, 2 * heads).astype(F32), ((0, 0), (2 * heads, lanes - 4 * heads)))
    return pl.pallas_call(
        functools.partial(_gate_prep_kernel, l=l, heads=heads),
        grid=(b,),
        in_specs=[pl.BlockSpec((1, l, lanes), lambda i: (i, 0, 0)),
                  pl.BlockSpec((1, lanes), lambda i: (0, 0)),
                  pl.BlockSpec((1, lanes), lambda i: (0, 0))],
        out_specs=pl.BlockSpec((1, l, lanes), lambda i: (i, 0, 0)),
        out_shape=jax.ShapeDtypeStruct((b, l, lanes), F32),
        compiler_params=_cparams("parallel"),
        name="gate_prep",
    )(z_ba, pad(a_log), pad(a_dt_bias))


_TRI_BASE_LOG2 = 3


def _dot1(a, b):
    return _mm(a.astype(BF16), b.astype(BF16))


def _unit_tri_inverse(a, dot):
    c = a.shape[-1]
    ii = lax.broadcasted_iota(jnp.int32, a.shape, a.ndim - 2)
    jj = lax.broadcasted_iota(jnp.int32, a.shape, a.ndim - 1)
    eye = jnp.where(ii == jj, 1.0, 0.0)
    p = -jnp.where((ii >> _TRI_BASE_LOG2) == (jj >> _TRI_BASE_LOG2), a, 0.0)
    t = eye + p
    span = 2
    while span < (1 << _TRI_BASE_LOG2):
        p = dot(p, p)
        t = t + dot(t, p)
        span *= 2
    log2 = _TRI_BASE_LOG2
    while (1 << log2) < c:
        pair = jnp.where((ii >> (log2 + 1)) == (jj >> (log2 + 1)), a, 0.0)
        cross = jnp.where((ii >> log2) == (jj >> log2), 0.0, pair)
        t = t - dot(t, dot(cross, t))
        log2 += 1
    return t


def _unit_tri_solve(a, rhs):
    t = _unit_tri_inverse(a, _dot1).astype(BF16)
    x0 = _mm(t, rhs.astype(BF16))
    resid = rhs - x0 - _dot3(a, x0)
    return x0 + _mm(t, resid.astype(BF16))


def _mm_tn(a, b):
    return lax.dot_general(a, b, (((1,), (1,)), ((0,), (0,))), preferred_element_type=F32)


_DELTA_GROUP = 8


def _delta_kernel(k_ref, v_ref, q_ref, col_ref, row_ref, s0_ref, og_ref, an_ref, o_ref, sfin_ref,
                  km_s, nm_s, p_s, r_s, eg_s, sall_s, st_s, *, l):
    c = A_CHUNK
    dk = A_HEAD_DIM
    n_chunks = l // c
    g = min(_DELTA_GROUP, n_chunks)
    shape = (2 * g, c, c)
    ii = lax.broadcasted_iota(jnp.int32, shape, 1)
    jj = lax.broadcasted_iota(jnp.int32, shape, 2)
    lag = jnp.where(lax.broadcasted_iota(jnp.int32, shape, 0) >= g, jj - ii, ii - jj)
    incl = lag >= 0
    strict = lag > 0
    two = lambda x: jnp.concatenate([x, x], axis=0)

    def pass1(gi, carry):
        rows_blk = pl.ds(pl.multiple_of(gi * (g * c), g * c), g * c)
        chunks = pl.ds(gi * g, g)
        kc = k_ref[0, rows_blk, :].reshape(g, c, dk)
        vc = v_ref[0, rows_blk, :].reshape(g, c, dk)
        qc = q_ref[0, rows_blk, :].reshape(g, c, dk)
        cols = col_ref[0, 0, rows_blk, :].reshape(g, c, 8)
        rws = row_ref[0, 0, chunks]
        beta = jnp.concatenate([cols[:, :, 0:1], cols[:, :, 1:2]], axis=0)
        gcol = jnp.concatenate([cols[:, :, 2:3], cols[:, :, 3:4]], axis=0)
        grow = jnp.concatenate([rws[:, 0:1, :], rws[:, 1:2, :]], axis=0)
        dec = jnp.where(incl, jnp.exp(jnp.where(incl, gcol - grow, 0.0)), 0.0)
        kb = kc.astype(BF16)
        kk = two(_mm_nt(kb, kb))
        qk = two(_mm_nt(qc.astype(BF16), kb))
        eg = jnp.exp(gcol)
        k2, v2, q2 = two(kc), two(vc), two(qc)
        sol = _unit_tri_solve(jnp.where(strict, kk * dec * beta, 0.0),
                              jnp.concatenate([v2 * beta, k2 * (beta * eg)], axis=2))
        ub = sol[:, :, :dk].astype(BF16)
        wb = sol[:, :, dk:].astype(BF16)
        glast = jnp.concatenate([gcol[:g, c - 1:c, :], gcol[g:, 0:1, :]], axis=0)
        ke = (k2 * jnp.exp(glast - gcol)).astype(BF16)
        qkd = (qk * dec).astype(BF16)
        km = _mm_tn(ke, wb)
        nm = _mm_tn(ke, ub)
        pm = q2 * eg - _mm(qkd, wb)
        rm = _mm(qkd, ub)
        eglast = jnp.exp(glast)
        for d in range(2):
            sl = slice(d * g, (d + 1) * g)
            km_s[d, chunks] = km[sl].astype(BF16)
            nm_s[d, chunks] = nm[sl]
            p_s[d, rows_blk, :] = pm[sl].reshape(g * c, dk).astype(BF16)
            r_s[d, rows_blk, :] = rm[sl].reshape(g * c, dk)
            eg_s[d, chunks] = jnp.broadcast_to(eglast[sl], (g, V7X_SUBLANES, dk))
        return carry

    lax.fori_loop(0, n_chunks // g, pass1, 0)

    st_s[...] = s0_ref[0, 0]

    def pass2(n, carry):
        for d, ch in ((0, n), (1, n_chunks - 1 - n)):
            s = st_s[d]
            sb = s.astype(BF16)
            sall_s[d, ch] = sb
            st_s[d] = s * eg_s[d, ch][0:1, :] - jnp.dot(km_s[d, ch], sb, preferred_element_type=F32) + nm_s[d, ch]
        return carry

    lax.fori_loop(0, n_chunks, pass2, 0)
    sfin_ref[0, 0] = st_s[...]

    def pass3(gi, carry):
        rows_blk = pl.ds(pl.multiple_of(gi * (g * c), g * c), g * c)
        chunks = pl.ds(gi * g, g)
        o = None
        for d in range(2):
            term = _mm(p_s[d, rows_blk, :].reshape(g, c, dk), sall_s[d, chunks]) + r_s[d, rows_blk, :].reshape(g, c, dk)
            o = term if o is None else o + term
        o = o.reshape(g * c, dk)
        o = o * lax.rsqrt(jnp.mean(o * o, axis=-1, keepdims=True) + RMS_EPS) * an_ref[...]
        og = og_ref[0, rows_blk, :]
        o_ref[0, rows_blk, :] = (o * (og * jax.nn.sigmoid(og))).astype(o_ref.dtype)
        return carry

    lax.fori_loop(0, n_chunks // g, pass3, 0)


def _delta_call(act, z_og, gates, s0, a_norm):
    b, l, ch = act.shape
    heads = ch // 3 // A_HEAD_DIM
    dk = A_HEAD_DIM
    c = A_CHUNK
    n_chunks = l // c
    g4 = jnp.stack([gates[..., i * heads:(i + 1) * heads] for i in range(4)], axis=-1)
    col = jnp.pad(jnp.transpose(g4, (0, 2, 1, 3)), ((0, 0), (0, 0), (0, 0), (0, 4)))
    row = jnp.transpose(g4[..., 2:4].reshape(b, n_chunks, c, heads, 2), (0, 3, 1, 4, 2))
    row = jnp.pad(row, ((0, 0), (0, 0), (0, 0), (0, 6), (0, 0)))
    tok = lambda off: pl.BlockSpec((1, l, dk), functools.partial(lambda i, h, off: (i, 0, h + off), off=off))
    return pl.pallas_call(
        functools.partial(_delta_kernel, l=l),
        grid=(b, heads),
        in_specs=[tok(0), tok(heads), tok(2 * heads),
                  pl.BlockSpec((1, 1, l, 8), lambda i, h: (i, h, 0, 0)),
                  pl.BlockSpec((1, 1, n_chunks, 8, c), lambda i, h: (i, h, 0, 0, 0)),
                  pl.BlockSpec((1, 1, 2, dk, dk), lambda i, h: (i, h, 0, 0, 0)),
                  tok(0),
                  pl.BlockSpec((1, dk), lambda i, h: (0, 0))],
        out_specs=(tok(0), pl.BlockSpec((1, 1, 2, dk, dk), lambda i, h: (i, h, 0, 0, 0))),
        out_shape=(jax.ShapeDtypeStruct((b, l, heads * dk), BF16),
                   jax.ShapeDtypeStruct((b, heads, 2, dk, dk), F32)),
        scratch_shapes=[pltpu.VMEM((2, n_chunks, dk, dk), BF16), pltpu.VMEM((2, n_chunks, dk, dk), F32),
                        pltpu.VMEM((2, l, dk), BF16), pltpu.VMEM((2, l, dk), F32),
                        pltpu.VMEM((2, n_chunks, V7X_SUBLANES, dk), F32),
                        pltpu.VMEM((2, n_chunks, dk, dk), BF16), pltpu.VMEM((2, dk, dk), F32)],
        compiler_params=_cparams("parallel", "parallel"),
        name="delta_rule",
    )(act, act, act, col, row, s0, z_og, a_norm.reshape(1, dk).astype(F32))


def _gmlp_kernel(zu_ref, zv_ref, ws_ref, bs_ref, o_ref):
    groups = ws_ref.shape[0]
    gd = zu_ref.shape[1] // groups
    u = jax.nn.gelu(zu_ref[...])
    v = jax.nn.gelu(zv_ref[...])
    mu = jnp.mean(v, axis=-1, keepdims=True)
    var = jnp.mean(jnp.square(v - mu), axis=-1, keepdims=True)
    vn = ((v - mu) * lax.rsqrt(var + RMS_EPS)).astype(BF16)
    mixed = [jnp.dot(ws_ref[g], vn[:, g * gd:(g + 1) * gd], preferred_element_type=F32) + bs_ref[:, g:g + 1]
             for g in range(groups)]
    o_ref[...] = (u * jnp.concatenate(mixed, axis=1)).astype(o_ref.dtype)


def _gmlp_call(z_b, w_s, b_s):
    n_tok, two_w = z_b.shape
    bw = two_w // 2
    groups, p, _ = w_s.shape
    return pl.pallas_call(
        _gmlp_kernel,
        grid=(n_tok // p,),
        in_specs=[pl.BlockSpec((p, bw), lambda i: (i, 0)),
                  pl.BlockSpec((p, bw), lambda i: (i, 1)),
                  pl.BlockSpec((groups, p, p), lambda i: (0, 0, 0)),
                  pl.BlockSpec((p, groups), lambda i: (0, 0))],
        out_specs=pl.BlockSpec((p, bw), lambda i: (i, 0)),
        out_shape=jax.ShapeDtypeStruct((n_tok, bw), BF16),
        compiler_params=_cparams("parallel"),
        name="gmlp",
    )(z_b, z_b, w_s, b_s.T.astype(F32))


_FEAT_PAD = 64


def _filter_feats(l):
    pos = np.arange(l, dtype=np.float32)
    t = np.linspace(0.0, 1.0, l, dtype=np.float32)[:, None]
    bands = np.linspace(1e-4, C_POS_BANDS - 1, C_POS_BANDS, dtype=np.float32)
    ang = np.float32(2.0 * math.pi / l) * pos[:, None] * bands
    feats = np.concatenate([t, np.cos(ang), -np.sin(ang)], axis=-1).astype(np.float32)
    return np.pad(feats, ((0, 0), (0, _FEAT_PAD - feats.shape[1])))


def _filter_kernel(feat_ref, w1_ref, b1_ref, f1_ref, w2_ref, b2_ref, f2_ref, w3_ref, b3_ref, rate_ref, o_ref):
    feats = feat_ref[...]
    hid = jnp.sin(f1_ref[...] * (_dot3(feats, w1_ref[...]) + b1_ref[...]))
    hid = jnp.sin(f2_ref[...] * (_dot3(hid, w2_ref[...]) + b2_ref[...]))
    filt = _dot3(hid, w3_ref[...]) + b3_ref[...]
    filt = filt * jnp.exp(-feats[:, 0:1] * rate_ref[...])
    o_ref[...] = filt * lax.rsqrt(jnp.sum(filt * filt, axis=0, keepdims=True) + RMS_EPS)


def _filter_call(l, fw1, fb1, freq1, fw2, fb2, freq2, fw3, fb3, tc=256):
    hidden = fw1.shape[1]
    two_c = fw3.shape[1]
    cw = two_c // 2
    feats = jnp.asarray(_filter_feats(l))
    w1 = jnp.pad(fw1.astype(F32), ((0, _FEAT_PAD - fw1.shape[0]), (0, 0)))
    rate = np.abs(np.linspace(C_MIN_DECAY, C_MAX_DECAY, cw, dtype=np.float32))
    rate = jnp.asarray(np.concatenate([rate, rate]).reshape(1, two_c))
    row = lambda v: v.reshape(1, -1).astype(F32)
    full = lambda shape: pl.BlockSpec(shape, lambda j: (0, 0))
    return pl.pallas_call(
        _filter_kernel,
        grid=(two_c // tc,),
        in_specs=[full((l, _FEAT_PAD)), full((_FEAT_PAD, hidden)), full((1, hidden)), full((1, hidden)),
                  full((hidden, hidden)), full((1, hidden)), full((1, hidden)),
                  pl.BlockSpec((hidden, tc), lambda j: (0, j)), pl.BlockSpec((1, tc), lambda j: (0, j)),
                  pl.BlockSpec((1, tc), lambda j: (0, j))],
        out_specs=pl.BlockSpec((l, tc), lambda j: (0, j)),
        out_shape=jax.ShapeDtypeStruct((l, two_c), F32),
        compiler_params=_cparams("parallel"),
        name="hyena_filter",
    )(feats, w1, row(fb1), row(freq1), fw2.astype(F32), row(fb2), row(freq2), fw3.astype(F32), row(fb3), rate)


def _dft_sizes(l):
    n1 = 64 if l >= 4096 else 32
    return n1, (2 * l) // n1


def _hi_lo(x):
    x = jnp.asarray(x, F32)
    hi = x.astype(BF16)
    return hi, (x - hi.astype(F32)).astype(BF16)


def _dft_consts(l):
    n1, n2 = _dft_sizes(l)
    n = n1 * n2
    half = n1 // 2
    k1 = np.arange(n1)[None, :, None]
    m1 = np.arange(half)[None, None, :]
    m2 = np.arange(n2)[:, None, None]
    f1 = np.exp(-2j * np.pi * (m1 * k1 / n1 + m2 * k1 / n))
    fwd = np.concatenate([f1.real, f1.imag], axis=1)
    fh, fl = _hi_lo(fwd)
    fwd1 = jnp.concatenate([fh, fh, fl], axis=2)
    rt = np.transpose(f1.real, (0, 2, 1)) / n
    it = np.transpose(f1.imag, (0, 2, 1)) / n
    (rh, rl), (ih, il) = _hi_lo(rt), _hi_lo(it)
    inv1 = jnp.concatenate([rh, ih, rh, ih, rl, il], axis=2)
    kk = np.arange(n2)
    f2 = np.exp(-2j * np.pi * np.outer(kk, kk) / n2)
    m = np.block([[f2.real, -f2.imag], [f2.imag, f2.real]])
    mi = np.block([[f2.real, f2.imag], [-f2.imag, f2.real]])
    (mh, ml), (mih, mil) = _hi_lo(m), _hi_lo(mi)
    return fwd1, jnp.concatenate([mh, mh, ml], axis=1), jnp.concatenate([mih, mih, mil], axis=1), inv1


_DFT_UNROLL = 4


def _stack3(x):
    hi, lo = _split_bf16(x)
    return jnp.concatenate([hi, lo, hi], axis=0)


def _dft_pitch(n2):
    return n2 + V7X_SUBLANES


def _dft_level1(x_ref, fwd1_ref, ar_s, ai_s, n1, n2):
    half = n1 // 2
    pitch = _dft_pitch(n2)

    def body(j, carry):
        xs = x_ref[pl.ds(j, half, stride=n2), :]
        a = jnp.dot(fwd1_ref[j], _stack3(xs), preferred_element_type=F32)
        ar_s[pl.ds(j, n1, stride=pitch), :] = a[:n1]
        ai_s[pl.ds(j, n1, stride=pitch), :] = a[n1:]
        return carry

    lax.fori_loop(0, n2, body, 0, unroll=_DFT_UNROLL)


def _block_rows(k1, n2):
    return pl.ds(pl.multiple_of(k1 * _dft_pitch(n2), V7X_SUBLANES), n2)


def _pair_rows(pair, n2):
    return pl.ds(pl.multiple_of(pair * (2 * n2), 2 * n2), 2 * n2)


def _side_by_side(x, n2):
    return jnp.concatenate([x[:n2], x[n2:]], axis=1)


def _dft_level2(ar_s, ai_s, m3_ref, pair, n2):
    lo, hi = _block_rows(2 * pair, n2), _block_rows(2 * pair + 1, n2)
    blk = jnp.concatenate([jnp.concatenate([ar_s[lo, :], ar_s[hi, :]], axis=1),
                           jnp.concatenate([ai_s[lo, :], ai_s[hi, :]], axis=1)], axis=0)
    x = jnp.dot(m3_ref[...], _stack3(blk), preferred_element_type=F32)
    return x[:n2], x[n2:]


def _spectrum_kernel(hf_ref, hb_ref, fwd1_ref, m3_ref, o_ref, ar_s, ai_s, *, n1, n2):
    tc = o_ref.shape[2]
    for which, h_ref in enumerate((hf_ref, hb_ref)):
        _dft_level1(h_ref, fwd1_ref, ar_s, ai_s, n1, n2)

        def body(pair, carry):
            xr, xi = _dft_level2(ar_s, ai_s, m3_ref, pair, n2)
            for side in range(2):
                rows = pl.ds(pl.multiple_of((2 * pair + side) * n2, n2), n2)
                lanes = slice(side * tc, (side + 1) * tc)
                if which == 0:
                    o_ref[0, rows, :] = xr[:, lanes]
                    o_ref[1, rows, :] = xi[:, lanes]
                else:
                    o_ref[0, rows, :] += xr[:, lanes]
                    o_ref[1, rows, :] -= xi[:, lanes]
            return carry

        lax.fori_loop(0, n1 // 2, body, 0, unroll=_DFT_UNROLL // 2)


def _spectrum_call(filt, consts, tc=128):
    l, two_c = filt.shape
    cw = two_c // 2
    nb = cw // tc
    n1, n2 = _dft_sizes(l)
    fwd1, m3, _, _ = consts
    return pl.pallas_call(
        functools.partial(_spectrum_kernel, n1=n1, n2=n2),
        grid=(nb,),
        in_specs=[pl.BlockSpec((l, tc), lambda j: (0, j)),
                  pl.BlockSpec((l, tc), lambda j: (0, j + nb)),
                  pl.BlockSpec(fwd1.shape, lambda j: (0, 0, 0)),
                  pl.BlockSpec(m3.shape, lambda j: (0, 0))],
        out_specs=pl.BlockSpec((2, 2 * l, tc), lambda j: (0, 0, j)),
        out_shape=jax.ShapeDtypeStruct((2, 2 * l, cw), F32),
        scratch_shapes=[pltpu.VMEM((n1 * _dft_pitch(n2), tc), F32), pltpu.VMEM((n1 * _dft_pitch(n2), tc), F32)],
        compiler_params=_cparams("parallel"),
        name="hyena_spectrum",
    )(filt, filt, fwd1, m3)


def _longconv_kernel(u_ref, x0_ref, h_ref, skip_ref, fwd1_ref, m3_ref, mi3_ref, inv1_ref, o_ref,
                     ar_s, ai_s, y_s, *, l, n1, n2):
    half = n1 // 2
    _dft_level1(u_ref.at[0], fwd1_ref, ar_s, ai_s, n1, n2)

    tc = o_ref.shape[2]

    pitch = _dft_pitch(n2)

    def per_k1_pair(pair, carry):
        xr, xi = _dft_level2(ar_s, ai_s, m3_ref, pair, n2)
        rows = _pair_rows(pair, n2)
        hr = _side_by_side(h_ref[0, rows, :], n2)
        hi = _side_by_side(h_ref[1, rows, :], n2)
        y = jnp.concatenate([xr * hr - xi * hi, xr * hi + xi * hr], axis=0)
        b = jnp.dot(mi3_ref[...], _stack3(y), preferred_element_type=F32)
        for side in range(2):
            blk = _block_rows(2 * pair + side, n2)
            ar_s[blk, :] = b[:n2, side * tc:(side + 1) * tc]
            ai_s[blk, :] = b[n2:, side * tc:(side + 1) * tc]
        return carry

    lax.fori_loop(0, n1 // 2, per_k1_pair, 0, unroll=_DFT_UNROLL // 2)

    def per_n2(j, carry):
        br = ar_s[pl.ds(j, n1, stride=pitch), :]
        bi = ai_s[pl.ds(j, n1, stride=pitch), :]
        (brh, brl), (bih, bil) = _split_bf16(br), _split_bf16(bi)
        rhs = jnp.concatenate([brh, bih, brl, bil, brh, bih], axis=0)
        y_s[pl.ds(j, half, stride=pitch), :] = jnp.dot(inv1_ref[j], rhs, preferred_element_type=F32)
        return carry

    lax.fori_loop(0, n2, per_n2, 0, unroll=_DFT_UNROLL)

    def finish(i, carry):
        rows = pl.ds(pl.multiple_of(i * n2, n2), n2)
        u = u_ref[0, rows, :]
        y = y_s[_block_rows(i, n2), :]
        o_ref[0, rows, :] = (x0_ref[0, rows, :] * (y + skip_ref[...] * u)).astype(o_ref.dtype)
        return carry

    lax.fori_loop(0, half, finish, 0, unroll=_DFT_UNROLL)


def _longconv_call(u, x0, spec, skip, consts, tc=128):
    b, l, cw = u.shape
    n1, n2 = _dft_sizes(l)
    fwd1, m3, mi3, inv1 = consts
    tok = pl.BlockSpec((1, l, tc), lambda j, i: (i, 0, j))
    const = lambda a: pl.BlockSpec(a.shape, lambda j, i: (0,) * a.ndim)
    return pl.pallas_call(
        functools.partial(_longconv_kernel, l=l, n1=n1, n2=n2),
        grid=(cw // tc, b),
        in_specs=[tok, tok,
                  pl.BlockSpec((2, 2 * l, tc), lambda j, i: (0, 0, j)),
                  pl.BlockSpec((1, tc), lambda j, i: (0, j)),
                  const(fwd1), const(m3), const(mi3), const(inv1)],
        out_specs=tok,
        out_shape=jax.ShapeDtypeStruct((b, l, cw), BF16),
        scratch_shapes=[pltpu.VMEM((n1 * _dft_pitch(n2), tc), F32), pltpu.VMEM((n1 * _dft_pitch(n2), tc), F32),
                        pltpu.VMEM((n1 // 2 * _dft_pitch(n2), tc), F32)],
        compiler_params=_cparams("parallel", "parallel"),
        name="hyena_longconv",
    )(u, x0, spec, skip.reshape(1, cw).astype(F32), fwd1, m3, mi3, inv1)


def _layer_weights(i, p):
    w_in = p["w_in"][i]
    aw, bw, cw = p["w_br_a"].shape[1], p["w_br_b"].shape[1], p["w_br_c"].shape[1]
    heads = p["a_log"].shape[2]
    off_ba = 2 * aw
    off_q = off_ba + 4 * heads
    off_og = off_q + aw
    off_bu = off_og + aw
    off_cx = off_bu + 2 * bw
    off_gate = off_cx + 3 * cw
    cast = lambda w: w.astype(BF16)
    return {
        "w_kvq": cast(jnp.concatenate([w_in[:, :off_ba], w_in[:, off_q:off_og]], axis=1)),
        "w_ba": cast(jnp.pad(w_in[:, off_ba:off_q], ((0, 0), (0, V7X_LANES - 4 * heads)))),
        "w_og": cast(w_in[:, off_og:off_bu]),
        "w_b": cast(w_in[:, off_bu:off_cx]),
        "w_c": cast(w_in[:, off_cx:off_gate]),
        "w_gate": cast(w_in[:, off_gate:]),
        "w_br_a": cast(p["w_br_a"][i]), "w_br_b": cast(p["w_br_b"][i]), "w_br_c": cast(p["w_br_c"][i]),
        "w_out": cast(p["w_out"][i]),
        "a_conv": p["a_conv"][i].reshape(9, -1), "c_conv": p["c_conv"][i].reshape(9, -1),
        "a_log": p["a_log"][i], "a_dt_bias": p["a_dt_bias"][i], "a_norm": p["a_norm"][i],
        "b_ws": cast(p["b_ws"][i]), "b_bs": p["b_bs"][i], "c_skip": p["c_skip"][i],
        "filter": tuple(p[k][i] for k in ("c_fw1", "c_fb1", "c_freq1", "c_fw2", "c_fb2", "c_freq2", "c_fw3", "c_fb3")),
        "p_wq": cast(p["p_wq"][i]), "p_keys": cast(p["p_keys"][i]), "p_u": cast(p["p_u"][i]), "p_v": cast(p["p_v"][i]),
    }


def _delta_branch(xf, b, l, rows, cols, s0, lw):
    z_kvq = _matmul(xf, lw["w_kvq"]).reshape(b, l, -1)
    z_og = _matmul(xf, lw["w_og"]).reshape(b, l, -1)
    z_ba = _matmul(xf, lw["w_ba"]).reshape(b, l, -1)
    act = _conv_a_call(z_kvq, lw["a_conv"], rows, cols)
    gates = _gate_prep_call(z_ba, lw["a_log"], lw["a_dt_bias"])
    o_a, s_fin = _delta_call(act, z_og, gates, s0, lw["a_norm"])
    return o_a.reshape(b * l, -1), s_fin


def _token_mixer(xn, rows, cols, s0, lw):
    b, l, d = xn.shape
    xf = xn.reshape(b * l, d)
    o_a, s_fin = _delta_branch(xf, b, l, rows, cols, s0, lw)
    o_b = _gmlp_call(_matmul(xf, lw["w_b"]), lw["b_ws"], lw["b_bs"])
    x0, u = _conv_c_call(_matmul(xf, lw["w_c"]).reshape(b, l, -1), lw["c_conv"], rows, cols)
    consts = _dft_consts(l)
    spec = _spectrum_call(_filter_call(l, *lw["filter"]), consts)
    o_c = _longconv_call(u, x0, spec, lw["c_skip"], consts).reshape(b * l, -1)
    merged = _merge_call(o_a, o_b, o_c, lw["w_br_a"], lw["w_br_b"], lw["w_br_c"], _matmul(xf, lw["w_gate"]))
    return _matmul(merged, lw["w_out"]).reshape(b, l, d), s_fin


def _peer_layer(xn, lw):
    b, l, d = xn.shape
    return _peer(xn.reshape(b * l, d), lw["p_wq"], lw["p_keys"], lw["p_u"], lw["p_v"]).reshape(b, l, d)


def kernel(x, c, ctx, c_ctx, ada_w, ada_b, w_in, a_conv, a_log, a_dt_bias, a_norm, b_ws, b_bs, c_conv, c_fw1, c_fb1,
           c_freq1, c_fw2, c_fb2, c_freq2, c_fw3, c_fb3, c_skip, w_br_a, w_br_b, w_br_c, w_out, p_wq, p_keys, p_u,
           p_v, final_norm):
    params = dict(w_in=w_in, a_conv=a_conv, a_log=a_log, a_dt_bias=a_dt_bias, a_norm=a_norm, b_ws=b_ws, b_bs=b_bs,
                  c_conv=c_conv, c_fw1=c_fw1, c_fb1=c_fb1, c_freq1=c_freq1, c_fw2=c_fw2, c_fb2=c_fb2, c_freq2=c_freq2,
                  c_fw3=c_fw3, c_fb3=c_fb3, c_skip=c_skip, w_br_a=w_br_a, w_br_b=w_br_b, w_br_c=w_br_c, w_out=w_out,
                  p_wq=p_wq, p_keys=p_keys, p_u=p_u, p_v=p_v)
    b, l, d = x.shape
    depth = w_in.shape[0]
    rows = l // GRID_W
    l_ctx = ctx.shape[1]
    heads = a_log.shape[2]
    s_zero = jnp.zeros((b, heads, 2, A_HEAD_DIM, A_HEAD_DIM), F32)
    cvec = jnp.concatenate([c, c_ctx[None, :], jnp.zeros((V7X_SUBLANES - b - 1, d), F32)], axis=0)

    h, h_pending = x, None
    hc, hc_pending = ctx, None
    for i in range(depth):
        lw = _layer_weights(i, params)
        mod_all = _mod_call(cvec, ada_w[i], ada_b[i])
        mod = mod_all[:b].reshape(b, N_MOD, 1, d)
        mod_c = mod_all[b].reshape(N_MOD, 1, 1, d)

        def norm(stream, pending, scale, shift):
            if pending is None:
                return stream, _norm_mod_call(stream, scale, shift)[1]
            return _norm_mod_call(stream, scale, shift, delta=pending[0], gate=pending[1], emit_h=True)

        hc, xnc = norm(hc, hc_pending, mod_c[1], mod_c[0])
        if i == depth - 1:
            _, s_ctx = _delta_branch(xnc.reshape(b * l_ctx, d), b, l_ctx, 1, l_ctx, s_zero, lw)
        else:
            out_c, s_ctx = _token_mixer(xnc, 1, l_ctx, s_zero, lw)
            hc, xnc2 = norm(hc, (out_c, mod_c[2]), mod_c[4], mod_c[3])
            hc_pending = (_peer_layer(xnc2, lw), mod_c[5])
        h, xn = norm(h, h_pending, mod[:, 1], mod[:, 0])
        out, _ = _token_mixer(xn, rows, GRID_W, s_ctx, lw)
        h, xn2 = norm(h, (out, mod[:, 2]), mod[:, 4], mod[:, 3])
        h_pending = (_peer_layer(xn2, lw), mod[:, 5])
    return _norm_mod_call(h, final_norm.reshape(1, 1, d), delta=h_pending[0], gate=h_pending[1], out_dtype=F32)[1]
```

```python
import functools
import math

import numpy as np
import jax
import jax.numpy as jnp
from jax import lax
from jax.experimental import pallas as pl
from jax.experimental.pallas import tpu as pltpu

F32 = jnp.float32
BF16 = jnp.bfloat16

V7X_LANES = 128
V7X_SUBLANES = 8
V7X_VMEM_LIMIT_BYTES = 56 * 1024 * 1024

GRID_W = 64
RMS_EPS = 1e-6
N_MOD = 6
A_HEAD_DIM = 128
A_CHUNK = 64
B_CHUNK = 128
C_POS_BANDS = 16
C_MIN_DECAY = math.log(1e-2) / 1.5
C_MAX_DECAY = math.log(1e-2) / 0.3
N_BRANCH = 3
P_KEYS = 128
P_TOPK = 16


def _cparams(*sem):
    return pltpu.CompilerParams(dimension_semantics=sem, vmem_limit_bytes=V7X_VMEM_LIMIT_BYTES)


def _mod_kernel(c_ref, w_ref, b_ref, o_ref):
    c = c_ref[...]
    a = (c * jax.nn.sigmoid(c)).astype(BF16)
    o_ref[...] = jnp.dot(a, w_ref[...].astype(BF16), preferred_element_type=F32) + b_ref[...]


def _mod_call(cvec, ada_w, ada_b):
    rows, d = cvec.shape
    n = ada_w.shape[1]
    tn = 1024
    return pl.pallas_call(
        _mod_kernel,
        grid=(n // tn,),
        in_specs=[pl.BlockSpec((rows, d), lambda j: (0, 0)),
                  pl.BlockSpec((d, tn), lambda j: (0, j)),
                  pl.BlockSpec((1, tn), lambda j: (0, j))],
        out_specs=pl.BlockSpec((rows, tn), lambda j: (0, j)),
        out_shape=jax.ShapeDtypeStruct((rows, n), F32),
        compiler_params=_cparams("parallel"),
        name="ada_mod",
    )(cvec, ada_w, ada_b.reshape(1, n))


def _norm_mod_kernel(*refs, with_delta, with_shift, emit_h):
    refs = list(refs)
    h = refs.pop(0)[0]
    if with_delta:
        d_ref, g_ref = refs.pop(0), refs.pop(0)
        h = h + g_ref[0] * d_ref[0]
    sc_ref = refs.pop(0)
    xn = h * lax.rsqrt(jnp.mean(h * h, axis=-1, keepdims=True) + RMS_EPS)
    if with_shift:
        xn = xn * (1.0 + sc_ref[0]) + refs.pop(0)[0]
    else:
        xn = xn * sc_ref[0]
    if emit_h:
        refs.pop(0)[0] = h
    refs.pop(0)[0] = xn.astype(refs[0].dtype)


def _norm_mod_call(h, scale, shift=None, delta=None, gate=None, emit_h=False, out_dtype=BF16, tl=256):
    b, l, d = h.shape
    tl = min(tl, l)
    tok = pl.BlockSpec((1, tl, d), lambda i, j: (i, j, 0))

    def vec(v):
        if v.shape[0] == 1:
            return pl.BlockSpec((1, 1, d), lambda i, j: (0, 0, 0))
        return pl.BlockSpec((1, 1, d), lambda i, j: (i, 0, 0))

    args, in_specs = [h], [tok]
    if delta is not None:
        args += [delta, gate]
        in_specs += [tok, vec(gate)]
    args.append(scale)
    in_specs.append(vec(scale))
    if shift is not None:
        args.append(shift)
        in_specs.append(vec(shift))
    out_shape, out_specs = [], []
    if emit_h:
        out_shape.append(jax.ShapeDtypeStruct((b, l, d), F32))
        out_specs.append(tok)
    out_shape.append(jax.ShapeDtypeStruct((b, l, d), out_dtype))
    out_specs.append(tok)
    res = pl.pallas_call(
        functools.partial(_norm_mod_kernel, with_delta=delta is not None, with_shift=shift is not None, emit_h=emit_h),
        grid=(b, l // tl),
        in_specs=in_specs, out_specs=tuple(out_specs), out_shape=tuple(out_shape),
        compiler_params=_cparams("parallel", "parallel"),
        name="norm_mod",
    )(*args)
    return (res[0], res[1]) if emit_h else (None, res[0])


def _mm_kernel(a_ref, w_ref, o_ref):
    o_ref[...] = jnp.dot(a_ref[...], w_ref[...], preferred_element_type=F32).astype(o_ref.dtype)


def _pick_tile(n, pref):
    t = min(pref, n)
    while n % t:
        t //= 2
    return t


def _matmul(a, w, out_dtype=F32, tm=512, tn=1024, ncols=None):
    m, k = a.shape
    n = w.shape[1] if ncols is None else ncols
    tm = _pick_tile(m, tm)
    tn = _pick_tile(n, tn)
    return pl.pallas_call(
        _mm_kernel,
        grid=(n // tn, m // tm),
        in_specs=[pl.BlockSpec((tm, k), lambda j, i: (i, 0)),
                  pl.BlockSpec((k, tn), lambda j, i: (0, j))],
        out_specs=pl.BlockSpec((tm, tn), lambda j, i: (i, j)),
        out_shape=jax.ShapeDtypeStruct((m, n), out_dtype),
        compiler_params=_cparams("parallel", "parallel"),
        name="matmul",
    )(a, w)


def _w_in_prep_kernel(cur_ref, nxt_ref, o_ref, *, first_shifted, shift):
    j = pl.program_id(0)

    @pl.when(j < first_shifted)
    def _copy():
        o_ref[...] = cur_ref[...].astype(o_ref.dtype)

    @pl.when(j >= first_shifted)
    def _shifted():
        o_ref[...] = jnp.concatenate([cur_ref[:, shift:], nxt_ref[:, :shift]], axis=1).astype(o_ref.dtype)


def _w_in_prep_call(w_in, layer, cut0, cut1):
    _, k, n = w_in.shape
    shift = cut1 - cut0
    n_out = n - shift
    tb = _pick_tile(math.gcd(n_out, cut0), 512)
    last_in = (n - 1) // tb
    return pl.pallas_call(
        functools.partial(_w_in_prep_kernel, first_shifted=cut0 // tb, shift=shift),
        grid=(n_out // tb,),
        in_specs=[pl.BlockSpec((None, k, tb), lambda j: (layer, 0, j)),
                  pl.BlockSpec((None, k, tb), lambda j: (layer, 0, jnp.minimum(j + 1, last_in)))],
        out_specs=pl.BlockSpec((k, tb), lambda j: (0, j)),
        out_shape=jax.ShapeDtypeStruct((k, n_out), BF16),
        compiler_params=_cparams("parallel"),
        name="w_in_prep",
    )(w_in, w_in)


def _merge_kernel(oa_ref, ob_ref, oc_ref, wa_ref, wb_ref, wc_ref, ga_ref, gb_ref, gc_ref, o_ref):
    acc = jax.nn.sigmoid(ga_ref[...]) * jnp.dot(oa_ref[...], wa_ref[...], preferred_element_type=F32)
    acc += jax.nn.sigmoid(gb_ref[...]) * jnp.dot(ob_ref[...], wb_ref[...], preferred_element_type=F32)
    acc += jax.nn.sigmoid(gc_ref[...]) * jnp.dot(oc_ref[...], wc_ref[...], preferred_element_type=F32)
    o_ref[...] = acc.astype(o_ref.dtype)


def _merge_call(o_a, o_b, o_c, w_a, w_b, w_c, z, gate_col, tm=512, tn=512):
    n_tok = o_a.shape[0]
    d = w_a.shape[1]
    tm = _pick_tile(n_tok, tm)
    tn = _pick_tile(math.gcd(d, gate_col) if gate_col else d, tn)
    nb = d // tn
    g0 = gate_col // tn

    def act(o):
        return pl.BlockSpec((tm, o.shape[1]), lambda j, i: (i, 0))

    def wgt(w):
        return pl.BlockSpec((w.shape[0], tn), lambda j, i: (0, j))

    def gate(br):
        return pl.BlockSpec((tm, tn), lambda j, i: (i, g0 + j + br * nb))

    return pl.pallas_call(
        _merge_kernel,
        grid=(nb, n_tok // tm),
        in_specs=[act(o_a), act(o_b), act(o_c), wgt(w_a), wgt(w_b), wgt(w_c), gate(0), gate(1), gate(2)],
        out_specs=pl.BlockSpec((tm, tn), lambda j, i: (i, j)),
        out_shape=jax.ShapeDtypeStruct((n_tok, d), BF16),
        compiler_params=_cparams("parallel", "parallel"),
        name="branch_merge",
    )(o_a, o_b, o_c, w_a, w_b, w_c, z, z, z)


def _topk_rows(s, payload, k):
    n, t = s.shape
    iota = lax.broadcasted_iota(jnp.int32, (n, t), 0).astype(F32)
    riota = lax.broadcasted_iota(jnp.int32, (k, t), 0)

    def body(r, carry):
        s, tv, tp = carry
        m = jnp.max(s, axis=0, keepdims=True)
        idx = jnp.min(jnp.where(s == m, iota, float(n)), axis=0, keepdims=True)
        sel = iota == idx
        p = idx if payload is None else jnp.max(jnp.where(sel, payload, -1.0), axis=0, keepdims=True)
        s = jnp.where(sel, -jnp.inf, s)
        tv = jnp.where(riota == r, m, tv)
        tp = jnp.where(riota == r, p, tp)
        return s, tv, tp

    _, tv, tp = lax.fori_loop(0, k, body, (s, jnp.zeros((k, t), F32), jnp.zeros((k, t), F32)))
    return tv, tp


def _peer_route_kernel(x_ref, wq_ref, keys_ref, e_ref, g_ref):
    t = x_ref.shape[0]
    q = jnp.dot(x_ref[...], wq_ref[...], preferred_element_type=F32).astype(BF16)
    half = q.shape[1] // 2
    tops = []
    for p in range(2):
        s_t = lax.dot_general(keys_ref[0, p], q[:, p * half:(p + 1) * half],
                              (((1,), (1,)), ((), ())), preferred_element_type=F32)
        tops.append(_topk_rows(s_t, None, P_TOPK))
    (s0, i0), (s1, i1) = tops
    cand, cid = [], []
    for a in range(P_TOPK):
        n_b = P_TOPK // (a + 1)
        rows = -(-n_b // V7X_SUBLANES) * V7X_SUBLANES
        keep = lax.broadcasted_iota(jnp.int32, (rows, t), 0) < n_b
        cand.append(jnp.where(keep, jnp.broadcast_to(s0[a:a + 1], (rows, t)) + s1[:rows], -jnp.inf))
        cid.append(jnp.broadcast_to(i0[a:a + 1], (rows, t)) * float(P_KEYS) + i1[:rows])
    best_s, best_e = _topk_rows(jnp.concatenate(cand, axis=0), jnp.concatenate(cid, axis=0), P_TOPK)
    ex = jnp.exp(best_s - jnp.max(best_s, axis=0, keepdims=True))
    g_ref[0] = ex / jnp.sum(ex, axis=0, keepdims=True)
    e_ref[0] = best_e.astype(jnp.int32)


def _peer_route_call(xn, wq, keys, tt=512):
    n_tok, d = xn.shape
    heads = keys.shape[0]
    qd = wq.shape[1] // heads
    tt = _pick_tile(n_tok, tt)
    return pl.pallas_call(
        _peer_route_kernel,
        grid=(n_tok // tt, heads),
        in_specs=[pl.BlockSpec((tt, d), lambda i, h: (i, 0)),
                  pl.BlockSpec((d, qd), lambda i, h: (0, h)),
                  pl.BlockSpec((1,) + keys.shape[1:], lambda i, h: (h, 0, 0, 0))],
        out_specs=(pl.BlockSpec((1, P_TOPK, tt), lambda i, h: (h, 0, i)),
                   pl.BlockSpec((1, P_TOPK, tt), lambda i, h: (h, 0, i))),
        out_shape=(jax.ShapeDtypeStruct((heads, P_TOPK, n_tok), jnp.int32),
                   jax.ShapeDtypeStruct((heads, P_TOPK, n_tok), F32)),
        compiler_params=_cparams("parallel", "parallel"),
        name="peer_route",
    )(xn, wq, keys)


_PEER_GROUP = 16
_PEER_STAGE_PITCH = P_KEYS + V7X_SUBLANES


def _peer_dense_kernel(x_ref, e_ref, g_ref, u_ref, v_ref, o_ref, gmat_ref, stage_ref, w_ref, *, eb, n_blocks):
    t = x_ref.shape[0]
    j = pl.program_id(1)
    cur = j % 2

    @pl.when(j == 0)
    def _build_gate_matrix():
        o_ref[...] = jnp.zeros_like(o_ref)
        w_ref[1] = jnp.zeros(w_ref.shape[1:], w_ref.dtype)
        sub = lax.broadcasted_iota(jnp.int32, (P_KEYS, e_ref.shape[1]), 0)

        def per_group(grp, carry):
            t0 = pl.multiple_of(grp * _PEER_GROUP, _PEER_GROUP)
            for s in range(_PEER_GROUP):
                e = e_ref[pl.ds(t0 + s, 1), :]
                g = g_ref[pl.ds(t0 + s, 1), :]
                hit_a = sub == (e >> 7)
                hit_b = sub == (e & (P_KEYS - 1))
                a_mat = jnp.where(hit_a, g, 0.0).astype(BF16)
                b_mat = jnp.where(hit_b, 1.0, 0.0).astype(BF16)
                stage_ref[s * _PEER_STAGE_PITCH:s * _PEER_STAGE_PITCH + P_KEYS, :] = _mm_nt(a_mat, b_mat)
            for i1 in range(P_KEYS):
                gmat_ref[pl.ds(t0, _PEER_GROUP), i1 * P_KEYS:(i1 + 1) * P_KEYS] = (
                    stage_ref[pl.ds(i1, _PEER_GROUP, stride=_PEER_STAGE_PITCH), :].astype(BF16))
            return carry

        lax.fori_loop(0, t // _PEER_GROUP, per_group, 0)

    o_ref[...] += jnp.dot(w_ref[1 - cur], v_ref[...], preferred_element_type=F32)
    blk = jnp.minimum(j, n_blocks - 1)
    gsel = gmat_ref[:, pl.ds(pl.multiple_of(blk * eb, eb), eb)]
    w_ref[cur] = (gsel.astype(F32) * jax.nn.gelu(_mm_nt(x_ref[...], u_ref[...]))).astype(BF16)


def _peer_dense_call(xn, e_nat, g_nat, u_tab, v_tab, tt=512, eb=256):
    n_tok, d = xn.shape
    n_exp = u_tab.shape[0]
    r = e_nat.shape[1]
    tt = _pick_tile(n_tok, tt)
    n_blocks = n_exp // eb
    return pl.pallas_call(
        functools.partial(_peer_dense_kernel, eb=eb, n_blocks=n_blocks),
        grid=(n_tok // tt, n_blocks + 1),
        in_specs=[pl.BlockSpec((tt, d), lambda i, j: (i, 0)),
                  pl.BlockSpec((tt, r), lambda i, j: (i, 0)),
                  pl.BlockSpec((tt, r), lambda i, j: (i, 0)),
                  pl.BlockSpec((eb, d), lambda i, j: (jnp.minimum(j, n_blocks - 1), 0)),
                  pl.BlockSpec((eb, d), lambda i, j: (jnp.maximum(j - 1, 0), 0))],
        out_specs=pl.BlockSpec((tt, d), lambda i, j: (i, 0)),
        out_shape=jax.ShapeDtypeStruct((n_tok, d), F32),
        scratch_shapes=[pltpu.VMEM((tt, n_exp), BF16), pltpu.VMEM((_PEER_GROUP * _PEER_STAGE_PITCH, P_KEYS), F32),
                        pltpu.VMEM((2, tt, eb), BF16)],
        compiler_params=_cparams("parallel", "arbitrary"),
        name="peer_dense",
    )(xn, e_nat, g_nat, u_tab, v_tab)


def _peer(xn, wq, keys, u_tab, v_tab):
    e_t, g_t = _peer_route_call(xn, wq, keys)
    heads = keys.shape[0]
    n_tok = xn.shape[0]
    e_nat = e_t.reshape(heads * P_TOPK, n_tok).T
    g_nat = g_t.reshape(heads * P_TOPK, n_tok).T
    return _peer_dense_call(xn, e_nat, g_nat, u_tab, v_tab)


def _split_bf16(x):
    hi = x.astype(BF16)
    lo = (x - hi.astype(F32)).astype(BF16)
    return hi, lo


def _mm(a, b):
    if a.ndim == 3:
        return lax.dot_general(a, b, (((2,), (1,)), ((0,), (0,))), preferred_element_type=F32)
    return jnp.dot(a, b, preferred_element_type=F32)


def _mm_nt(a, b):
    if a.ndim == 3:
        return lax.dot_general(a, b, (((2,), (2,)), ((0,), (0,))), preferred_element_type=F32)
    return lax.dot_general(a, b, (((1,), (1,)), ((), ())), preferred_element_type=F32)


def _dot3(a, b):
    ah, al = _split_bf16(a)
    bh, bl = _split_bf16(b)
    return _mm(ah, bh) + _mm(ah, bl) + _mm(al, bh)


_CONV_FILL_ROWS = 512


def _conv_pad_rows(cols):
    return -(-(cols + 1) // V7X_SUBLANES) * V7X_SUBLANES


def _conv_fill(x_ref, x0_s, xm_s, xq_s, *, l, cols, pad):
    tc = x0_s.shape[1]
    zeros = jnp.zeros((pad, tc), F32)
    for s in (x0_s, xm_s, xq_s):
        s[0:pad, :] = zeros
        s[pad + l:pad + l + pad, :] = zeros
    step = min(_CONV_FILL_ROWS, l)
    for s0 in range(0, l, step):
        x0_s[pad + s0:pad + s0 + step, :] = x_ref[s0:s0 + step, :]
    col = lax.broadcasted_iota(jnp.int32, (step, tc), 0) & (cols - 1)
    for s0 in range(0, l, step):
        xm_s[pad + s0:pad + s0 + step, :] = jnp.where(col >= 1, x0_s[pad + s0 - 1:pad + s0 - 1 + step, :], 0.0)
        xq_s[pad + s0:pad + s0 + step, :] = jnp.where(col <= cols - 2, x0_s[pad + s0 + 1:pad + s0 + 1 + step, :], 0.0)


def _conv_rows(x0_s, xm_s, xq_s, w_ref, t0, n, *, rows, cols, pad):
    acc = None
    for dr in ((-1, 0, 1) if rows > 1 else (0,)):
        base = pl.multiple_of(t0 + pad + dr * cols, V7X_SUBLANES)
        for dc, src in ((-1, xm_s), (0, x0_s), (1, xq_s)):
            tap = (dr + 1) * 3 + dc + 1
            term = src[pl.ds(base, n), :] * w_ref[tap:tap + 1, :]
            acc = term if acc is None else acc + term
    return acc


_CONV_CHUNK = 128


def _conv_a_kernel(z_ref, w_ref, o_ref, x0_s, xm_s, xq_s, *, l, rows, cols, pad, blocks_per_part):
    _conv_fill(z_ref.at[0], x0_s, xm_s, xq_s, l=l, cols=cols, pad=pad)
    part = pl.program_id(1) // blocks_per_part
    use_norm = part != 1
    post = jnp.where(part == 2, A_HEAD_DIM ** -0.5, 1.0)
    tc = o_ref.shape[2]

    def chunk(i, carry):
        t0 = pl.multiple_of(i * _CONV_CHUNK, _CONV_CHUNK)
        y = _conv_rows(x0_s, xm_s, xq_s, w_ref, t0, _CONV_CHUNK, rows=rows, cols=cols, pad=pad)
        y = y * jax.nn.sigmoid(y)
        outs = []
        for h0 in range(0, tc, A_HEAD_DIM):
            yh = y[:, h0:h0 + A_HEAD_DIM]
            inv = lax.rsqrt(jnp.sum(yh * yh, axis=-1, keepdims=True) + RMS_EPS) * post
            outs.append(yh * jnp.where(use_norm, inv, 1.0))
        o_ref[0, pl.ds(t0, _CONV_CHUNK), :] = jnp.concatenate(outs, axis=1) if len(outs) > 1 else outs[0]
        return carry

    lax.fori_loop(0, l // _CONV_CHUNK, chunk, 0, unroll=2)


def _conv_a_call(z_kvq, w9, rows, cols, tc=256):
    b, l, _ = z_kvq.shape
    ch = w9.shape[1]
    pad = _conv_pad_rows(cols)
    scr = pltpu.VMEM((l + 2 * pad, tc), F32)
    return pl.pallas_call(
        functools.partial(_conv_a_kernel, l=l, rows=rows, cols=cols, pad=pad, blocks_per_part=ch // 3 // tc),
        grid=(b, ch // tc),
        in_specs=[pl.BlockSpec((1, l, tc), lambda i, j: (i, 0, j)),
                  pl.BlockSpec((9, tc), lambda i, j: (0, j))],
        out_specs=pl.BlockSpec((1, l, tc), lambda i, j: (i, 0, j)),
        out_shape=jax.ShapeDtypeStruct((b, l, ch), F32),
        scratch_shapes=[scr, scr, scr],
        compiler_params=_cparams("parallel", "parallel"),
        name="conv_a",
    )(z_kvq, w9)


def _conv_c_kernel(z0_ref, z1_ref, z2_ref, w0_ref, w1_ref, w2_ref, x0_ref, u_ref, *scr, l, rows, cols, pad):
    parts = ((z0_ref, w0_ref, scr[0:3]), (z1_ref, w1_ref, scr[3:6]), (z2_ref, w2_ref, scr[6:9]))
    for z_ref, _, s in parts:
        _conv_fill(z_ref.at[0], *s, l=l, cols=cols, pad=pad)

    def chunk(i, carry):
        t0 = pl.multiple_of(i * _CONV_CHUNK, _CONV_CHUNK)
        y = [_conv_rows(*s, w_ref, t0, _CONV_CHUNK, rows=rows, cols=cols, pad=pad) for _, w_ref, s in parts]
        x0_ref[0, pl.ds(t0, _CONV_CHUNK), :] = y[0]
        u_ref[0, pl.ds(t0, _CONV_CHUNK), :] = y[1] * y[2]
        return carry

    lax.fori_loop(0, l // _CONV_CHUNK, chunk, 0, unroll=2)


def _conv_c_call(z_c, w9, rows, cols, col0=0, tc=128):
    b, l, _ = z_c.shape
    cw = w9.shape[1] // 3
    nb = cw // tc
    c0 = col0 // tc
    pad = _conv_pad_rows(cols)
    scr = pltpu.VMEM((l + 2 * pad, tc), F32)
    zspec = [pl.BlockSpec((1, l, tc), functools.partial(lambda i, j, p: (i, 0, c0 + j + p * nb), p=p)) for p in range(3)]
    wspec = [pl.BlockSpec((9, tc), functools.partial(lambda i, j, p: (0, j + p * nb), p=p)) for p in range(3)]
    ospec = pl.BlockSpec((1, l, tc), lambda i, j: (i, 0, j))
    return pl.pallas_call(
        functools.partial(_conv_c_kernel, l=l, rows=rows, cols=cols, pad=pad),
        grid=(b, nb),
        in_specs=zspec + wspec,
        out_specs=(ospec, ospec),
        out_shape=(jax.ShapeDtypeStruct((b, l, cw), F32), jax.ShapeDtypeStruct((b, l, cw), F32)),
        scratch_shapes=[scr] * 9,
        compiler_params=_cparams("parallel", "parallel"),
        name="conv_c",
    )(z_c, z_c, z_c, w9, w9, w9)


def _gate_prep_kernel(z_ref, alog_ref, dtb_ref, o_ref, *, l, heads):
    lanes = z_ref.shape[2]
    ii = lax.broadcasted_iota(jnp.int32, (A_CHUNK, A_CHUNK), 0)
    jj = lax.broadcasted_iota(jnp.int32, (A_CHUNK, A_CHUNK), 1)
    lower = jnp.where(ii >= jj, 1.0, 0.0)
    upper = jnp.where(ii <= jj, 1.0, 0.0)
    lane = lax.broadcasted_iota(jnp.int32, (A_CHUNK, lanes), 1)

    def chunk(n, carry):
        r0 = pl.multiple_of(n * A_CHUNK, A_CHUNK)
        z = z_ref[0, pl.ds(r0, A_CHUNK), :]
        beta = jax.nn.sigmoid(z)
        x = z + dtb_ref[...]
        softplus = jnp.maximum(x, 0.0) + jnp.log(1.0 + jnp.exp(-jnp.abs(x)))
        la = -jnp.exp(alog_ref[...]) * softplus
        pre = _dot3(lower, la)
        suf = _dot3(upper, la)
        o_ref[0, pl.ds(r0, A_CHUNK), :] = jnp.where(lane < 2 * heads, beta, jnp.where(lane < 3 * heads, pre, suf))
        return carry

    lax.fori_loop(0, l // A_CHUNK, chunk, 0)


def _gate_prep_call(z_ba, a_log, a_dt_bias):
    b, l, lanes = z_ba.shape
    heads = a_log.shape[1]
    pad = lambda p: jnp.pad(p.reshape(1, 2 * heads).astype(F32), ((0, 0), (2 * heads, lanes - 4 * heads)))
    return pl.pallas_call(
        functools.partial(_gate_prep_kernel, l=l, heads=heads),
        grid=(b,),
        in_specs=[pl.BlockSpec((1, l, lanes), lambda i: (i, 0, 0)),
                  pl.BlockSpec((1, lanes), lambda i: (0, 0)),
                  pl.BlockSpec((1, lanes), lambda i: (0, 0))],
        out_specs=pl.BlockSpec((1, l, lanes), lambda i: (i, 0, 0)),
        out_shape=jax.ShapeDtypeStruct((b, l, lanes), F32),
        compiler_params=_cparams("parallel"),
        name="gate_prep",
    )(z_ba, pad(a_log), pad(a_dt_bias))


_TRI_BASE_LOG2 = 3


def _dot1(a, b):
    return _mm(a.astype(BF16), b.astype(BF16))


def _unit_tri_inverse(a, dot):
    c = a.shape[-1]
    ii = lax.broadcasted_iota(jnp.int32, a.shape, a.ndim - 2)
    jj = lax.broadcasted_iota(jnp.int32, a.shape, a.ndim - 1)
    eye = jnp.where(ii == jj, 1.0, 0.0)
    p = -jnp.where((ii >> _TRI_BASE_LOG2) == (jj >> _TRI_BASE_LOG2), a, 0.0)
    t = eye + p
    span = 2
    while span < (1 << _TRI_BASE_LOG2):
        p = dot(p, p)
        t = t + dot(t, p)
        span *= 2
    log2 = _TRI_BASE_LOG2
    while (1 << log2) < c:
        pair = jnp.where((ii >> (log2 + 1)) == (jj >> (log2 + 1)), a, 0.0)
        cross = jnp.where((ii >> log2) == (jj >> log2), 0.0, pair)
        t = t - dot(t, dot(cross, t))
        log2 += 1
    return t


def _unit_tri_solve(a, rhs):
    t = _unit_tri_inverse(a, _dot1).astype(BF16)
    x0 = _mm(t, rhs.astype(BF16))
    resid = rhs - x0 - _dot3(a, x0)
    return x0 + _mm(t, resid.astype(BF16))


def _mm_tn(a, b):
    return lax.dot_general(a, b, (((1,), (1,)), ((0,), (0,))), preferred_element_type=F32)


_DELTA_GROUP = 8


def _delta_kernel(k_ref, v_ref, q_ref, col_ref, row_ref, s0_ref, og_ref, an_ref, o_ref, sfin_ref,
                  km_s, nm_s, p_s, r_s, eg_s, sall_s, st_s, *, l):
    c = A_CHUNK
    dk = A_HEAD_DIM
    n_chunks = l // c
    g = min(_DELTA_GROUP, n_chunks)
    shape = (2 * g, c, c)
    ii = lax.broadcasted_iota(jnp.int32, shape, 1)
    jj = lax.broadcasted_iota(jnp.int32, shape, 2)
    lag = jnp.where(lax.broadcasted_iota(jnp.int32, shape, 0) >= g, jj - ii, ii - jj)
    incl = lag >= 0
    strict = lag > 0
    two = lambda x: jnp.concatenate([x, x], axis=0)

    def pass1(gi, carry):
        rows_blk = pl.ds(pl.multiple_of(gi * (g * c), g * c), g * c)
        chunks = pl.ds(gi * g, g)
        kc = k_ref[0, rows_blk, :].reshape(g, c, dk)
        vc = v_ref[0, rows_blk, :].reshape(g, c, dk)
        qc = q_ref[0, rows_blk, :].reshape(g, c, dk)
        cols = col_ref[0, 0, rows_blk, :].reshape(g, c, 8)
        rws = row_ref[0, 0, chunks]
        beta = jnp.concatenate([cols[:, :, 0:1], cols[:, :, 1:2]], axis=0)
        gcol = jnp.concatenate([cols[:, :, 2:3], cols[:, :, 3:4]], axis=0)
        grow = jnp.concatenate([rws[:, 0:1, :], rws[:, 1:2, :]], axis=0)
        dec = jnp.where(incl, jnp.exp(jnp.where(incl, gcol - grow, 0.0)), 0.0)
        kb = kc.astype(BF16)
        kk = two(_mm_nt(kb, kb))
        qk = two(_mm_nt(qc.astype(BF16), kb))
        eg = jnp.exp(gcol)
        k2, v2, q2 = two(kc), two(vc), two(qc)
        sol = _unit_tri_solve(jnp.where(strict, kk * dec * beta, 0.0),
                              jnp.concatenate([v2 * beta, k2 * (beta * eg)], axis=2))
        ub = sol[:, :, :dk].astype(BF16)
        wb = sol[:, :, dk:].astype(BF16)
        glast = jnp.concatenate([gcol[:g, c - 1:c, :], gcol[g:, 0:1, :]], axis=0)
        ke = (k2 * jnp.exp(glast - gcol)).astype(BF16)
        qkd = (qk * dec).astype(BF16)
        km = _mm_tn(ke, wb)
        nm = _mm_tn(ke, ub)
        pm = q2 * eg - _mm(qkd, wb)
        rm = _mm(qkd, ub)
        eglast = jnp.exp(glast)
        for d in range(2):
            sl = slice(d * g, (d + 1) * g)
            km_s[d, chunks] = km[sl].astype(BF16)
            nm_s[d, chunks] = nm[sl]
            p_s[d, rows_blk, :] = pm[sl].reshape(g * c, dk).astype(BF16)
            r_s[d, rows_blk, :] = rm[sl].reshape(g * c, dk)
            eg_s[d, chunks] = jnp.broadcast_to(eglast[sl], (g, V7X_SUBLANES, dk))
        return carry

    lax.fori_loop(0, n_chunks // g, pass1, 0)

    st_s[...] = s0_ref[0, 0]

    def pass2(n, carry):
        for d, ch in ((0, n), (1, n_chunks - 1 - n)):
            s = st_s[d]
            sb = s.astype(BF16)
            sall_s[d, ch] = sb
            st_s[d] = s * eg_s[d, ch][0:1, :] - jnp.dot(km_s[d, ch], sb, preferred_element_type=F32) + nm_s[d, ch]
        return carry

    lax.fori_loop(0, n_chunks, pass2, 0)
    sfin_ref[0, 0] = st_s[...]

    def pass3(gi, carry):
        rows_blk = pl.ds(pl.multiple_of(gi * (g * c), g * c), g * c)
        chunks = pl.ds(gi * g, g)
        o = None
        for d in range(2):
            term = _mm(p_s[d, rows_blk, :].reshape(g, c, dk), sall_s[d, chunks]) + r_s[d, rows_blk, :].reshape(g, c, dk)
            o = term if o is None else o + term
        o = o.reshape(g * c, dk)
        o = o * lax.rsqrt(jnp.mean(o * o, axis=-1, keepdims=True) + RMS_EPS) * an_ref[...]
        og = og_ref[0, rows_blk, :]
        o_ref[0, rows_blk, :] = (o * (og * jax.nn.sigmoid(og))).astype(o_ref.dtype)
        return carry

    lax.fori_loop(0, n_chunks // g, pass3, 0)


def _delta_call(act, z_og, gates, s0, a_norm, og_col=0):
    b, l, ch = act.shape
    heads = ch // 3 // A_HEAD_DIM
    dk = A_HEAD_DIM
    c = A_CHUNK
    n_chunks = l // c
    g4 = jnp.stack([gates[..., i * heads:(i + 1) * heads] for i in range(4)], axis=-1)
    col = jnp.pad(jnp.transpose(g4, (0, 2, 1, 3)), ((0, 0), (0, 0), (0, 0), (0, 4)))
    row = jnp.transpose(g4[..., 2:4].reshape(b, n_chunks, c, heads, 2), (0, 3, 1, 4, 2))
    row = jnp.pad(row, ((0, 0), (0, 0), (0, 0), (0, 6), (0, 0)))
    tok = lambda off: pl.BlockSpec((1, l, dk), functools.partial(lambda i, h, off: (i, 0, h + off), off=off))
    return pl.pallas_call(
        functools.partial(_delta_kernel, l=l),
        grid=(b, heads),
        in_specs=[tok(0), tok(heads), tok(2 * heads),
                  pl.BlockSpec((1, 1, l, 8), lambda i, h: (i, h, 0, 0)),
                  pl.BlockSpec((1, 1, n_chunks, 8, c), lambda i, h: (i, h, 0, 0, 0)),
                  pl.BlockSpec((1, 1, 2, dk, dk), lambda i, h: (i, h, 0, 0, 0)),
                  tok(og_col // dk),
                  pl.BlockSpec((1, dk), lambda i, h: (0, 0))],
        out_specs=(tok(0), pl.BlockSpec((1, 1, 2, dk, dk), lambda i, h: (i, h, 0, 0, 0))),
        out_shape=(jax.ShapeDtypeStruct((b, l, heads * dk), BF16),
                   jax.ShapeDtypeStruct((b, heads, 2, dk, dk), F32)),
        scratch_shapes=[pltpu.VMEM((2, n_chunks, dk, dk), BF16), pltpu.VMEM((2, n_chunks, dk, dk), F32),
                        pltpu.VMEM((2, l, dk), BF16), pltpu.VMEM((2, l, dk), F32),
                        pltpu.VMEM((2, n_chunks, V7X_SUBLANES, dk), F32),
                        pltpu.VMEM((2, n_chunks, dk, dk), BF16), pltpu.VMEM((2, dk, dk), F32)],
        compiler_params=_cparams("parallel", "parallel"),
        name="delta_rule",
    )(act, act, act, col, row, s0, z_og, a_norm.reshape(1, dk).astype(F32))


def _gmlp_kernel(zu_ref, zv_ref, ws_ref, bs_ref, o_ref):
    groups = ws_ref.shape[0]
    gd = zu_ref.shape[1] // groups
    u = jax.nn.gelu(zu_ref[...])
    v = jax.nn.gelu(zv_ref[...])
    mu = jnp.mean(v, axis=-1, keepdims=True)
    var = jnp.mean(jnp.square(v - mu), axis=-1, keepdims=True)
    vn = ((v - mu) * lax.rsqrt(var + RMS_EPS)).astype(BF16)
    mixed = [jnp.dot(ws_ref[g], vn[:, g * gd:(g + 1) * gd], preferred_element_type=F32) + bs_ref[:, g:g + 1]
             for g in range(groups)]
    o_ref[...] = (u * jnp.concatenate(mixed, axis=1)).astype(o_ref.dtype)


def _gmlp_call(z_b, w_s, b_s, bw, col0=0):
    n_tok = z_b.shape[0]
    groups, p, _ = w_s.shape
    c0 = col0 // bw
    return pl.pallas_call(
        _gmlp_kernel,
        grid=(n_tok // p,),
        in_specs=[pl.BlockSpec((p, bw), lambda i: (i, c0)),
                  pl.BlockSpec((p, bw), lambda i: (i, c0 + 1)),
                  pl.BlockSpec((groups, p, p), lambda i: (0, 0, 0)),
                  pl.BlockSpec((p, groups), lambda i: (0, 0))],
        out_specs=pl.BlockSpec((p, bw), lambda i: (i, 0)),
        out_shape=jax.ShapeDtypeStruct((n_tok, bw), BF16),
        compiler_params=_cparams("parallel"),
        name="gmlp",
    )(z_b, z_b, w_s, b_s.T.astype(F32))


_FEAT_PAD = 64


def _filter_feats(l):
    pos = np.arange(l, dtype=np.float32)
    t = np.linspace(0.0, 1.0, l, dtype=np.float32)[:, None]
    bands = np.linspace(1e-4, C_POS_BANDS - 1, C_POS_BANDS, dtype=np.float32)
    ang = np.float32(2.0 * math.pi / l) * pos[:, None] * bands
    feats = np.concatenate([t, np.cos(ang), -np.sin(ang)], axis=-1).astype(np.float32)
    return np.pad(feats, ((0, 0), (0, _FEAT_PAD - feats.shape[1])))


def _filter_kernel(feat_ref, w1_ref, b1_ref, f1_ref, w2_ref, b2_ref, f2_ref, w3_ref, b3_ref, rate_ref, o_ref):
    feats = feat_ref[...]
    hid = jnp.sin(f1_ref[...] * (_dot3(feats, w1_ref[...]) + b1_ref[...]))
    hid = jnp.sin(f2_ref[...] * (_dot3(hid, w2_ref[...]) + b2_ref[...]))
    filt = _dot3(hid, w3_ref[...]) + b3_ref[...]
    filt = filt * jnp.exp(-feats[:, 0:1] * rate_ref[...])
    o_ref[...] = filt * lax.rsqrt(jnp.sum(filt * filt, axis=0, keepdims=True) + RMS_EPS)


def _filter_call(l, fw1, fb1, freq1, fw2, fb2, freq2, fw3, fb3, tc=256):
    hidden = fw1.shape[1]
    two_c = fw3.shape[1]
    cw = two_c // 2
    feats = jnp.asarray(_filter_feats(l))
    w1 = jnp.pad(fw1.astype(F32), ((0, _FEAT_PAD - fw1.shape[0]), (0, 0)))
    rate = np.abs(np.linspace(C_MIN_DECAY, C_MAX_DECAY, cw, dtype=np.float32))
    rate = jnp.asarray(np.concatenate([rate, rate]).reshape(1, two_c))
    row = lambda v: v.reshape(1, -1).astype(F32)
    full = lambda shape: pl.BlockSpec(shape, lambda j: (0, 0))
    return pl.pallas_call(
        _filter_kernel,
        grid=(two_c // tc,),
        in_specs=[full((l, _FEAT_PAD)), full((_FEAT_PAD, hidden)), full((1, hidden)), full((1, hidden)),
                  full((hidden, hidden)), full((1, hidden)), full((1, hidden)),
                  pl.BlockSpec((hidden, tc), lambda j: (0, j)), pl.BlockSpec((1, tc), lambda j: (0, j)),
                  pl.BlockSpec((1, tc), lambda j: (0, j))],
        out_specs=pl.BlockSpec((l, tc), lambda j: (0, j)),
        out_shape=jax.ShapeDtypeStruct((l, two_c), F32),
        compiler_params=_cparams("parallel"),
        name="hyena_filter",
    )(feats, w1, row(fb1), row(freq1), fw2.astype(F32), row(fb2), row(freq2), fw3.astype(F32), row(fb3), rate)


def _dft_sizes(l):
    n1 = 64 if l >= 4096 else 32
    return n1, (2 * l) // n1


def _hi_lo(x):
    x = jnp.asarray(x, F32)
    hi = x.astype(BF16)
    return hi, (x - hi.astype(F32)).astype(BF16)


def _dft_consts(l):
    n1, n2 = _dft_sizes(l)
    n = n1 * n2
    half = n1 // 2
    k1 = np.arange(n1)[None, :, None]
    m1 = np.arange(half)[None, None, :]
    m2 = np.arange(n2)[:, None, None]
    f1 = np.exp(-2j * np.pi * (m1 * k1 / n1 + m2 * k1 / n))
    fwd = np.concatenate([f1.real, f1.imag], axis=1)
    fh, fl = _hi_lo(fwd)
    fwd1 = jnp.concatenate([fh, fh, fl], axis=2)
    rt = np.transpose(f1.real, (0, 2, 1)) / n
    it = np.transpose(f1.imag, (0, 2, 1)) / n
    (rh, rl), (ih, il) = _hi_lo(rt), _hi_lo(it)
    inv1 = jnp.concatenate([rh, ih, rh, ih, rl, il], axis=2)
    kk = np.arange(n2)
    f2 = np.exp(-2j * np.pi * np.outer(kk, kk) / n2)
    m = np.block([[f2.real, -f2.imag], [f2.imag, f2.real]])
    mi = np.block([[f2.real, f2.imag], [-f2.imag, f2.real]])
    (mh, ml), (mih, mil) = _hi_lo(m), _hi_lo(mi)
    return fwd1, jnp.concatenate([mh, mh, ml], axis=1), jnp.concatenate([mih, mih, mil], axis=1), inv1


_DFT_UNROLL = 4


def _stack3(x):
    hi, lo = _split_bf16(x)
    return jnp.concatenate([hi, lo, hi], axis=0)


def _dft_pitch(n2):
    return n2 + V7X_SUBLANES


def _dft_level1(x_ref, fwd1_ref, ar_s, ai_s, n1, n2):
    half = n1 // 2
    pitch = _dft_pitch(n2)

    def body(j, carry):
        xs = x_ref[pl.ds(j, half, stride=n2), :]
        a = jnp.dot(fwd1_ref[j], _stack3(xs), preferred_element_type=F32)
        ar_s[pl.ds(j, n1, stride=pitch), :] = a[:n1]
        ai_s[pl.ds(j, n1, stride=pitch), :] = a[n1:]
        return carry

    lax.fori_loop(0, n2, body, 0, unroll=_DFT_UNROLL)


def _block_rows(k1, n2):
    return pl.ds(pl.multiple_of(k1 * _dft_pitch(n2), V7X_SUBLANES), n2)


def _pair_rows(pair, n2):
    return pl.ds(pl.multiple_of(pair * (2 * n2), 2 * n2), 2 * n2)


def _side_by_side(x, n2):
    return jnp.concatenate([x[:n2], x[n2:]], axis=1)


def _dft_level2(ar_s, ai_s, m3_ref, pair, n2):
    lo, hi = _block_rows(2 * pair, n2), _block_rows(2 * pair + 1, n2)
    blk = jnp.concatenate([jnp.concatenate([ar_s[lo, :], ar_s[hi, :]], axis=1),
                           jnp.concatenate([ai_s[lo, :], ai_s[hi, :]], axis=1)], axis=0)
    x = jnp.dot(m3_ref[...], _stack3(blk), preferred_element_type=F32)
    return x[:n2], x[n2:]


def _spectrum_kernel(hf_ref, hb_ref, fwd1_ref, m3_ref, o_ref, ar_s, ai_s, *, n1, n2):
    tc = o_ref.shape[2]
    for which, h_ref in enumerate((hf_ref, hb_ref)):
        _dft_level1(h_ref, fwd1_ref, ar_s, ai_s, n1, n2)

        def body(pair, carry):
            xr, xi = _dft_level2(ar_s, ai_s, m3_ref, pair, n2)
            for side in range(2):
                rows = pl.ds(pl.multiple_of((2 * pair + side) * n2, n2), n2)
                lanes = slice(side * tc, (side + 1) * tc)
                if which == 0:
                    o_ref[0, rows, :] = xr[:, lanes]
                    o_ref[1, rows, :] = xi[:, lanes]
                else:
                    o_ref[0, rows, :] += xr[:, lanes]
                    o_ref[1, rows, :] -= xi[:, lanes]
            return carry

        lax.fori_loop(0, n1 // 2, body, 0, unroll=_DFT_UNROLL // 2)


def _spectrum_call(filt, consts, tc=128):
    l, two_c = filt.shape
    cw = two_c // 2
    nb = cw // tc
    n1, n2 = _dft_sizes(l)
    fwd1, m3, _, _ = consts
    return pl.pallas_call(
        functools.partial(_spectrum_kernel, n1=n1, n2=n2),
        grid=(nb,),
        in_specs=[pl.BlockSpec((l, tc), lambda j: (0, j)),
                  pl.BlockSpec((l, tc), lambda j: (0, j + nb)),
                  pl.BlockSpec(fwd1.shape, lambda j: (0, 0, 0)),
                  pl.BlockSpec(m3.shape, lambda j: (0, 0))],
        out_specs=pl.BlockSpec((2, 2 * l, tc), lambda j: (0, 0, j)),
        out_shape=jax.ShapeDtypeStruct((2, 2 * l, cw), F32),
        scratch_shapes=[pltpu.VMEM((n1 * _dft_pitch(n2), tc), F32), pltpu.VMEM((n1 * _dft_pitch(n2), tc), F32)],
        compiler_params=_cparams("parallel"),
        name="hyena_spectrum",
    )(filt, filt, fwd1, m3)


def _longconv_kernel(u_ref, x0_ref, h_ref, skip_ref, fwd1_ref, m3_ref, mi3_ref, inv1_ref, o_ref,
                     ar_s, ai_s, y_s, *, l, n1, n2):
    half = n1 // 2
    _dft_level1(u_ref.at[0], fwd1_ref, ar_s, ai_s, n1, n2)

    tc = o_ref.shape[2]

    pitch = _dft_pitch(n2)

    def per_k1_pair(pair, carry):
        xr, xi = _dft_level2(ar_s, ai_s, m3_ref, pair, n2)
        rows = _pair_rows(pair, n2)
        hr = _side_by_side(h_ref[0, rows, :], n2)
        hi = _side_by_side(h_ref[1, rows, :], n2)
        y = jnp.concatenate([xr * hr - xi * hi, xr * hi + xi * hr], axis=0)
        b = jnp.dot(mi3_ref[...], _stack3(y), preferred_element_type=F32)
        for side in range(2):
            blk = _block_rows(2 * pair + side, n2)
            ar_s[blk, :] = b[:n2, side * tc:(side + 1) * tc]
            ai_s[blk, :] = b[n2:, side * tc:(side + 1) * tc]
        return carry

    lax.fori_loop(0, n1 // 2, per_k1_pair, 0, unroll=_DFT_UNROLL // 2)

    def per_n2(j, carry):
        br = ar_s[pl.ds(j, n1, stride=pitch), :]
        bi = ai_s[pl.ds(j, n1, stride=pitch), :]
        (brh, brl), (bih, bil) = _split_bf16(br), _split_bf16(bi)
        rhs = jnp.concatenate([brh, bih, brl, bil, brh, bih], axis=0)
        y_s[pl.ds(j, half, stride=pitch), :] = jnp.dot(inv1_ref[j], rhs, preferred_element_type=F32)
        return carry

    lax.fori_loop(0, n2, per_n2, 0, unroll=_DFT_UNROLL)

    def finish(i, carry):
        rows = pl.ds(pl.multiple_of(i * n2, n2), n2)
        u = u_ref[0, rows, :]
        y = y_s[_block_rows(i, n2), :]
        o_ref[0, rows, :] = (x0_ref[0, rows, :] * (y + skip_ref[...] * u)).astype(o_ref.dtype)
        return carry

    lax.fori_loop(0, half, finish, 0, unroll=_DFT_UNROLL)


def _longconv_call(u, x0, spec, skip, consts, tc=128):
    b, l, cw = u.shape
    n1, n2 = _dft_sizes(l)
    fwd1, m3, mi3, inv1 = consts
    tok = pl.BlockSpec((1, l, tc), lambda j, i: (i, 0, j))
    const = lambda a: pl.BlockSpec(a.shape, lambda j, i: (0,) * a.ndim)
    return pl.pallas_call(
        functools.partial(_longconv_kernel, l=l, n1=n1, n2=n2),
        grid=(cw // tc, b),
        in_specs=[tok, tok,
                  pl.BlockSpec((2, 2 * l, tc), lambda j, i: (0, 0, j)),
                  pl.BlockSpec((1, tc), lambda j, i: (0, j)),
                  const(fwd1), const(m3), const(mi3), const(inv1)],
        out_specs=tok,
        out_shape=jax.ShapeDtypeStruct((b, l, cw), BF16),
        scratch_shapes=[pltpu.VMEM((n1 * _dft_pitch(n2), tc), F32), pltpu.VMEM((n1 * _dft_pitch(n2), tc), F32),
                        pltpu.VMEM((n1 // 2 * _dft_pitch(n2), tc), F32)],
        compiler_params=_cparams("parallel", "parallel"),
        name="hyena_longconv",
    )(u, x0, spec, skip.reshape(1, cw).astype(F32), fwd1, m3, mi3, inv1)


def _layer_weights(i, p):
    aw, bw, cw = p["w_br_a"].shape[1], p["w_br_b"].shape[1], p["w_br_c"].shape[1]
    heads = p["a_log"].shape[2]
    off_ba = 2 * aw
    off_q = off_ba + 4 * heads
    cast = lambda w: w.astype(BF16)
    col_og = 3 * aw
    col_b = col_og + aw
    col_c = col_b + 2 * bw
    col_gate = col_c + 3 * cw
    return {
        "w_main": _w_in_prep_call(p["w_in"], i, off_ba, off_q),
        "w_ba": cast(jnp.pad(p["w_in"][i][:, off_ba:off_q], ((0, 0), (0, V7X_LANES - 4 * heads)))),
        "col_og": col_og, "col_b": col_b, "col_c": col_c, "col_gate": col_gate, "bw": bw,
        "w_br_a": cast(p["w_br_a"][i]), "w_br_b": cast(p["w_br_b"][i]), "w_br_c": cast(p["w_br_c"][i]),
        "w_out": cast(p["w_out"][i]),
        "a_conv": p["a_conv"][i].reshape(9, -1), "c_conv": p["c_conv"][i].reshape(9, -1),
        "a_log": p["a_log"][i], "a_dt_bias": p["a_dt_bias"][i], "a_norm": p["a_norm"][i],
        "b_ws": cast(p["b_ws"][i]), "b_bs": p["b_bs"][i], "c_skip": p["c_skip"][i],
        "filter": tuple(p[k][i] for k in ("c_fw1", "c_fb1", "c_freq1", "c_fw2", "c_fb2", "c_freq2", "c_fw3", "c_fb3")),
        "p_wq": cast(p["p_wq"][i]), "p_keys": cast(p["p_keys"][i]), "p_u": cast(p["p_u"][i]), "p_v": cast(p["p_v"][i]),
    }


def _delta_branch(xf, z, b, l, rows, cols, s0, lw):
    z3 = z.reshape(b, l, -1)
    z_ba = _matmul(xf, lw["w_ba"]).reshape(b, l, -1)
    act = _conv_a_call(z3, lw["a_conv"], rows, cols)
    gates = _gate_prep_call(z_ba, lw["a_log"], lw["a_dt_bias"])
    o_a, s_fin = _delta_call(act, z3, gates, s0, lw["a_norm"], og_col=lw["col_og"])
    return o_a.reshape(b * l, -1), s_fin


def _token_mixer(xn, rows, cols, s0, lw):
    b, l, d = xn.shape
    xf = xn.reshape(b * l, d)
    z = _matmul(xf, lw["w_main"])
    o_a, s_fin = _delta_branch(xf, z, b, l, rows, cols, s0, lw)
    o_b = _gmlp_call(z, lw["b_ws"], lw["b_bs"], lw["bw"], col0=lw["col_b"])
    x0, u = _conv_c_call(z.reshape(b, l, -1), lw["c_conv"], rows, cols, col0=lw["col_c"])
    consts = _dft_consts(l)
    spec = _spectrum_call(_filter_call(l, *lw["filter"]), consts)
    o_c = _longconv_call(u, x0, spec, lw["c_skip"], consts).reshape(b * l, -1)
    merged = _merge_call(o_a, o_b, o_c, lw["w_br_a"], lw["w_br_b"], lw["w_br_c"], z, lw["col_gate"])
    return _matmul(merged, lw["w_out"]).reshape(b, l, d), s_fin


def _peer_layer(xn, lw):
    b, l, d = xn.shape
    return _peer(xn.reshape(b * l, d), lw["p_wq"], lw["p_keys"], lw["p_u"], lw["p_v"]).reshape(b, l, d)


def kernel(x, c, ctx, c_ctx, ada_w, ada_b, w_in, a_conv, a_log, a_dt_bias, a_norm, b_ws, b_bs, c_conv, c_fw1, c_fb1,
           c_freq1, c_fw2, c_fb2, c_freq2, c_fw3, c_fb3, c_skip, w_br_a, w_br_b, w_br_c, w_out, p_wq, p_keys, p_u,
           p_v, final_norm):
    params = dict(w_in=w_in, a_conv=a_conv, a_log=a_log, a_dt_bias=a_dt_bias, a_norm=a_norm, b_ws=b_ws, b_bs=b_bs,
                  c_conv=c_conv, c_fw1=c_fw1, c_fb1=c_fb1, c_freq1=c_freq1, c_fw2=c_fw2, c_fb2=c_fb2, c_freq2=c_freq2,
                  c_fw3=c_fw3, c_fb3=c_fb3, c_skip=c_skip, w_br_a=w_br_a, w_br_b=w_br_b, w_br_c=w_br_c, w_out=w_out,
                  p_wq=p_wq, p_keys=p_keys, p_u=p_u, p_v=p_v)
    b, l, d = x.shape
    depth = w_in.shape[0]
    rows = l // GRID_W
    l_ctx = ctx.shape[1]
    heads = a_log.shape[2]
    s_zero = jnp.zeros((b, heads, 2, A_HEAD_DIM, A_HEAD_DIM), F32)
    cvec = jnp.concatenate([c, c_ctx[None, :], jnp.zeros((V7X_SUBLANES - b - 1, d), F32)], axis=0)

    h, h_pending = x, None
    hc, hc_pending = ctx, None
    for i in range(depth):
        lw = _layer_weights(i, params)
        mod_all = _mod_call(cvec, ada_w[i], ada_b[i])
        mod = mod_all[:b].reshape(b, N_MOD, 1, d)
        mod_c = mod_all[b].reshape(N_MOD, 1, 1, d)

        def norm(stream, pending, scale, shift):
            if pending is None:
                return stream, _norm_mod_call(stream, scale, shift)[1]
            return _norm_mod_call(stream, scale, shift, delta=pending[0], gate=pending[1], emit_h=True)

        hc, xnc = norm(hc, hc_pending, mod_c[1], mod_c[0])
        if i == depth - 1:
            xcf = xnc.reshape(b * l_ctx, d)
            z_c = _matmul(xcf, lw["w_main"], ncols=lw["col_b"])
            _, s_ctx = _delta_branch(xcf, z_c, b, l_ctx, 1, l_ctx, s_zero, lw)
        else:
            out_c, s_ctx = _token_mixer(xnc, 1, l_ctx, s_zero, lw)
            hc, xnc2 = norm(hc, (out_c, mod_c[2]), mod_c[4], mod_c[3])
            hc_pending = (_peer_layer(xnc2, lw), mod_c[5])
        h, xn = norm(h, h_pending, mod[:, 1], mod[:, 0])
        out, _ = _token_mixer(xn, rows, GRID_W, s_ctx, lw)
        h, xn2 = norm(h, (out, mod[:, 2]), mod[:, 4], mod[:, 3])
        h_pending = (_peer_layer(xn2, lw), mod[:, 5])
    return _norm_mod_call(h, final_norm.reshape(1, 1, d), delta=h_pending[0], gate=h_pending[1], out_dtype=F32)[1]
```

```python
import functools
import math

import numpy as np
import jax
import jax.numpy as jnp
from jax import lax
from jax.experimental import pallas as pl
from jax.experimental.pallas import tpu as pltpu

F32 = jnp.float32
BF16 = jnp.bfloat16

V7X_LANES = 128
V7X_SUBLANES = 8
V7X_VMEM_LIMIT_BYTES = 56 * 1024 * 1024

GRID_W = 64
RMS_EPS = 1e-6
N_MOD = 6
A_HEAD_DIM = 128
A_CHUNK = 64
B_CHUNK = 128
C_POS_BANDS = 16
C_MIN_DECAY = math.log(1e-2) / 1.5
C_MAX_DECAY = math.log(1e-2) / 0.3
N_BRANCH = 3
P_KEYS = 128
P_TOPK = 16


def _cparams(*sem):
    return pltpu.CompilerParams(dimension_semantics=sem, vmem_limit_bytes=V7X_VMEM_LIMIT_BYTES)


def _mod_kernel(c_ref, w_ref, b_ref, o_ref):
    c = c_ref[...]
    a = (c * jax.nn.sigmoid(c)).astype(BF16)
    o_ref[...] = jnp.dot(a, w_ref[...].astype(BF16), preferred_element_type=F32) + b_ref[...]


def _mod_call(cvec, ada_w, ada_b, layer):
    rows, d = cvec.shape
    n = ada_w.shape[-1]
    tn = 1024
    return pl.pallas_call(
        _mod_kernel,
        grid=(n // tn,),
        in_specs=[pl.BlockSpec((rows, d), lambda j: (0, 0)),
                  _layer_block((d, tn), lambda j: (0, j), ada_w, layer),
                  pl.BlockSpec((1, tn), lambda j: (0, j))],
        out_specs=pl.BlockSpec((rows, tn), lambda j: (0, j)),
        out_shape=jax.ShapeDtypeStruct((rows, n), F32),
        compiler_params=_cparams("parallel"),
        name="ada_mod",
    )(cvec, ada_w, ada_b.reshape(1, n))


def _norm_mod_kernel(*refs, with_delta, with_shift, emit_h):
    refs = list(refs)
    h = refs.pop(0)[0]
    if with_delta:
        d_ref, g_ref = refs.pop(0), refs.pop(0)
        h = h + g_ref[0] * d_ref[0]
    sc_ref = refs.pop(0)
    xn = h * lax.rsqrt(jnp.mean(h * h, axis=-1, keepdims=True) + RMS_EPS)
    if with_shift:
        xn = xn * (1.0 + sc_ref[0]) + refs.pop(0)[0]
    else:
        xn = xn * sc_ref[0]
    if emit_h:
        refs.pop(0)[0] = h
    refs.pop(0)[0] = xn.astype(refs[0].dtype)


def _norm_mod_call(h, scale, shift=None, delta=None, gate=None, emit_h=False, out_dtype=BF16, tl=256):
    b, l, d = h.shape
    tl = min(tl, l)
    tok = pl.BlockSpec((1, tl, d), lambda i, j: (i, j, 0))

    def vec(v):
        if v.shape[0] == 1:
            return pl.BlockSpec((1, 1, d), lambda i, j: (0, 0, 0))
        return pl.BlockSpec((1, 1, d), lambda i, j: (i, 0, 0))

    args, in_specs = [h], [tok]
    if delta is not None:
        args += [delta, gate]
        in_specs += [tok, vec(gate)]
    args.append(scale)
    in_specs.append(vec(scale))
    if shift is not None:
        args.append(shift)
        in_specs.append(vec(shift))
    out_shape, out_specs = [], []
    if emit_h:
        out_shape.append(jax.ShapeDtypeStruct((b, l, d), F32))
        out_specs.append(tok)
    out_shape.append(jax.ShapeDtypeStruct((b, l, d), out_dtype))
    out_specs.append(tok)
    res = pl.pallas_call(
        functools.partial(_norm_mod_kernel, with_delta=delta is not None, with_shift=shift is not None, emit_h=emit_h),
        grid=(b, l // tl),
        in_specs=in_specs, out_specs=tuple(out_specs), out_shape=tuple(out_shape),
        compiler_params=_cparams("parallel", "parallel"),
        name="norm_mod",
    )(*args)
    return (res[0], res[1]) if emit_h else (None, res[0])


def _mm_kernel(a_ref, w_ref, o_ref):
    o_ref[...] = jnp.dot(a_ref[...], w_ref[...], preferred_element_type=F32).astype(o_ref.dtype)


def _pick_tile(n, pref):
    t = min(pref, n)
    while n % t:
        t //= 2
    return t


def _layer_block(block, index_map, w, layer):
    if layer is None:
        return pl.BlockSpec(block, index_map)
    assert w.ndim == len(block) + 1
    return pl.BlockSpec((None,) + block, lambda *g: (layer,) + tuple(index_map(*g)))


def _matmul(a, w, out_dtype=F32, tm=512, tn=1024, ncols=None, layer=None):
    m, k = a.shape
    n = w.shape[-1] if ncols is None else ncols
    tm = _pick_tile(m, tm)
    tn = _pick_tile(n, tn)
    return pl.pallas_call(
        _mm_kernel,
        grid=(n // tn, m // tm),
        in_specs=[pl.BlockSpec((tm, k), lambda j, i: (i, 0)),
                  _layer_block((k, tn), lambda j, i: (0, j), w, layer)],
        out_specs=pl.BlockSpec((tm, tn), lambda j, i: (i, j)),
        out_shape=jax.ShapeDtypeStruct((m, n), out_dtype),
        compiler_params=_cparams("parallel", "parallel"),
        name="matmul",
    )(a, w)


def _merge_kernel(oa_ref, ob_ref, oc_ref, wa_ref, wb_ref, wc_ref, ga_ref, gb_ref, gc_ref, o_ref):
    acc = jax.nn.sigmoid(ga_ref[...]) * jnp.dot(oa_ref[...], wa_ref[...], preferred_element_type=F32)
    acc += jax.nn.sigmoid(gb_ref[...]) * jnp.dot(ob_ref[...], wb_ref[...], preferred_element_type=F32)
    acc += jax.nn.sigmoid(gc_ref[...]) * jnp.dot(oc_ref[...], wc_ref[...], preferred_element_type=F32)
    o_ref[...] = acc.astype(o_ref.dtype)


def _merge_call(o_a, o_b, o_c, w_a, w_b, w_c, z, gate_col, tm=512, tn=512):
    n_tok = o_a.shape[0]
    d = w_a.shape[1]
    tm = _pick_tile(n_tok, tm)
    tn = _pick_tile(math.gcd(d, gate_col) if gate_col else d, tn)
    nb = d // tn
    g0 = gate_col // tn

    def act(o):
        return pl.BlockSpec((tm, o.shape[1]), lambda j, i: (i, 0))

    def wgt(w):
        return pl.BlockSpec((w.shape[0], tn), lambda j, i: (0, j))

    def gate(br):
        return pl.BlockSpec((tm, tn), lambda j, i: (i, g0 + j + br * nb))

    return pl.pallas_call(
        _merge_kernel,
        grid=(nb, n_tok // tm),
        in_specs=[act(o_a), act(o_b), act(o_c), wgt(w_a), wgt(w_b), wgt(w_c), gate(0), gate(1), gate(2)],
        out_specs=pl.BlockSpec((tm, tn), lambda j, i: (i, j)),
        out_shape=jax.ShapeDtypeStruct((n_tok, d), BF16),
        compiler_params=_cparams("parallel", "parallel"),
        name="branch_merge",
    )(o_a, o_b, o_c, w_a, w_b, w_c, z, z, z)


def _topk_rows(s, payload, k):
    n, t = s.shape
    iota = lax.broadcasted_iota(jnp.int32, (n, t), 0).astype(F32)
    riota = lax.broadcasted_iota(jnp.int32, (k, t), 0)

    def body(r, carry):
        s, tv, tp = carry
        m = jnp.max(s, axis=0, keepdims=True)
        idx = jnp.min(jnp.where(s == m, iota, float(n)), axis=0, keepdims=True)
        sel = iota == idx
        p = idx if payload is None else jnp.max(jnp.where(sel, payload, -1.0), axis=0, keepdims=True)
        s = jnp.where(sel, -jnp.inf, s)
        tv = jnp.where(riota == r, m, tv)
        tp = jnp.where(riota == r, p, tp)
        return s, tv, tp

    _, tv, tp = lax.fori_loop(0, k, body, (s, jnp.zeros((k, t), F32), jnp.zeros((k, t), F32)))
    return tv, tp


def _peer_route_kernel(x_ref, wq_ref, keys_ref, e_ref, g_ref):
    t = x_ref.shape[0]
    q = jnp.dot(x_ref[...], wq_ref[...], preferred_element_type=F32).astype(BF16)
    half = q.shape[1] // 2
    tops = []
    for p in range(2):
        s_t = lax.dot_general(keys_ref[0, p], q[:, p * half:(p + 1) * half],
                              (((1,), (1,)), ((), ())), preferred_element_type=F32)
        tops.append(_topk_rows(s_t, None, P_TOPK))
    (s0, i0), (s1, i1) = tops
    cand, cid = [], []
    for a in range(P_TOPK):
        n_b = P_TOPK // (a + 1)
        rows = -(-n_b // V7X_SUBLANES) * V7X_SUBLANES
        keep = lax.broadcasted_iota(jnp.int32, (rows, t), 0) < n_b
        cand.append(jnp.where(keep, jnp.broadcast_to(s0[a:a + 1], (rows, t)) + s1[:rows], -jnp.inf))
        cid.append(jnp.broadcast_to(i0[a:a + 1], (rows, t)) * float(P_KEYS) + i1[:rows])
    best_s, best_e = _topk_rows(jnp.concatenate(cand, axis=0), jnp.concatenate(cid, axis=0), P_TOPK)
    ex = jnp.exp(best_s - jnp.max(best_s, axis=0, keepdims=True))
    g_ref[0] = ex / jnp.sum(ex, axis=0, keepdims=True)
    e_ref[0] = best_e.astype(jnp.int32)


def _peer_route_call(xn, wq, keys, tt=512):
    n_tok, d = xn.shape
    heads = keys.shape[0]
    qd = wq.shape[1] // heads
    tt = _pick_tile(n_tok, tt)
    return pl.pallas_call(
        _peer_route_kernel,
        grid=(n_tok // tt, heads),
        in_specs=[pl.BlockSpec((tt, d), lambda i, h: (i, 0)),
                  pl.BlockSpec((d, qd), lambda i, h: (0, h)),
                  pl.BlockSpec((1,) + keys.shape[1:], lambda i, h: (h, 0, 0, 0))],
        out_specs=(pl.BlockSpec((1, P_TOPK, tt), lambda i, h: (h, 0, i)),
                   pl.BlockSpec((1, P_TOPK, tt), lambda i, h: (h, 0, i))),
        out_shape=(jax.ShapeDtypeStruct((heads, P_TOPK, n_tok), jnp.int32),
                   jax.ShapeDtypeStruct((heads, P_TOPK, n_tok), F32)),
        compiler_params=_cparams("parallel", "parallel"),
        name="peer_route",
    )(xn, wq, keys)


_PEER_GROUP = 16
_PEER_STAGE_PITCH = P_KEYS + V7X_SUBLANES


def _peer_dense_kernel(x_ref, e_ref, g_ref, u_ref, v_ref, o_ref, gmat_ref, stage_ref, w_ref, *, eb, n_blocks):
    t = x_ref.shape[0]
    j = pl.program_id(1)
    cur = j % 2

    @pl.when(j == 0)
    def _build_gate_matrix():
        o_ref[...] = jnp.zeros_like(o_ref)
        w_ref[1] = jnp.zeros(w_ref.shape[1:], w_ref.dtype)
        sub = lax.broadcasted_iota(jnp.int32, (P_KEYS, e_ref.shape[1]), 0)

        def per_group(grp, carry):
            t0 = pl.multiple_of(grp * _PEER_GROUP, _PEER_GROUP)
            for s in range(_PEER_GROUP):
                e = e_ref[pl.ds(t0 + s, 1), :]
                g = g_ref[pl.ds(t0 + s, 1), :]
                hit_a = sub == (e >> 7)
                hit_b = sub == (e & (P_KEYS - 1))
                a_mat = jnp.where(hit_a, g, 0.0).astype(BF16)
                b_mat = jnp.where(hit_b, 1.0, 0.0).astype(BF16)
                stage_ref[s * _PEER_STAGE_PITCH:s * _PEER_STAGE_PITCH + P_KEYS, :] = _mm_nt(a_mat, b_mat)
            for i1 in range(P_KEYS):
                gmat_ref[pl.ds(t0, _PEER_GROUP), i1 * P_KEYS:(i1 + 1) * P_KEYS] = (
                    stage_ref[pl.ds(i1, _PEER_GROUP, stride=_PEER_STAGE_PITCH), :].astype(BF16))
            return carry

        lax.fori_loop(0, t // _PEER_GROUP, per_group, 0)

    o_ref[...] += jnp.dot(w_ref[1 - cur], v_ref[...], preferred_element_type=F32)
    blk = jnp.minimum(j, n_blocks - 1)
    gsel = gmat_ref[:, pl.ds(pl.multiple_of(blk * eb, eb), eb)]
    w_ref[cur] = (gsel.astype(F32) * jax.nn.gelu(_mm_nt(x_ref[...], u_ref[...]))).astype(BF16)


def _peer_dense_call(xn, e_nat, g_nat, u_tab, v_tab, tt=512, eb=256, layer=None):
    n_tok, d = xn.shape
    n_exp = u_tab.shape[-2]
    r = e_nat.shape[1]
    tt = _pick_tile(n_tok, tt)
    n_blocks = n_exp // eb
    return pl.pallas_call(
        functools.partial(_peer_dense_kernel, eb=eb, n_blocks=n_blocks),
        grid=(n_tok // tt, n_blocks + 1),
        in_specs=[pl.BlockSpec((tt, d), lambda i, j: (i, 0)),
                  pl.BlockSpec((tt, r), lambda i, j: (i, 0)),
                  pl.BlockSpec((tt, r), lambda i, j: (i, 0)),
                  _layer_block((eb, d), lambda i, j: (jnp.minimum(j, n_blocks - 1), 0), u_tab, layer),
                  _layer_block((eb, d), lambda i, j: (jnp.maximum(j - 1, 0), 0), v_tab, layer)],
        out_specs=pl.BlockSpec((tt, d), lambda i, j: (i, 0)),
        out_shape=jax.ShapeDtypeStruct((n_tok, d), F32),
        scratch_shapes=[pltpu.VMEM((tt, n_exp), BF16), pltpu.VMEM((_PEER_GROUP * _PEER_STAGE_PITCH, P_KEYS), F32),
                        pltpu.VMEM((2, tt, eb), BF16)],
        compiler_params=_cparams("parallel", "arbitrary"),
        name="peer_dense",
    )(xn, e_nat, g_nat, u_tab, v_tab)


def _peer(xn, wq, keys, u_tab, v_tab, layer=None):
    e_t, g_t = _peer_route_call(xn, wq, keys)
    heads = keys.shape[0]
    n_tok = xn.shape[0]
    e_nat = e_t.reshape(heads * P_TOPK, n_tok).T
    g_nat = g_t.reshape(heads * P_TOPK, n_tok).T
    return _peer_dense_call(xn, e_nat, g_nat, u_tab, v_tab, layer=layer)


def _split_bf16(x):
    hi = x.astype(BF16)
    lo = (x - hi.astype(F32)).astype(BF16)
    return hi, lo


def _mm(a, b):
    if a.ndim == 3:
        return lax.dot_general(a, b, (((2,), (1,)), ((0,), (0,))), preferred_element_type=F32)
    return jnp.dot(a, b, preferred_element_type=F32)


def _mm_nt(a, b):
    if a.ndim == 3:
        return lax.dot_general(a, b, (((2,), (2,)), ((0,), (0,))), preferred_element_type=F32)
    return lax.dot_general(a, b, (((1,), (1,)), ((), ())), preferred_element_type=F32)


def _dot3(a, b):
    ah, al = _split_bf16(a)
    bh, bl = _split_bf16(b)
    return _mm(ah, bh) + _mm(ah, bl) + _mm(al, bh)


_CONV_FILL_ROWS = 512


def _conv_pad_rows(cols):
    return -(-(cols + 1) // V7X_SUBLANES) * V7X_SUBLANES


def _conv_fill(x_ref, x0_s, xm_s, xq_s, *, l, cols, pad):
    tc = x0_s.shape[1]
    zeros = jnp.zeros((pad, tc), F32)
    for s in (x0_s, xm_s, xq_s):
        s[0:pad, :] = zeros
        s[pad + l:pad + l + pad, :] = zeros
    step = min(_CONV_FILL_ROWS, l)
    for s0 in range(0, l, step):
        x0_s[pad + s0:pad + s0 + step, :] = x_ref[s0:s0 + step, :]
    col = lax.broadcasted_iota(jnp.int32, (step, tc), 0) & (cols - 1)
    for s0 in range(0, l, step):
        xm_s[pad + s0:pad + s0 + step, :] = jnp.where(col >= 1, x0_s[pad + s0 - 1:pad + s0 - 1 + step, :], 0.0)
        xq_s[pad + s0:pad + s0 + step, :] = jnp.where(col <= cols - 2, x0_s[pad + s0 + 1:pad + s0 + 1 + step, :], 0.0)


def _conv_rows(x0_s, xm_s, xq_s, w_ref, t0, n, *, rows, cols, pad):
    acc = None
    for dr in ((-1, 0, 1) if rows > 1 else (0,)):
        base = pl.multiple_of(t0 + pad + dr * cols, V7X_SUBLANES)
        for dc, src in ((-1, xm_s), (0, x0_s), (1, xq_s)):
            tap = (dr + 1) * 3 + dc + 1
            term = src[pl.ds(base, n), :] * w_ref[tap:tap + 1, :]
            acc = term if acc is None else acc + term
    return acc


_CONV_CHUNK = 128


def _conv_a_kernel(z_ref, w_ref, o_ref, x0_s, xm_s, xq_s, *, l, rows, cols, pad, blocks_per_part):
    _conv_fill(z_ref.at[0], x0_s, xm_s, xq_s, l=l, cols=cols, pad=pad)
    part = pl.program_id(1) // blocks_per_part
    use_norm = part != 1
    post = jnp.where(part == 2, A_HEAD_DIM ** -0.5, 1.0)
    tc = o_ref.shape[2]

    def chunk(i, carry):
        t0 = pl.multiple_of(i * _CONV_CHUNK, _CONV_CHUNK)
        y = _conv_rows(x0_s, xm_s, xq_s, w_ref, t0, _CONV_CHUNK, rows=rows, cols=cols, pad=pad)
        y = y * jax.nn.sigmoid(y)
        outs = []
        for h0 in range(0, tc, A_HEAD_DIM):
            yh = y[:, h0:h0 + A_HEAD_DIM]
            inv = lax.rsqrt(jnp.sum(yh * yh, axis=-1, keepdims=True) + RMS_EPS) * post
            outs.append(yh * jnp.where(use_norm, inv, 1.0))
        o_ref[0, pl.ds(t0, _CONV_CHUNK), :] = jnp.concatenate(outs, axis=1) if len(outs) > 1 else outs[0]
        return carry

    lax.fori_loop(0, l // _CONV_CHUNK, chunk, 0, unroll=2)


def _conv_a_call(z_kvq, w9, rows, cols, tc=256):
    b, l, _ = z_kvq.shape
    ch = w9.shape[1]
    pad = _conv_pad_rows(cols)
    scr = pltpu.VMEM((l + 2 * pad, tc), F32)
    return pl.pallas_call(
        functools.partial(_conv_a_kernel, l=l, rows=rows, cols=cols, pad=pad, blocks_per_part=ch // 3 // tc),
        grid=(b, ch // tc),
        in_specs=[pl.BlockSpec((1, l, tc), lambda i, j: (i, 0, j)),
                  pl.BlockSpec((9, tc), lambda i, j: (0, j))],
        out_specs=pl.BlockSpec((1, l, tc), lambda i, j: (i, 0, j)),
        out_shape=jax.ShapeDtypeStruct((b, l, ch), F32),
        scratch_shapes=[scr, scr, scr],
        compiler_params=_cparams("parallel", "parallel"),
        name="conv_a",
    )(z_kvq, w9)


def _conv_c_kernel(z0_ref, z1_ref, z2_ref, w0_ref, w1_ref, w2_ref, x0_ref, u_ref, *scr, l, rows, cols, pad):
    parts = ((z0_ref, w0_ref, scr[0:3]), (z1_ref, w1_ref, scr[3:6]), (z2_ref, w2_ref, scr[6:9]))
    for z_ref, _, s in parts:
        _conv_fill(z_ref.at[0], *s, l=l, cols=cols, pad=pad)

    def chunk(i, carry):
        t0 = pl.multiple_of(i * _CONV_CHUNK, _CONV_CHUNK)
        y = [_conv_rows(*s, w_ref, t0, _CONV_CHUNK, rows=rows, cols=cols, pad=pad) for _, w_ref, s in parts]
        x0_ref[0, pl.ds(t0, _CONV_CHUNK), :] = y[0]
        u_ref[0, pl.ds(t0, _CONV_CHUNK), :] = y[1] * y[2]
        return carry

    lax.fori_loop(0, l // _CONV_CHUNK, chunk, 0, unroll=2)


def _conv_c_call(z_c, w9, rows, cols, col0=0, tc=128):
    b, l, _ = z_c.shape
    cw = w9.shape[1] // 3
    nb = cw // tc
    c0 = col0 // tc
    pad = _conv_pad_rows(cols)
    scr = pltpu.VMEM((l + 2 * pad, tc), F32)
    zspec = [pl.BlockSpec((1, l, tc), functools.partial(lambda i, j, p: (i, 0, c0 + j + p * nb), p=p)) for p in range(3)]
    wspec = [pl.BlockSpec((9, tc), functools.partial(lambda i, j, p: (0, j + p * nb), p=p)) for p in range(3)]
    ospec = pl.BlockSpec((1, l, tc), lambda i, j: (i, 0, j))
    return pl.pallas_call(
        functools.partial(_conv_c_kernel, l=l, rows=rows, cols=cols, pad=pad),
        grid=(b, nb),
        in_specs=zspec + wspec,
        out_specs=(ospec, ospec),
        out_shape=(jax.ShapeDtypeStruct((b, l, cw), F32), jax.ShapeDtypeStruct((b, l, cw), F32)),
        scratch_shapes=[scr] * 9,
        compiler_params=_cparams("parallel", "parallel"),
        name="conv_c",
    )(z_c, z_c, z_c, w9, w9, w9)


def _gate_prep_kernel(z_ref, alog_ref, dtb_ref, o_ref, *, l, heads):
    lanes = z_ref.shape[2]
    ii = lax.broadcasted_iota(jnp.int32, (A_CHUNK, A_CHUNK), 0)
    jj = lax.broadcasted_iota(jnp.int32, (A_CHUNK, A_CHUNK), 1)
    lower = jnp.where(ii >= jj, 1.0, 0.0)
    upper = jnp.where(ii <= jj, 1.0, 0.0)
    lane = lax.broadcasted_iota(jnp.int32, (A_CHUNK, lanes), 1)

    def chunk(n, carry):
        r0 = pl.multiple_of(n * A_CHUNK, A_CHUNK)
        z = z_ref[0, pl.ds(r0, A_CHUNK), :]
        beta = jax.nn.sigmoid(z)
        x = z + dtb_ref[...]
        softplus = jnp.maximum(x, 0.0) + jnp.log(1.0 + jnp.exp(-jnp.abs(x)))
        la = -jnp.exp(alog_ref[...]) * softplus
        pre = _dot3(lower, la)
        suf = _dot3(upper, la)
        o_ref[0, pl.ds(r0, A_CHUNK), :] = jnp.where(lane < 2 * heads, beta, jnp.where(lane < 3 * heads, pre, suf))
        return carry

    lax.fori_loop(0, l // A_CHUNK, chunk, 0)


def _gate_prep_call(z_ba, a_log, a_dt_bias):
    b, l, lanes = z_ba.shape
    heads = a_log.shape[1]
    pad = lambda p: jnp.pad(p.reshape(1, 2 * heads).astype(F32), ((0, 0), (2 * heads, lanes - 4 * heads)))
    return pl.pallas_call(
        functools.partial(_gate_prep_kernel, l=l, heads=heads),
        grid=(b,),
        in_specs=[pl.BlockSpec((1, l, lanes), lambda i: (i, 0, 0)),
                  pl.BlockSpec((1, lanes), lambda i: (0, 0)),
                  pl.BlockSpec((1, lanes), lambda i: (0, 0))],
        out_specs=pl.BlockSpec((1, l, lanes), lambda i: (i, 0, 0)),
        out_shape=jax.ShapeDtypeStruct((b, l, lanes), F32),
        compiler_params=_cparams("parallel"),
        name="gate_prep",
    )(z_ba, pad(a_log), pad(a_dt_bias))


_TRI_BASE_LOG2 = 3


def _dot1(a, b):
    return _mm(a.astype(BF16), b.astype(BF16))


def _unit_tri_inverse(a, dot):
    c = a.shape[-1]
    ii = lax.broadcasted_iota(jnp.int32, a.shape, a.ndim - 2)
    jj = lax.broadcasted_iota(jnp.int32, a.shape, a.ndim - 1)
    eye = jnp.where(ii == jj, 1.0, 0.0)
    p = -jnp.where((ii >> _TRI_BASE_LOG2) == (jj >> _TRI_BASE_LOG2), a, 0.0)
    t = eye + p
    span = 2
    while span < (1 << _TRI_BASE_LOG2):
        p = dot(p, p)
        t = t + dot(t, p)
        span *= 2
    log2 = _TRI_BASE_LOG2
    while (1 << log2) < c:
        pair = jnp.where((ii >> (log2 + 1)) == (jj >> (log2 + 1)), a, 0.0)
        cross = jnp.where((ii >> log2) == (jj >> log2), 0.0, pair)
        t = t - dot(t, dot(cross, t))
        log2 += 1
    return t


def _unit_tri_solve(a, rhs):
    t = _unit_tri_inverse(a, _dot1).astype(BF16)
    x0 = _mm(t, rhs.astype(BF16))
    resid = rhs - x0 - _dot3(a, x0)
    return x0 + _mm(t, resid.astype(BF16))


def _mm_tn(a, b):
    return lax.dot_general(a, b, (((1,), (1,)), ((0,), (0,))), preferred_element_type=F32)


_DELTA_GROUP = 8


def _delta_kernel(k_ref, v_ref, q_ref, col_ref, row_ref, s0_ref, og_ref, an_ref, o_ref, sfin_ref,
                  km_s, nm_s, p_s, r_s, eg_s, sall_s, st_s, *, l):
    c = A_CHUNK
    dk = A_HEAD_DIM
    n_chunks = l // c
    g = min(_DELTA_GROUP, n_chunks)
    shape = (2 * g, c, c)
    ii = lax.broadcasted_iota(jnp.int32, shape, 1)
    jj = lax.broadcasted_iota(jnp.int32, shape, 2)
    lag = jnp.where(lax.broadcasted_iota(jnp.int32, shape, 0) >= g, jj - ii, ii - jj)
    incl = lag >= 0
    strict = lag > 0
    two = lambda x: jnp.concatenate([x, x], axis=0)

    def pass1(gi, carry):
        rows_blk = pl.ds(pl.multiple_of(gi * (g * c), g * c), g * c)
        chunks = pl.ds(gi * g, g)
        kc = k_ref[0, rows_blk, :].reshape(g, c, dk)
        vc = v_ref[0, rows_blk, :].reshape(g, c, dk)
        qc = q_ref[0, rows_blk, :].reshape(g, c, dk)
        cols = col_ref[0, 0, rows_blk, :].reshape(g, c, 8)
        rws = row_ref[0, 0, chunks]
        beta = jnp.concatenate([cols[:, :, 0:1], cols[:, :, 1:2]], axis=0)
        gcol = jnp.concatenate([cols[:, :, 2:3], cols[:, :, 3:4]], axis=0)
        grow = jnp.concatenate([rws[:, 0:1, :], rws[:, 1:2, :]], axis=0)
        dec = jnp.where(incl, jnp.exp(jnp.where(incl, gcol - grow, 0.0)), 0.0)
        kb = kc.astype(BF16)
        kk = two(_mm_nt(kb, kb))
        qk = two(_mm_nt(qc.astype(BF16), kb))
        eg = jnp.exp(gcol)
        k2, v2, q2 = two(kc), two(vc), two(qc)
        sol = _unit_tri_solve(jnp.where(strict, kk * dec * beta, 0.0),
                              jnp.concatenate([v2 * beta, k2 * (beta * eg)], axis=2))
        ub = sol[:, :, :dk].astype(BF16)
        wb = sol[:, :, dk:].astype(BF16)
        glast = jnp.concatenate([gcol[:g, c - 1:c, :], gcol[g:, 0:1, :]], axis=0)
        ke = (k2 * jnp.exp(glast - gcol)).astype(BF16)
        qkd = (qk * dec).astype(BF16)
        km = _mm_tn(ke, wb)
        nm = _mm_tn(ke, ub)
        pm = q2 * eg - _mm(qkd, wb)
        rm = _mm(qkd, ub)
        eglast = jnp.exp(glast)
        for d in range(2):
            sl = slice(d * g, (d + 1) * g)
            km_s[d, chunks] = km[sl].astype(BF16)
            nm_s[d, chunks] = nm[sl]
            p_s[d, rows_blk, :] = pm[sl].reshape(g * c, dk).astype(BF16)
            r_s[d, rows_blk, :] = rm[sl].reshape(g * c, dk)
            eg_s[d, chunks] = jnp.broadcast_to(eglast[sl], (g, V7X_SUBLANES, dk))
        return carry

    lax.fori_loop(0, n_chunks // g, pass1, 0)

    st_s[...] = s0_ref[0, 0]

    def pass2(n, carry):
        for d, ch in ((0, n), (1, n_chunks - 1 - n)):
            s = st_s[d]
            sb = s.astype(BF16)
            sall_s[d, ch] = sb
            st_s[d] = s * eg_s[d, ch][0:1, :] - jnp.dot(km_s[d, ch], sb, preferred_element_type=F32) + nm_s[d, ch]
        return carry

    lax.fori_loop(0, n_chunks, pass2, 0)
    sfin_ref[0, 0] = st_s[...]

    def pass3(gi, carry):
        rows_blk = pl.ds(pl.multiple_of(gi * (g * c), g * c), g * c)
        chunks = pl.ds(gi * g, g)
        o = None
        for d in range(2):
            term = _mm(p_s[d, rows_blk, :].reshape(g, c, dk), sall_s[d, chunks]) + r_s[d, rows_blk, :].reshape(g, c, dk)
            o = term if o is None else o + term
        o = o.reshape(g * c, dk)
        o = o * lax.rsqrt(jnp.mean(o * o, axis=-1, keepdims=True) + RMS_EPS) * an_ref[...]
        og = og_ref[0, rows_blk, :]
        o_ref[0, rows_blk, :] = (o * (og * jax.nn.sigmoid(og))).astype(o_ref.dtype)
        return carry

    lax.fori_loop(0, n_chunks // g, pass3, 0)


def _delta_call(act, z_og, gates, s0, a_norm, og_col=0):
    b, l, ch = act.shape
    heads = ch // 3 // A_HEAD_DIM
    dk = A_HEAD_DIM
    c = A_CHUNK
    n_chunks = l // c
    g4 = jnp.stack([gates[..., i * heads:(i + 1) * heads] for i in range(4)], axis=-1)
    col = jnp.pad(jnp.transpose(g4, (0, 2, 1, 3)), ((0, 0), (0, 0), (0, 0), (0, 4)))
    row = jnp.transpose(g4[..., 2:4].reshape(b, n_chunks, c, heads, 2), (0, 3, 1, 4, 2))
    row = jnp.pad(row, ((0, 0), (0, 0), (0, 0), (0, 6), (0, 0)))
    tok = lambda off: pl.BlockSpec((1, l, dk), functools.partial(lambda i, h, off: (i, 0, h + off), off=off))
    return pl.pallas_call(
        functools.partial(_delta_kernel, l=l),
        grid=(b, heads),
        in_specs=[tok(0), tok(heads), tok(2 * heads),
                  pl.BlockSpec((1, 1, l, 8), lambda i, h: (i, h, 0, 0)),
                  pl.BlockSpec((1, 1, n_chunks, 8, c), lambda i, h: (i, h, 0, 0, 0)),
                  pl.BlockSpec((1, 1, 2, dk, dk), lambda i, h: (i, h, 0, 0, 0)),
                  tok(og_col // dk),
                  pl.BlockSpec((1, dk), lambda i, h: (0, 0))],
        out_specs=(tok(0), pl.BlockSpec((1, 1, 2, dk, dk), lambda i, h: (i, h, 0, 0, 0))),
        out_shape=(jax.ShapeDtypeStruct((b, l, heads * dk), BF16),
                   jax.ShapeDtypeStruct((b, heads, 2, dk, dk), F32)),
        scratch_shapes=[pltpu.VMEM((2, n_chunks, dk, dk), BF16), pltpu.VMEM((2, n_chunks, dk, dk), F32),
                        pltpu.VMEM((2, l, dk), BF16), pltpu.VMEM((2, l, dk), F32),
                        pltpu.VMEM((2, n_chunks, V7X_SUBLANES, dk), F32),
                        pltpu.VMEM((2, n_chunks, dk, dk), BF16), pltpu.VMEM((2, dk, dk), F32)],
        compiler_params=_cparams("parallel", "parallel"),
        name="delta_rule",
    )(act, act, act, col, row, s0, z_og, a_norm.reshape(1, dk).astype(F32))


def _gmlp_kernel(zu_ref, zv_ref, ws_ref, bs_ref, o_ref):
    groups = ws_ref.shape[0]
    gd = zu_ref.shape[1] // groups
    u = jax.nn.gelu(zu_ref[...])
    v = jax.nn.gelu(zv_ref[...])
    mu = jnp.mean(v, axis=-1, keepdims=True)
    var = jnp.mean(jnp.square(v - mu), axis=-1, keepdims=True)
    vn = ((v - mu) * lax.rsqrt(var + RMS_EPS)).astype(BF16)
    mixed = [jnp.dot(ws_ref[g], vn[:, g * gd:(g + 1) * gd], preferred_element_type=F32) + bs_ref[:, g:g + 1]
             for g in range(groups)]
    o_ref[...] = (u * jnp.concatenate(mixed, axis=1)).astype(o_ref.dtype)


def _gmlp_call(z_b, w_s, b_s, bw, col0=0):
    n_tok = z_b.shape[0]
    groups, p, _ = w_s.shape
    c0 = col0 // bw
    return pl.pallas_call(
        _gmlp_kernel,
        grid=(n_tok // p,),
        in_specs=[pl.BlockSpec((p, bw), lambda i: (i, c0)),
                  pl.BlockSpec((p, bw), lambda i: (i, c0 + 1)),
                  pl.BlockSpec((groups, p, p), lambda i: (0, 0, 0)),
                  pl.BlockSpec((p, groups), lambda i: (0, 0))],
        out_specs=pl.BlockSpec((p, bw), lambda i: (i, 0)),
        out_shape=jax.ShapeDtypeStruct((n_tok, bw), BF16),
        compiler_params=_cparams("parallel"),
        name="gmlp",
    )(z_b, z_b, w_s, b_s.T.astype(F32))


_FEAT_PAD = 64


def _filter_feats(l):
    pos = np.arange(l, dtype=np.float32)
    t = np.linspace(0.0, 1.0, l, dtype=np.float32)[:, None]
    bands = np.linspace(1e-4, C_POS_BANDS - 1, C_POS_BANDS, dtype=np.float32)
    ang = np.float32(2.0 * math.pi / l) * pos[:, None] * bands
    feats = np.concatenate([t, np.cos(ang), -np.sin(ang)], axis=-1).astype(np.float32)
    return np.pad(feats, ((0, 0), (0, _FEAT_PAD - feats.shape[1])))


def _filter_kernel(feat_ref, w1_ref, b1_ref, f1_ref, w2_ref, b2_ref, f2_ref, w3_ref, b3_ref, rate_ref, o_ref):
    feats = feat_ref[...]
    hid = jnp.sin(f1_ref[...] * (_dot3(feats, w1_ref[...]) + b1_ref[...]))
    hid = jnp.sin(f2_ref[...] * (_dot3(hid, w2_ref[...]) + b2_ref[...]))
    filt = _dot3(hid, w3_ref[...]) + b3_ref[...]
    filt = filt * jnp.exp(-feats[:, 0:1] * rate_ref[...])
    o_ref[...] = filt * lax.rsqrt(jnp.sum(filt * filt, axis=0, keepdims=True) + RMS_EPS)


def _filter_call(l, fw1, fb1, freq1, fw2, fb2, freq2, fw3, fb3, tc=256):
    hidden = fw1.shape[1]
    two_c = fw3.shape[1]
    cw = two_c // 2
    feats = jnp.asarray(_filter_feats(l))
    w1 = jnp.pad(fw1.astype(F32), ((0, _FEAT_PAD - fw1.shape[0]), (0, 0)))
    rate = np.abs(np.linspace(C_MIN_DECAY, C_MAX_DECAY, cw, dtype=np.float32))
    rate = jnp.asarray(np.concatenate([rate, rate]).reshape(1, two_c))
    row = lambda v: v.reshape(1, -1).astype(F32)
    full = lambda shape: pl.BlockSpec(shape, lambda j: (0, 0))
    return pl.pallas_call(
        _filter_kernel,
        grid=(two_c // tc,),
        in_specs=[full((l, _FEAT_PAD)), full((_FEAT_PAD, hidden)), full((1, hidden)), full((1, hidden)),
                  full((hidden, hidden)), full((1, hidden)), full((1, hidden)),
                  pl.BlockSpec((hidden, tc), lambda j: (0, j)), pl.BlockSpec((1, tc), lambda j: (0, j)),
                  pl.BlockSpec((1, tc), lambda j: (0, j))],
        out_specs=pl.BlockSpec((l, tc), lambda j: (0, j)),
        out_shape=jax.ShapeDtypeStruct((l, two_c), F32),
        compiler_params=_cparams("parallel"),
        name="hyena_filter",
    )(feats, w1, row(fb1), row(freq1), fw2.astype(F32), row(fb2), row(freq2), fw3.astype(F32), row(fb3), rate)


def _dft_sizes(l):
    n1 = 64 if l >= 4096 else 32
    return n1, (2 * l) // n1


def _hi_lo(x):
    x = jnp.asarray(x, F32)
    hi = x.astype(BF16)
    return hi, (x - hi.astype(F32)).astype(BF16)


def _dft_consts(l):
    n1, n2 = _dft_sizes(l)
    n = n1 * n2
    half = n1 // 2
    k1 = np.arange(n1)[None, :, None]
    m1 = np.arange(half)[None, None, :]
    m2 = np.arange(n2)[:, None, None]
    f1 = np.exp(-2j * np.pi * (m1 * k1 / n1 + m2 * k1 / n))
    fwd = np.concatenate([f1.real, f1.imag], axis=1)
    fh, fl = _hi_lo(fwd)
    fwd1 = jnp.concatenate([fh, fh, fl], axis=2)
    rt = np.transpose(f1.real, (0, 2, 1)) / n
    it = np.transpose(f1.imag, (0, 2, 1)) / n
    (rh, rl), (ih, il) = _hi_lo(rt), _hi_lo(it)
    inv1 = jnp.concatenate([rh, ih, rh, ih, rl, il], axis=2)
    kk = np.arange(n2)
    f2 = np.exp(-2j * np.pi * np.outer(kk, kk) / n2)
    m = np.block([[f2.real, -f2.imag], [f2.imag, f2.real]])
    mi = np.block([[f2.real, f2.imag], [-f2.imag, f2.real]])
    (mh, ml), (mih, mil) = _hi_lo(m), _hi_lo(mi)
    return fwd1, jnp.concatenate([mh, mh, ml], axis=1), jnp.concatenate([mih, mih, mil], axis=1), inv1


_DFT_UNROLL = 4


def _stack3(x):
    hi, lo = _split_bf16(x)
    return jnp.concatenate([hi, lo, hi], axis=0)


def _dft_pitch(n2):
    return n2 + V7X_SUBLANES


def _dft_level1(x_ref, fwd1_ref, ar_s, ai_s, n1, n2):
    half = n1 // 2
    pitch = _dft_pitch(n2)

    def body(j, carry):
        xs = x_ref[pl.ds(j, half, stride=n2), :]
        a = jnp.dot(fwd1_ref[j], _stack3(xs), preferred_element_type=F32)
        ar_s[pl.ds(j, n1, stride=pitch), :] = a[:n1]
        ai_s[pl.ds(j, n1, stride=pitch), :] = a[n1:]
        return carry

    lax.fori_loop(0, n2, body, 0, unroll=_DFT_UNROLL)


def _block_rows(k1, n2):
    return pl.ds(pl.multiple_of(k1 * _dft_pitch(n2), V7X_SUBLANES), n2)


def _pair_rows(pair, n2):
    return pl.ds(pl.multiple_of(pair * (2 * n2), 2 * n2), 2 * n2)


def _side_by_side(x, n2):
    return jnp.concatenate([x[:n2], x[n2:]], axis=1)


def _dft_level2(ar_s, ai_s, m3_ref, pair, n2):
    lo, hi = _block_rows(2 * pair, n2), _block_rows(2 * pair + 1, n2)
    blk = jnp.concatenate([jnp.concatenate([ar_s[lo, :], ar_s[hi, :]], axis=1),
                           jnp.concatenate([ai_s[lo, :], ai_s[hi, :]], axis=1)], axis=0)
    x = jnp.dot(m3_ref[...], _stack3(blk), preferred_element_type=F32)
    return x[:n2], x[n2:]


def _spectrum_kernel(hf_ref, hb_ref, fwd1_ref, m3_ref, o_ref, ar_s, ai_s, *, n1, n2):
    tc = o_ref.shape[2]
    for which, h_ref in enumerate((hf_ref, hb_ref)):
        _dft_level1(h_ref, fwd1_ref, ar_s, ai_s, n1, n2)

        def body(pair, carry):
            xr, xi = _dft_level2(ar_s, ai_s, m3_ref, pair, n2)
            for side in range(2):
                rows = pl.ds(pl.multiple_of((2 * pair + side) * n2, n2), n2)
                lanes = slice(side * tc, (side + 1) * tc)
                if which == 0:
                    o_ref[0, rows, :] = xr[:, lanes]
                    o_ref[1, rows, :] = xi[:, lanes]
                else:
                    o_ref[0, rows, :] += xr[:, lanes]
                    o_ref[1, rows, :] -= xi[:, lanes]
            return carry

        lax.fori_loop(0, n1 // 2, body, 0, unroll=_DFT_UNROLL // 2)


def _spectrum_call(filt, consts, tc=128):
    l, two_c = filt.shape
    cw = two_c // 2
    nb = cw // tc
    n1, n2 = _dft_sizes(l)
    fwd1, m3, _, _ = consts
    return pl.pallas_call(
        functools.partial(_spectrum_kernel, n1=n1, n2=n2),
        grid=(nb,),
        in_specs=[pl.BlockSpec((l, tc), lambda j: (0, j)),
                  pl.BlockSpec((l, tc), lambda j: (0, j + nb)),
                  pl.BlockSpec(fwd1.shape, lambda j: (0, 0, 0)),
                  pl.BlockSpec(m3.shape, lambda j: (0, 0))],
        out_specs=pl.BlockSpec((2, 2 * l, tc), lambda j: (0, 0, j)),
        out_shape=jax.ShapeDtypeStruct((2, 2 * l, cw), F32),
        scratch_shapes=[pltpu.VMEM((n1 * _dft_pitch(n2), tc), F32), pltpu.VMEM((n1 * _dft_pitch(n2), tc), F32)],
        compiler_params=_cparams("parallel"),
        name="hyena_spectrum",
    )(filt, filt, fwd1, m3)


def _longconv_kernel(u_ref, x0_ref, h_ref, skip_ref, fwd1_ref, m3_ref, mi3_ref, inv1_ref, o_ref,
                     ar_s, ai_s, y_s, *, l, n1, n2):
    half = n1 // 2
    _dft_level1(u_ref.at[0], fwd1_ref, ar_s, ai_s, n1, n2)

    tc = o_ref.shape[2]

    pitch = _dft_pitch(n2)

    def per_k1_pair(pair, carry):
        xr, xi = _dft_level2(ar_s, ai_s, m3_ref, pair, n2)
        rows = _pair_rows(pair, n2)
        hr = _side_by_side(h_ref[0, rows, :], n2)
        hi = _side_by_side(h_ref[1, rows, :], n2)
        y = jnp.concatenate([xr * hr - xi * hi, xr * hi + xi * hr], axis=0)
        b = jnp.dot(mi3_ref[...], _stack3(y), preferred_element_type=F32)
        for side in range(2):
            blk = _block_rows(2 * pair + side, n2)
            ar_s[blk, :] = b[:n2, side * tc:(side + 1) * tc]
            ai_s[blk, :] = b[n2:, side * tc:(side + 1) * tc]
        return carry

    lax.fori_loop(0, n1 // 2, per_k1_pair, 0, unroll=_DFT_UNROLL // 2)

    def per_n2(j, carry):
        br = ar_s[pl.ds(j, n1, stride=pitch), :]
        bi = ai_s[pl.ds(j, n1, stride=pitch), :]
        (brh, brl), (bih, bil) = _split_bf16(br), _split_bf16(bi)
        rhs = jnp.concatenate([brh, bih, brl, bil, brh, bih], axis=0)
        y_s[pl.ds(j, half, stride=pitch), :] = jnp.dot(inv1_ref[j], rhs, preferred_element_type=F32)
        return carry

    lax.fori_loop(0, n2, per_n2, 0, unroll=_DFT_UNROLL)

    def finish(i, carry):
        rows = pl.ds(pl.multiple_of(i * n2, n2), n2)
        u = u_ref[0, rows, :]
        y = y_s[_block_rows(i, n2), :]
        o_ref[0, rows, :] = (x0_ref[0, rows, :] * (y + skip_ref[...] * u)).astype(o_ref.dtype)
        return carry

    lax.fori_loop(0, half, finish, 0, unroll=_DFT_UNROLL)


def _longconv_call(u, x0, spec, skip, consts, tc=128):
    b, l, cw = u.shape
    n1, n2 = _dft_sizes(l)
    fwd1, m3, mi3, inv1 = consts
    tok = pl.BlockSpec((1, l, tc), lambda j, i: (i, 0, j))
    const = lambda a: pl.BlockSpec(a.shape, lambda j, i: (0,) * a.ndim)
    return pl.pallas_call(
        functools.partial(_longconv_kernel, l=l, n1=n1, n2=n2),
        grid=(cw // tc, b),
        in_specs=[tok, tok,
                  pl.BlockSpec((2, 2 * l, tc), lambda j, i: (0, 0, j)),
                  pl.BlockSpec((1, tc), lambda j, i: (0, j)),
                  const(fwd1), const(m3), const(mi3), const(inv1)],
        out_specs=tok,
        out_shape=jax.ShapeDtypeStruct((b, l, cw), BF16),
        scratch_shapes=[pltpu.VMEM((n1 * _dft_pitch(n2), tc), F32), pltpu.VMEM((n1 * _dft_pitch(n2), tc), F32),
                        pltpu.VMEM((n1 // 2 * _dft_pitch(n2), tc), F32)],
        compiler_params=_cparams("parallel", "parallel"),
        name="hyena_longconv",
    )(u, x0, spec, skip.reshape(1, cw).astype(F32), fwd1, m3, mi3, inv1)


def _weight_stacks(p):
    aw = p["w_br_a"].shape[1]
    heads = p["a_log"].shape[2]
    off_ba = 2 * aw
    off_q = off_ba + 4 * heads
    w_in = p["w_in"]
    cast = lambda w: w.astype(BF16)
    return {
        "w_main": cast(jnp.concatenate([w_in[:, :, :off_ba], w_in[:, :, off_q:]], axis=2)),
        "w_ba": cast(jnp.pad(w_in[:, :, off_ba:off_q], ((0, 0), (0, 0), (0, V7X_LANES - 4 * heads)))),
        "w_out": cast(p["w_out"]), "p_u": cast(p["p_u"]), "p_v": cast(p["p_v"]),
    }


def _layer_weights(i, p, stacks):
    aw, bw, cw = p["w_br_a"].shape[1], p["w_br_b"].shape[1], p["w_br_c"].shape[1]
    cast = lambda w: w.astype(BF16)
    col_og = 3 * aw
    col_b = col_og + aw
    col_c = col_b + 2 * bw
    col_gate = col_c + 3 * cw
    return {
        "layer": i, **stacks,
        "col_og": col_og, "col_b": col_b, "col_c": col_c, "col_gate": col_gate, "bw": bw,
        "w_br_a": cast(p["w_br_a"][i]), "w_br_b": cast(p["w_br_b"][i]), "w_br_c": cast(p["w_br_c"][i]),
        "a_conv": p["a_conv"][i].reshape(9, -1), "c_conv": p["c_conv"][i].reshape(9, -1),
        "a_log": p["a_log"][i], "a_dt_bias": p["a_dt_bias"][i], "a_norm": p["a_norm"][i],
        "b_ws": cast(p["b_ws"][i]), "b_bs": p["b_bs"][i], "c_skip": p["c_skip"][i],
        "filter": tuple(p[k][i] for k in ("c_fw1", "c_fb1", "c_freq1", "c_fw2", "c_fb2", "c_freq2", "c_fw3", "c_fb3")),
        "p_wq": cast(p["p_wq"][i]), "p_keys": cast(p["p_keys"][i]),
    }


def _delta_branch(xf, z, b, l, rows, cols, s0, lw):
    z3 = z.reshape(b, l, -1)
    z_ba = _matmul(xf, lw["w_ba"], layer=lw["layer"]).reshape(b, l, -1)
    act = _conv_a_call(z3, lw["a_conv"], rows, cols)
    gates = _gate_prep_call(z_ba, lw["a_log"], lw["a_dt_bias"])
    o_a, s_fin = _delta_call(act, z3, gates, s0, lw["a_norm"], og_col=lw["col_og"])
    return o_a.reshape(b * l, -1), s_fin


def _token_mixer(xn, rows, cols, s0, lw):
    b, l, d = xn.shape
    xf = xn.reshape(b * l, d)
    z = _matmul(xf, lw["w_main"], layer=lw["layer"])
    o_a, s_fin = _delta_branch(xf, z, b, l, rows, cols, s0, lw)
    o_b = _gmlp_call(z, lw["b_ws"], lw["b_bs"], lw["bw"], col0=lw["col_b"])
    x0, u = _conv_c_call(z.reshape(b, l, -1), lw["c_conv"], rows, cols, col0=lw["col_c"])
    consts = _dft_consts(l)
    spec = _spectrum_call(_filter_call(l, *lw["filter"]), consts)
    o_c = _longconv_call(u, x0, spec, lw["c_skip"], consts).reshape(b * l, -1)
    merged = _merge_call(o_a, o_b, o_c, lw["w_br_a"], lw["w_br_b"], lw["w_br_c"], z, lw["col_gate"])
    return _matmul(merged, lw["w_out"], layer=lw["layer"]).reshape(b, l, d), s_fin


def _peer_layer(xn, lw):
    b, l, d = xn.shape
    return _peer(xn.reshape(b * l, d), lw["p_wq"], lw["p_keys"], lw["p_u"], lw["p_v"],
                 layer=lw["layer"]).reshape(b, l, d)


def kernel(x, c, ctx, c_ctx, ada_w, ada_b, w_in, a_conv, a_log, a_dt_bias, a_norm, b_ws, b_bs, c_conv, c_fw1, c_fb1,
           c_freq1, c_fw2, c_fb2, c_freq2, c_fw3, c_fb3, c_skip, w_br_a, w_br_b, w_br_c, w_out, p_wq, p_keys, p_u,
           p_v, final_norm):
    params = dict(w_in=w_in, a_conv=a_conv, a_log=a_log, a_dt_bias=a_dt_bias, a_norm=a_norm, b_ws=b_ws, b_bs=b_bs,
                  c_conv=c_conv, c_fw1=c_fw1, c_fb1=c_fb1, c_freq1=c_freq1, c_fw2=c_fw2, c_fb2=c_fb2, c_freq2=c_freq2,
                  c_fw3=c_fw3, c_fb3=c_fb3, c_skip=c_skip, w_br_a=w_br_a, w_br_b=w_br_b, w_br_c=w_br_c, w_out=w_out,
                  p_wq=p_wq, p_keys=p_keys, p_u=p_u, p_v=p_v)
    stacks = _weight_stacks(params)
    b, l, d = x.shape
    depth = w_in.shape[0]
    rows = l // GRID_W
    l_ctx = ctx.shape[1]
    heads = a_log.shape[2]
    s_zero = jnp.zeros((b, heads, 2, A_HEAD_DIM, A_HEAD_DIM), F32)
    cvec = jnp.concatenate([c, c_ctx[None, :], jnp.zeros((V7X_SUBLANES - b - 1, d), F32)], axis=0)

    h, h_pending = x, None
    hc, hc_pending = ctx, None
    for i in range(depth):
        lw = _layer_weights(i, params, stacks)
        mod_all = _mod_call(cvec, ada_w, ada_b[i], i)
        mod = mod_all[:b].reshape(b, N_MOD, 1, d)
        mod_c = mod_all[b].reshape(N_MOD, 1, 1, d)

        def norm(stream, pending, scale, shift):
            if pending is None:
                return stream, _norm_mod_call(stream, scale, shift)[1]
            return _norm_mod_call(stream, scale, shift, delta=pending[0], gate=pending[1], emit_h=True)

        hc, xnc = norm(hc, hc_pending, mod_c[1], mod_c[0])
        if i == depth - 1:
            xcf = xnc.reshape(b * l_ctx, d)
            z_c = _matmul(xcf, lw["w_main"], ncols=lw["col_b"], layer=i)
            _, s_ctx = _delta_branch(xcf, z_c, b, l_ctx, 1, l_ctx, s_zero, lw)
        else:
            out_c, s_ctx = _token_mixer(xnc, 1, l_ctx, s_zero, lw)
            hc, xnc2 = norm(hc, (out_c, mod_c[2]), mod_c[4], mod_c[3])
            hc_pending = (_peer_layer(xnc2, lw), mod_c[5])
        h, xn = norm(h, h_pending, mod[:, 1], mod[:, 0])
        out, _ = _token_mixer(xn, rows, GRID_W, s_ctx, lw)
        h, xn2 = norm(h, (out, mod[:, 2]), mod[:, 4], mod[:, 3])
        h_pending = (_peer_layer(xn2, lw), mod[:, 5])
    return _norm_mod_call(h, final_norm.reshape(1, 1, d), delta=h_pending[0], gate=h_pending[1], out_dtype=F32)[1]
```

```python
import functools
import math

import numpy as np
import jax
import jax.numpy as jnp
from jax import lax
from jax.experimental import pallas as pl
from jax.experimental.pallas import tpu as pltpu

F32 = jnp.float32
BF16 = jnp.bfloat16

V7X_LANES = 128
V7X_SUBLANES = 8
V7X_VMEM_LIMIT_BYTES = 56 * 1024 * 1024

GRID_W = 64
RMS_EPS = 1e-6
N_MOD = 6
A_HEAD_DIM = 128
A_CHUNK = 64
B_CHUNK = 128
C_POS_BANDS = 16
C_MIN_DECAY = math.log(1e-2) / 1.5
C_MAX_DECAY = math.log(1e-2) / 0.3
N_BRANCH = 3
P_KEYS = 128
P_TOPK = 16


def _cparams(*sem):
    return pltpu.CompilerParams(dimension_semantics=sem, vmem_limit_bytes=V7X_VMEM_LIMIT_BYTES)


def _mod_kernel(c_ref, w_ref, b_ref, o_ref):
    c = c_ref[...]
    a = (c * jax.nn.sigmoid(c)).astype(BF16)
    o_ref[...] = jnp.dot(a, w_ref[...].astype(BF16), preferred_element_type=F32) + b_ref[...]


def _mod_call(cvec, ada_w, ada_b, layer):
    rows, d = cvec.shape
    n = ada_w.shape[-1]
    tn = 1024
    return pl.pallas_call(
        _mod_kernel,
        grid=(n // tn,),
        in_specs=[pl.BlockSpec((rows, d), lambda j: (0, 0)),
                  _layer_block((d, tn), lambda j: (0, j), ada_w, layer),
                  pl.BlockSpec((1, tn), lambda j: (0, j))],
        out_specs=pl.BlockSpec((rows, tn), lambda j: (0, j)),
        out_shape=jax.ShapeDtypeStruct((rows, n), F32),
        compiler_params=_cparams("parallel"),
        name="ada_mod",
    )(cvec, ada_w, ada_b.reshape(1, n))


def _norm_mod_kernel(*refs, with_delta, with_shift, emit_h):
    refs = list(refs)
    h = refs.pop(0)[0]
    if with_delta:
        d_ref, g_ref = refs.pop(0), refs.pop(0)
        h = h + g_ref[0] * d_ref[0]
    sc_ref = refs.pop(0)
    xn = h * lax.rsqrt(jnp.mean(h * h, axis=-1, keepdims=True) + RMS_EPS)
    if with_shift:
        xn = xn * (1.0 + sc_ref[0]) + refs.pop(0)[0]
    else:
        xn = xn * sc_ref[0]
    if emit_h:
        refs.pop(0)[0] = h
    refs.pop(0)[0] = xn.astype(refs[0].dtype)


def _norm_mod_call(h, scale, shift=None, delta=None, gate=None, emit_h=False, out_dtype=BF16, tl=256):
    b, l, d = h.shape
    tl = min(tl, l)
    tok = pl.BlockSpec((1, tl, d), lambda i, j: (i, j, 0))

    def vec(v):
        if v.shape[0] == 1:
            return pl.BlockSpec((1, 1, d), lambda i, j: (0, 0, 0))
        return pl.BlockSpec((1, 1, d), lambda i, j: (i, 0, 0))

    args, in_specs = [h], [tok]
    if delta is not None:
        args += [delta, gate]
        in_specs += [tok, vec(gate)]
    args.append(scale)
    in_specs.append(vec(scale))
    if shift is not None:
        args.append(shift)
        in_specs.append(vec(shift))
    out_shape, out_specs = [], []
    if emit_h:
        out_shape.append(jax.ShapeDtypeStruct((b, l, d), F32))
        out_specs.append(tok)
    out_shape.append(jax.ShapeDtypeStruct((b, l, d), out_dtype))
    out_specs.append(tok)
    res = pl.pallas_call(
        functools.partial(_norm_mod_kernel, with_delta=delta is not None, with_shift=shift is not None, emit_h=emit_h),
        grid=(b, l // tl),
        in_specs=in_specs, out_specs=tuple(out_specs), out_shape=tuple(out_shape),
        compiler_params=_cparams("parallel", "parallel"),
        name="norm_mod",
    )(*args)
    return (res[0], res[1]) if emit_h else (None, res[0])


def _mm_kernel(a_ref, w_ref, o_ref):
    o_ref[...] = jnp.dot(a_ref[...], w_ref[...], preferred_element_type=F32).astype(o_ref.dtype)


def _pick_tile(n, pref):
    t = min(pref, n)
    while n % t:
        t //= 2
    return t


def _layer_block(block, index_map, w, layer):
    if layer is None:
        return pl.BlockSpec(block, index_map)
    assert w.ndim == len(block) + 1
    return pl.BlockSpec((None,) + block, lambda *g: (layer,) + tuple(index_map(*g)))


def _matmul(a, w, out_dtype=F32, tm=512, tn=1024, ncols=None, layer=None):
    m, k = a.shape
    n = w.shape[-1] if ncols is None else ncols
    tm = _pick_tile(m, tm)
    tn = _pick_tile(n, tn)
    return pl.pallas_call(
        _mm_kernel,
        grid=(n // tn, m // tm),
        in_specs=[pl.BlockSpec((tm, k), lambda j, i: (i, 0)),
                  _layer_block((k, tn), lambda j, i: (0, j), w, layer)],
        out_specs=pl.BlockSpec((tm, tn), lambda j, i: (i, j)),
        out_shape=jax.ShapeDtypeStruct((m, n), out_dtype),
        compiler_params=_cparams("parallel", "parallel"),
        name="matmul",
    )(a, w)


def _merge_kernel(oa_ref, ob_ref, oc_ref, wa_ref, wb_ref, wc_ref, ga_ref, gb_ref, gc_ref, o_ref):
    acc = jax.nn.sigmoid(ga_ref[...]) * jnp.dot(oa_ref[...], wa_ref[...], preferred_element_type=F32)
    acc += jax.nn.sigmoid(gb_ref[...]) * jnp.dot(ob_ref[...], wb_ref[...], preferred_element_type=F32)
    acc += jax.nn.sigmoid(gc_ref[...]) * jnp.dot(oc_ref[...], wc_ref[...], preferred_element_type=F32)
    o_ref[...] = acc.astype(o_ref.dtype)


def _merge_call(o_a, o_b, o_c, w_a, w_b, w_c, z, gate_col, tm=512, tn=512):
    n_tok = o_a.shape[0]
    d = w_a.shape[1]
    tm = _pick_tile(n_tok, tm)
    tn = _pick_tile(math.gcd(d, gate_col) if gate_col else d, tn)
    nb = d // tn
    g0 = gate_col // tn

    def act(o):
        return pl.BlockSpec((tm, o.shape[1]), lambda j, i: (i, 0))

    def wgt(w):
        return pl.BlockSpec((w.shape[0], tn), lambda j, i: (0, j))

    def gate(br):
        return pl.BlockSpec((tm, tn), lambda j, i: (i, g0 + j + br * nb))

    return pl.pallas_call(
        _merge_kernel,
        grid=(nb, n_tok // tm),
        in_specs=[act(o_a), act(o_b), act(o_c), wgt(w_a), wgt(w_b), wgt(w_c), gate(0), gate(1), gate(2)],
        out_specs=pl.BlockSpec((tm, tn), lambda j, i: (i, j)),
        out_shape=jax.ShapeDtypeStruct((n_tok, d), BF16),
        compiler_params=_cparams("parallel", "parallel"),
        name="branch_merge",
    )(o_a, o_b, o_c, w_a, w_b, w_c, z, z, z)


def _topk_rows(s, payload, k):
    n, t = s.shape
    iota = lax.broadcasted_iota(jnp.int32, (n, t), 0).astype(F32)
    riota = lax.broadcasted_iota(jnp.int32, (k, t), 0)

    def body(r, carry):
        s, tv, tp = carry
        m = jnp.max(s, axis=0, keepdims=True)
        idx = jnp.min(jnp.where(s == m, iota, float(n)), axis=0, keepdims=True)
        sel = iota == idx
        p = idx if payload is None else jnp.max(jnp.where(sel, payload, -1.0), axis=0, keepdims=True)
        s = jnp.where(sel, -jnp.inf, s)
        tv = jnp.where(riota == r, m, tv)
        tp = jnp.where(riota == r, p, tp)
        return s, tv, tp

    _, tv, tp = lax.fori_loop(0, k, body, (s, jnp.zeros((k, t), F32), jnp.zeros((k, t), F32)))
    return tv, tp


def _peer_route_kernel(x_ref, wq_ref, keys_ref, e_ref, g_ref):
    t = x_ref.shape[0]
    q = jnp.dot(x_ref[...], wq_ref[...], preferred_element_type=F32).astype(BF16)
    half = q.shape[1] // 2
    tops = []
    for p in range(2):
        s_t = lax.dot_general(keys_ref[0, p], q[:, p * half:(p + 1) * half],
                              (((1,), (1,)), ((), ())), preferred_element_type=F32)
        tops.append(_topk_rows(s_t, None, P_TOPK))
    (s0, i0), (s1, i1) = tops
    cand, cid = [], []
    for a in range(P_TOPK):
        n_b = P_TOPK // (a + 1)
        rows = -(-n_b // V7X_SUBLANES) * V7X_SUBLANES
        keep = lax.broadcasted_iota(jnp.int32, (rows, t), 0) < n_b
        cand.append(jnp.where(keep, jnp.broadcast_to(s0[a:a + 1], (rows, t)) + s1[:rows], -jnp.inf))
        cid.append(jnp.broadcast_to(i0[a:a + 1], (rows, t)) * float(P_KEYS) + i1[:rows])
    best_s, best_e = _topk_rows(jnp.concatenate(cand, axis=0), jnp.concatenate(cid, axis=0), P_TOPK)
    ex = jnp.exp(best_s - jnp.max(best_s, axis=0, keepdims=True))
    g_ref[0] = ex / jnp.sum(ex, axis=0, keepdims=True)
    e_ref[0] = best_e.astype(jnp.int32)


def _peer_route_call(xn, wq, keys, tt=512):
    n_tok, d = xn.shape
    heads = keys.shape[0]
    qd = wq.shape[1] // heads
    tt = _pick_tile(n_tok, tt)
    return pl.pallas_call(
        _peer_route_kernel,
        grid=(n_tok // tt, heads),
        in_specs=[pl.BlockSpec((tt, d), lambda i, h: (i, 0)),
                  pl.BlockSpec((d, qd), lambda i, h: (0, h)),
                  pl.BlockSpec((1,) + keys.shape[1:], lambda i, h: (h, 0, 0, 0))],
        out_specs=(pl.BlockSpec((1, P_TOPK, tt), lambda i, h: (h, 0, i)),
                   pl.BlockSpec((1, P_TOPK, tt), lambda i, h: (h, 0, i))),
        out_shape=(jax.ShapeDtypeStruct((heads, P_TOPK, n_tok), jnp.int32),
                   jax.ShapeDtypeStruct((heads, P_TOPK, n_tok), F32)),
        compiler_params=_cparams("parallel", "parallel"),
        name="peer_route",
    )(xn, wq, keys)


_PEER_GROUP = 16
_PEER_STAGE_PITCH = P_KEYS + V7X_SUBLANES


def _peer_dense_kernel(x_ref, e_ref, g_ref, u_ref, v_ref, o_ref, gmat_ref, stage_ref, w_ref, *, eb, n_blocks):
    t = x_ref.shape[0]
    j = pl.program_id(1)
    cur = j % 2

    @pl.when(j == 0)
    def _build_gate_matrix():
        o_ref[...] = jnp.zeros_like(o_ref)
        w_ref[1] = jnp.zeros(w_ref.shape[1:], w_ref.dtype)
        sub = lax.broadcasted_iota(jnp.int32, (P_KEYS, e_ref.shape[1]), 0)

        def per_group(grp, carry):
            t0 = pl.multiple_of(grp * _PEER_GROUP, _PEER_GROUP)
            for s in range(_PEER_GROUP):
                e = e_ref[pl.ds(t0 + s, 1), :]
                g = g_ref[pl.ds(t0 + s, 1), :]
                hit_a = sub == (e >> 7)
                hit_b = sub == (e & (P_KEYS - 1))
                a_mat = jnp.where(hit_a, g, 0.0).astype(BF16)
                b_mat = jnp.where(hit_b, 1.0, 0.0).astype(BF16)
                stage_ref[s * _PEER_STAGE_PITCH:s * _PEER_STAGE_PITCH + P_KEYS, :] = _mm_nt(a_mat, b_mat)
            for i1 in range(P_KEYS):
                gmat_ref[pl.ds(t0, _PEER_GROUP), i1 * P_KEYS:(i1 + 1) * P_KEYS] = (
                    stage_ref[pl.ds(i1, _PEER_GROUP, stride=_PEER_STAGE_PITCH), :].astype(BF16))
            return carry

        lax.fori_loop(0, t // _PEER_GROUP, per_group, 0)

    o_ref[...] += jnp.dot(w_ref[1 - cur], v_ref[...], preferred_element_type=F32)
    blk = jnp.minimum(j, n_blocks - 1)
    gsel = gmat_ref[:, pl.ds(pl.multiple_of(blk * eb, eb), eb)]
    w_ref[cur] = (gsel.astype(F32) * jax.nn.gelu(_mm_nt(x_ref[...], u_ref[...]))).astype(BF16)


def _peer_dense_call(xn, e_nat, g_nat, u_tab, v_tab, tt=512, eb=256, layer=None):
    n_tok, d = xn.shape
    n_exp = u_tab.shape[-2]
    r = e_nat.shape[1]
    tt = _pick_tile(n_tok, tt)
    n_blocks = n_exp // eb
    return pl.pallas_call(
        functools.partial(_peer_dense_kernel, eb=eb, n_blocks=n_blocks),
        grid=(n_tok // tt, n_blocks + 1),
        in_specs=[pl.BlockSpec((tt, d), lambda i, j: (i, 0)),
                  pl.BlockSpec((tt, r), lambda i, j: (i, 0)),
                  pl.BlockSpec((tt, r), lambda i, j: (i, 0)),
                  _layer_block((eb, d), lambda i, j: (jnp.minimum(j, n_blocks - 1), 0), u_tab, layer),
                  _layer_block((eb, d), lambda i, j: (jnp.maximum(j - 1, 0), 0), v_tab, layer)],
        out_specs=pl.BlockSpec((tt, d), lambda i, j: (i, 0)),
        out_shape=jax.ShapeDtypeStruct((n_tok, d), F32),
        scratch_shapes=[pltpu.VMEM((tt, n_exp), BF16), pltpu.VMEM((_PEER_GROUP * _PEER_STAGE_PITCH, P_KEYS), F32),
                        pltpu.VMEM((2, tt, eb), BF16)],
        compiler_params=_cparams("parallel", "arbitrary"),
        name="peer_dense",
    )(xn, e_nat, g_nat, u_tab, v_tab)


def _peer(xn, wq, keys, u_tab, v_tab, layer=None):
    e_t, g_t = _peer_route_call(xn, wq, keys)
    heads = keys.shape[0]
    n_tok = xn.shape[0]
    e_nat = e_t.reshape(heads * P_TOPK, n_tok).T
    g_nat = g_t.reshape(heads * P_TOPK, n_tok).T
    return _peer_dense_call(xn, e_nat, g_nat, u_tab, v_tab, layer=layer)


def _split_bf16(x):
    hi = x.astype(BF16)
    lo = (x - hi.astype(F32)).astype(BF16)
    return hi, lo


def _mm(a, b):
    if a.ndim == 3:
        return lax.dot_general(a, b, (((2,), (1,)), ((0,), (0,))), preferred_element_type=F32)
    return jnp.dot(a, b, preferred_element_type=F32)


def _mm_nt(a, b):
    if a.ndim == 3:
        return lax.dot_general(a, b, (((2,), (2,)), ((0,), (0,))), preferred_element_type=F32)
    return lax.dot_general(a, b, (((1,), (1,)), ((), ())), preferred_element_type=F32)


def _dot3(a, b):
    ah, al = _split_bf16(a)
    bh, bl = _split_bf16(b)
    return _mm(ah, bh) + _mm(ah, bl) + _mm(al, bh)


_CONV_FILL_ROWS = 512


def _conv_pad_rows(cols):
    return -(-(cols + 1) // V7X_SUBLANES) * V7X_SUBLANES


def _conv_fill(x_ref, x0_s, xm_s, xq_s, *, l, cols, pad):
    tc = x0_s.shape[1]
    zeros = jnp.zeros((pad, tc), F32)
    for s in (x0_s, xm_s, xq_s):
        s[0:pad, :] = zeros
        s[pad + l:pad + l + pad, :] = zeros
    step = min(_CONV_FILL_ROWS, l)
    for s0 in range(0, l, step):
        x0_s[pad + s0:pad + s0 + step, :] = x_ref[s0:s0 + step, :]
    col = lax.broadcasted_iota(jnp.int32, (step, tc), 0) & (cols - 1)
    for s0 in range(0, l, step):
        xm_s[pad + s0:pad + s0 + step, :] = jnp.where(col >= 1, x0_s[pad + s0 - 1:pad + s0 - 1 + step, :], 0.0)
        xq_s[pad + s0:pad + s0 + step, :] = jnp.where(col <= cols - 2, x0_s[pad + s0 + 1:pad + s0 + 1 + step, :], 0.0)


def _conv_rows(x0_s, xm_s, xq_s, w_ref, t0, n, *, rows, cols, pad):
    acc = None
    for dr in ((-1, 0, 1) if rows > 1 else (0,)):
        base = pl.multiple_of(t0 + pad + dr * cols, V7X_SUBLANES)
        for dc, src in ((-1, xm_s), (0, x0_s), (1, xq_s)):
            tap = (dr + 1) * 3 + dc + 1
            term = src[pl.ds(base, n), :] * w_ref[tap:tap + 1, :]
            acc = term if acc is None else acc + term
    return acc


_CONV_CHUNK = 128


def _conv_a_kernel(z_ref, w_ref, o_ref, x0_s, xm_s, xq_s, *, l, rows, cols, pad, blocks_per_part):
    _conv_fill(z_ref.at[0], x0_s, xm_s, xq_s, l=l, cols=cols, pad=pad)
    part = pl.program_id(1) // blocks_per_part
    use_norm = part != 1
    post = jnp.where(part == 2, A_HEAD_DIM ** -0.5, 1.0)
    tc = o_ref.shape[2]

    def chunk(i, carry):
        t0 = pl.multiple_of(i * _CONV_CHUNK, _CONV_CHUNK)
        y = _conv_rows(x0_s, xm_s, xq_s, w_ref, t0, _CONV_CHUNK, rows=rows, cols=cols, pad=pad)
        y = y * jax.nn.sigmoid(y)
        outs = []
        for h0 in range(0, tc, A_HEAD_DIM):
            yh = y[:, h0:h0 + A_HEAD_DIM]
            inv = lax.rsqrt(jnp.sum(yh * yh, axis=-1, keepdims=True) + RMS_EPS) * post
            outs.append(yh * jnp.where(use_norm, inv, 1.0))
        o_ref[0, pl.ds(t0, _CONV_CHUNK), :] = jnp.concatenate(outs, axis=1) if len(outs) > 1 else outs[0]
        return carry

    lax.fori_loop(0, l // _CONV_CHUNK, chunk, 0, unroll=2)


def _conv_a_call(z_kvq, w9, rows, cols, tc=256):
    b, l, _ = z_kvq.shape
    ch = w9.shape[1]
    pad = _conv_pad_rows(cols)
    scr = pltpu.VMEM((l + 2 * pad, tc), F32)
    return pl.pallas_call(
        functools.partial(_conv_a_kernel, l=l, rows=rows, cols=cols, pad=pad, blocks_per_part=ch // 3 // tc),
        grid=(b, ch // tc),
        in_specs=[pl.BlockSpec((1, l, tc), lambda i, j: (i, 0, j)),
                  pl.BlockSpec((9, tc), lambda i, j: (0, j))],
        out_specs=pl.BlockSpec((1, l, tc), lambda i, j: (i, 0, j)),
        out_shape=jax.ShapeDtypeStruct((b, l, ch), F32),
        scratch_shapes=[scr, scr, scr],
        compiler_params=_cparams("parallel", "parallel"),
        name="conv_a",
    )(z_kvq, w9)


def _conv_c_kernel(z0_ref, z1_ref, z2_ref, w0_ref, w1_ref, w2_ref, x0_ref, u_ref, *scr, l, rows, cols, pad):
    parts = ((z0_ref, w0_ref, scr[0:3]), (z1_ref, w1_ref, scr[3:6]), (z2_ref, w2_ref, scr[6:9]))
    for z_ref, _, s in parts:
        _conv_fill(z_ref.at[0], *s, l=l, cols=cols, pad=pad)

    def chunk(i, carry):
        t0 = pl.multiple_of(i * _CONV_CHUNK, _CONV_CHUNK)
        y = [_conv_rows(*s, w_ref, t0, _CONV_CHUNK, rows=rows, cols=cols, pad=pad) for _, w_ref, s in parts]
        x0_ref[0, pl.ds(t0, _CONV_CHUNK), :] = y[0]
        u_ref[0, pl.ds(t0, _CONV_CHUNK), :] = y[1] * y[2]
        return carry

    lax.fori_loop(0, l // _CONV_CHUNK, chunk, 0, unroll=2)


def _conv_c_call(z_c, w9, rows, cols, col0=0, tc=128):
    b, l, _ = z_c.shape
    cw = w9.shape[1] // 3
    nb = cw // tc
    c0 = col0 // tc
    pad = _conv_pad_rows(cols)
    scr = pltpu.VMEM((l + 2 * pad, tc), F32)
    zspec = [pl.BlockSpec((1, l, tc), functools.partial(lambda i, j, p: (i, 0, c0 + j + p * nb), p=p)) for p in range(3)]
    wspec = [pl.BlockSpec((9, tc), functools.partial(lambda i, j, p: (0, j + p * nb), p=p)) for p in range(3)]
    ospec = pl.BlockSpec((1, l, tc), lambda i, j: (i, 0, j))
    return pl.pallas_call(
        functools.partial(_conv_c_kernel, l=l, rows=rows, cols=cols, pad=pad),
        grid=(b, nb),
        in_specs=zspec + wspec,
        out_specs=(ospec, ospec),
        out_shape=(jax.ShapeDtypeStruct((b, l, cw), F32), jax.ShapeDtypeStruct((b, l, cw), F32)),
        scratch_shapes=[scr] * 9,
        compiler_params=_cparams("parallel", "parallel"),
        name="conv_c",
    )(z_c, z_c, z_c, w9, w9, w9)


def _gate_prep_kernel(z_ref, alog_ref, dtb_ref, o_ref, *, l, heads):
    lanes = z_ref.shape[2]
    ii = lax.broadcasted_iota(jnp.int32, (A_CHUNK, A_CHUNK), 0)
    jj = lax.broadcasted_iota(jnp.int32, (A_CHUNK, A_CHUNK), 1)
    lower = jnp.where(ii >= jj, 1.0, 0.0)
    upper = jnp.where(ii <= jj, 1.0, 0.0)
    lane = lax.broadcasted_iota(jnp.int32, (A_CHUNK, lanes), 1)

    def chunk(n, carry):
        r0 = pl.multiple_of(n * A_CHUNK, A_CHUNK)
        z = z_ref[0, pl.ds(r0, A_CHUNK), :]
        beta = jax.nn.sigmoid(z)
        x = z + dtb_ref[...]
        softplus = jnp.maximum(x, 0.0) + jnp.log(1.0 + jnp.exp(-jnp.abs(x)))
        la = -jnp.exp(alog_ref[...]) * softplus
        pre = _dot3(lower, la)
        suf = _dot3(upper, la)
        o_ref[0, pl.ds(r0, A_CHUNK), :] = jnp.where(lane < 2 * heads, beta, jnp.where(lane < 3 * heads, pre, suf))
        return carry

    lax.fori_loop(0, l // A_CHUNK, chunk, 0)


def _gate_prep_call(z_ba, a_log, a_dt_bias):
    b, l, lanes = z_ba.shape
    heads = a_log.shape[1]
    pad = lambda p: jnp.pad(p.reshape(1, 2 * heads).astype(F32), ((0, 0), (2 * heads, lanes - 4 * heads)))
    return pl.pallas_call(
        functools.partial(_gate_prep_kernel, l=l, heads=heads),
        grid=(b,),
        in_specs=[pl.BlockSpec((1, l, lanes), lambda i: (i, 0, 0)),
                  pl.BlockSpec((1, lanes), lambda i: (0, 0)),
                  pl.BlockSpec((1, lanes), lambda i: (0, 0))],
        out_specs=pl.BlockSpec((1, l, lanes), lambda i: (i, 0, 0)),
        out_shape=jax.ShapeDtypeStruct((b, l, lanes), F32),
        compiler_params=_cparams("parallel"),
        name="gate_prep",
    )(z_ba, pad(a_log), pad(a_dt_bias))


_TRI_BASE_LOG2 = 3


def _dot1(a, b):
    return _mm(a.astype(BF16), b.astype(BF16))


def _unit_tri_inverse(a, dot):
    c = a.shape[-1]
    ii = lax.broadcasted_iota(jnp.int32, a.shape, a.ndim - 2)
    jj = lax.broadcasted_iota(jnp.int32, a.shape, a.ndim - 1)
    eye = jnp.where(ii == jj, 1.0, 0.0)
    p = -jnp.where((ii >> _TRI_BASE_LOG2) == (jj >> _TRI_BASE_LOG2), a, 0.0)
    t = eye + p
    span = 2
    while span < (1 << _TRI_BASE_LOG2):
        p = dot(p, p)
        t = t + dot(t, p)
        span *= 2
    log2 = _TRI_BASE_LOG2
    while (1 << log2) < c:
        pair = jnp.where((ii >> (log2 + 1)) == (jj >> (log2 + 1)), a, 0.0)
        cross = jnp.where((ii >> log2) == (jj >> log2), 0.0, pair)
        t = t - dot(t, dot(cross, t))
        log2 += 1
    return t


def _unit_tri_solve(a, rhs):
    t = _unit_tri_inverse(a, _dot1).astype(BF16)
    x0 = _mm(t, rhs.astype(BF16))
    resid = rhs - x0 - _dot3(a, x0)
    return x0 + _mm(t, resid.astype(BF16))


def _mm_tn(a, b):
    return lax.dot_general(a, b, (((1,), (1,)), ((0,), (0,))), preferred_element_type=F32)


_DELTA_GROUP = 16


def _delta_kernel(k_ref, v_ref, q_ref, col_ref, row_ref, s0_ref, og_ref, an_ref, o_ref, sfin_ref,
                  km_s, nm_s, p_s, r_s, eg_s, sall_s, *, l):
    c = A_CHUNK
    dk = A_HEAD_DIM
    n_chunks = l // c
    g = min(_DELTA_GROUP, n_chunks)
    shape = (2 * g, c, c)
    ii = lax.broadcasted_iota(jnp.int32, shape, 1)
    jj = lax.broadcasted_iota(jnp.int32, shape, 2)
    lag = jnp.where(lax.broadcasted_iota(jnp.int32, shape, 0) >= g, jj - ii, ii - jj)
    incl = lag >= 0
    strict = lag > 0
    two = lambda x: jnp.concatenate([x, x], axis=0)

    def pass1(gi, carry):
        rows_blk = pl.ds(pl.multiple_of(gi * (g * c), g * c), g * c)
        chunks = pl.ds(gi * g, g)
        kc = k_ref[0, rows_blk, :].reshape(g, c, dk)
        vc = v_ref[0, rows_blk, :].reshape(g, c, dk)
        qc = q_ref[0, rows_blk, :].reshape(g, c, dk)
        cols = col_ref[0, 0, rows_blk, :].reshape(g, c, 8)
        rws = row_ref[0, 0, chunks]
        beta = jnp.concatenate([cols[:, :, 0:1], cols[:, :, 1:2]], axis=0)
        gcol = jnp.concatenate([cols[:, :, 2:3], cols[:, :, 3:4]], axis=0)
        grow = jnp.concatenate([rws[:, 0:1, :], rws[:, 1:2, :]], axis=0)
        dec = jnp.where(incl, jnp.exp(jnp.where(incl, gcol - grow, 0.0)), 0.0)
        kb = kc.astype(BF16)
        kk = two(_mm_nt(kb, kb))
        qk = two(_mm_nt(qc.astype(BF16), kb))
        eg = jnp.exp(gcol)
        k2, v2, q2 = two(kc), two(vc), two(qc)
        sol = _unit_tri_solve(jnp.where(strict, kk * dec * beta, 0.0),
                              jnp.concatenate([v2 * beta, k2 * (beta * eg)], axis=2))
        ub = sol[:, :, :dk].astype(BF16)
        wb = sol[:, :, dk:].astype(BF16)
        glast = jnp.concatenate([gcol[:g, c - 1:c, :], gcol[g:, 0:1, :]], axis=0)
        ke = (k2 * jnp.exp(glast - gcol)).astype(BF16)
        qkd = (qk * dec).astype(BF16)
        km = _mm_tn(ke, wb)
        nm = _mm_tn(ke, ub)
        pm = q2 * eg - _mm(qkd, wb)
        rm = _mm(qkd, ub)
        eglast = jnp.exp(glast)
        for d in range(2):
            sl = slice(d * g, (d + 1) * g)
            km_s[d, chunks] = km[sl].astype(BF16)
            nm_s[d, chunks] = nm[sl]
            p_s[d, rows_blk, :] = pm[sl].reshape(g * c, dk).astype(BF16)
            r_s[d, rows_blk, :] = rm[sl].reshape(g * c, dk)
            eg_s[d, chunks] = jnp.broadcast_to(eglast[sl], (g, V7X_SUBLANES, dk))
        return carry

    lax.fori_loop(0, n_chunks // g, pass1, 0)

    def pass2(n, states):
        new_states = []
        for d, ch in ((0, n), (1, n_chunks - 1 - n)):
            s = states[d]
            sb = s.astype(BF16)
            sall_s[d, ch] = sb
            new_states.append(s * eg_s[d, ch][0:1, :] - jnp.dot(km_s[d, ch], sb, preferred_element_type=F32)
                              + nm_s[d, ch])
        return tuple(new_states)

    s_f, s_b = lax.fori_loop(0, n_chunks, pass2, (s0_ref[0, 0, 0], s0_ref[0, 0, 1]))
    sfin_ref[0, 0, 0] = s_f
    sfin_ref[0, 0, 1] = s_b

    def pass3(gi, carry):
        rows_blk = pl.ds(pl.multiple_of(gi * (g * c), g * c), g * c)
        chunks = pl.ds(gi * g, g)
        o = None
        for d in range(2):
            term = _mm(p_s[d, rows_blk, :].reshape(g, c, dk), sall_s[d, chunks]) + r_s[d, rows_blk, :].reshape(g, c, dk)
            o = term if o is None else o + term
        o = o.reshape(g * c, dk)
        o = o * lax.rsqrt(jnp.mean(o * o, axis=-1, keepdims=True) + RMS_EPS) * an_ref[...]
        og = og_ref[0, rows_blk, :]
        o_ref[0, rows_blk, :] = (o * (og * jax.nn.sigmoid(og))).astype(o_ref.dtype)
        return carry

    lax.fori_loop(0, n_chunks // g, pass3, 0)


def _delta_call(act, z_og, gates, s0, a_norm, og_col=0):
    b, l, ch = act.shape
    heads = ch // 3 // A_HEAD_DIM
    dk = A_HEAD_DIM
    c = A_CHUNK
    n_chunks = l // c
    g4 = jnp.stack([gates[..., i * heads:(i + 1) * heads] for i in range(4)], axis=-1)
    col = jnp.pad(jnp.transpose(g4, (0, 2, 1, 3)), ((0, 0), (0, 0), (0, 0), (0, 4)))
    row = jnp.transpose(g4[..., 2:4].reshape(b, n_chunks, c, heads, 2), (0, 3, 1, 4, 2))
    row = jnp.pad(row, ((0, 0), (0, 0), (0, 0), (0, 6), (0, 0)))
    tok = lambda off: pl.BlockSpec((1, l, dk), functools.partial(lambda i, h, off: (i, 0, h + off), off=off))
    return pl.pallas_call(
        functools.partial(_delta_kernel, l=l),
        grid=(b, heads),
        in_specs=[tok(0), tok(heads), tok(2 * heads),
                  pl.BlockSpec((1, 1, l, 8), lambda i, h: (i, h, 0, 0)),
                  pl.BlockSpec((1, 1, n_chunks, 8, c), lambda i, h: (i, h, 0, 0, 0)),
                  pl.BlockSpec((1, 1, 2, dk, dk), lambda i, h: (i, h, 0, 0, 0)),
                  tok(og_col // dk),
                  pl.BlockSpec((1, dk), lambda i, h: (0, 0))],
        out_specs=(tok(0), pl.BlockSpec((1, 1, 2, dk, dk), lambda i, h: (i, h, 0, 0, 0))),
        out_shape=(jax.ShapeDtypeStruct((b, l, heads * dk), BF16),
                   jax.ShapeDtypeStruct((b, heads, 2, dk, dk), F32)),
        scratch_shapes=[pltpu.VMEM((2, n_chunks, dk, dk), BF16), pltpu.VMEM((2, n_chunks, dk, dk), F32),
                        pltpu.VMEM((2, l, dk), BF16), pltpu.VMEM((2, l, dk), F32),
                        pltpu.VMEM((2, n_chunks, V7X_SUBLANES, dk), F32),
                        pltpu.VMEM((2, n_chunks, dk, dk), BF16)],
        compiler_params=_cparams("parallel", "parallel"),
        name="delta_rule",
    )(act, act, act, col, row, s0, z_og, a_norm.reshape(1, dk).astype(F32))


def _gmlp_kernel(zu_ref, zv_ref, ws_ref, bs_ref, o_ref):
    groups = ws_ref.shape[0]
    gd = zu_ref.shape[1] // groups
    u = jax.nn.gelu(zu_ref[...])
    v = jax.nn.gelu(zv_ref[...])
    mu = jnp.mean(v, axis=-1, keepdims=True)
    var = jnp.mean(jnp.square(v - mu), axis=-1, keepdims=True)
    vn = ((v - mu) * lax.rsqrt(var + RMS_EPS)).astype(BF16)
    mixed = [jnp.dot(ws_ref[g], vn[:, g * gd:(g + 1) * gd], preferred_element_type=F32) + bs_ref[:, g:g + 1]
             for g in range(groups)]
    o_ref[...] = (u * jnp.concatenate(mixed, axis=1)).astype(o_ref.dtype)


def _gmlp_call(z_b, w_s, b_s, bw, col0=0):
    n_tok = z_b.shape[0]
    groups, p, _ = w_s.shape
    c0 = col0 // bw
    return pl.pallas_call(
        _gmlp_kernel,
        grid=(n_tok // p,),
        in_specs=[pl.BlockSpec((p, bw), lambda i: (i, c0)),
                  pl.BlockSpec((p, bw), lambda i: (i, c0 + 1)),
                  pl.BlockSpec((groups, p, p), lambda i: (0, 0, 0)),
                  pl.BlockSpec((p, groups), lambda i: (0, 0))],
        out_specs=pl.BlockSpec((p, bw), lambda i: (i, 0)),
        out_shape=jax.ShapeDtypeStruct((n_tok, bw), BF16),
        compiler_params=_cparams("parallel"),
        name="gmlp",
    )(z_b, z_b, w_s, b_s.T.astype(F32))


_FEAT_PAD = 64


def _filter_feats(l):
    pos = np.arange(l, dtype=np.float32)
    t = np.linspace(0.0, 1.0, l, dtype=np.float32)[:, None]
    bands = np.linspace(1e-4, C_POS_BANDS - 1, C_POS_BANDS, dtype=np.float32)
    ang = np.float32(2.0 * math.pi / l) * pos[:, None] * bands
    feats = np.concatenate([t, np.cos(ang), -np.sin(ang)], axis=-1).astype(np.float32)
    return np.pad(feats, ((0, 0), (0, _FEAT_PAD - feats.shape[1])))


def _filter_kernel(feat_ref, w1_ref, b1_ref, f1_ref, w2_ref, b2_ref, f2_ref, w3_ref, b3_ref, rate_ref, o_ref):
    feats = feat_ref[...]
    hid = jnp.sin(f1_ref[...] * (_dot3(feats, w1_ref[...]) + b1_ref[...]))
    hid = jnp.sin(f2_ref[...] * (_dot3(hid, w2_ref[...]) + b2_ref[...]))
    filt = _dot3(hid, w3_ref[...]) + b3_ref[...]
    filt = filt * jnp.exp(-feats[:, 0:1] * rate_ref[...])
    o_ref[...] = filt * lax.rsqrt(jnp.sum(filt * filt, axis=0, keepdims=True) + RMS_EPS)


def _filter_call(l, fw1, fb1, freq1, fw2, fb2, freq2, fw3, fb3, tc=256):
    hidden = fw1.shape[1]
    two_c = fw3.shape[1]
    cw = two_c // 2
    feats = jnp.asarray(_filter_feats(l))
    w1 = jnp.pad(fw1.astype(F32), ((0, _FEAT_PAD - fw1.shape[0]), (0, 0)))
    rate = np.abs(np.linspace(C_MIN_DECAY, C_MAX_DECAY, cw, dtype=np.float32))
    rate = jnp.asarray(np.concatenate([rate, rate]).reshape(1, two_c))
    row = lambda v: v.reshape(1, -1).astype(F32)
    full = lambda shape: pl.BlockSpec(shape, lambda j: (0, 0))
    return pl.pallas_call(
        _filter_kernel,
        grid=(two_c // tc,),
        in_specs=[full((l, _FEAT_PAD)), full((_FEAT_PAD, hidden)), full((1, hidden)), full((1, hidden)),
                  full((hidden, hidden)), full((1, hidden)), full((1, hidden)),
                  pl.BlockSpec((hidden, tc), lambda j: (0, j)), pl.BlockSpec((1, tc), lambda j: (0, j)),
                  pl.BlockSpec((1, tc), lambda j: (0, j))],
        out_specs=pl.BlockSpec((l, tc), lambda j: (0, j)),
        out_shape=jax.ShapeDtypeStruct((l, two_c), F32),
        compiler_params=_cparams("parallel"),
        name="hyena_filter",
    )(feats, w1, row(fb1), row(freq1), fw2.astype(F32), row(fb2), row(freq2), fw3.astype(F32), row(fb3), rate)


def _dft_sizes(l):
    n1 = 64 if l >= 4096 else 32
    return n1, (2 * l) // n1


def _hi_lo(x):
    x = jnp.asarray(x, F32)
    hi = x.astype(BF16)
    return hi, (x - hi.astype(F32)).astype(BF16)


def _dft_consts(l):
    n1, n2 = _dft_sizes(l)
    n = n1 * n2
    half = n1 // 2
    k1 = np.arange(n1)[None, :, None]
    m1 = np.arange(half)[None, None, :]
    m2 = np.arange(n2)[:, None, None]
    f1 = np.exp(-2j * np.pi * (m1 * k1 / n1 + m2 * k1 / n))
    fwd = np.concatenate([f1.real, f1.imag], axis=1)
    fh, fl = _hi_lo(fwd)
    fwd1 = jnp.concatenate([fh, fh, fl], axis=2)
    rt = np.transpose(f1.real, (0, 2, 1)) / n
    it = np.transpose(f1.imag, (0, 2, 1)) / n
    (rh, rl), (ih, il) = _hi_lo(rt), _hi_lo(it)
    inv1 = jnp.concatenate([rh, ih, rh, ih, rl, il], axis=2)
    kk = np.arange(n2)
    f2 = np.exp(-2j * np.pi * np.outer(kk, kk) / n2)
    m = np.block([[f2.real, -f2.imag], [f2.imag, f2.real]])
    mi = np.block([[f2.real, f2.imag], [-f2.imag, f2.real]])
    (mh, ml), (mih, mil) = _hi_lo(m), _hi_lo(mi)
    return fwd1, jnp.concatenate([mh, mh, ml], axis=1), jnp.concatenate([mih, mih, mil], axis=1), inv1


_DFT_UNROLL = 4


def _stack3(x):
    hi, lo = _split_bf16(x)
    return jnp.concatenate([hi, lo, hi], axis=0)


def _dft_pitch(n2):
    return n2 + V7X_SUBLANES


def _dft_level1(x_ref, fwd1_ref, ar_s, ai_s, n1, n2):
    half = n1 // 2
    pitch = _dft_pitch(n2)

    def body(j, carry):
        xs = x_ref[pl.ds(j, half, stride=n2), :]
        a = jnp.dot(fwd1_ref[j], _stack3(xs), preferred_element_type=F32)
        ar_s[pl.ds(j, n1, stride=pitch), :] = a[:n1]
        ai_s[pl.ds(j, n1, stride=pitch), :] = a[n1:]
        return carry

    lax.fori_loop(0, n2, body, 0, unroll=_DFT_UNROLL)


def _block_rows(k1, n2):
    return pl.ds(pl.multiple_of(k1 * _dft_pitch(n2), V7X_SUBLANES), n2)


def _pair_rows(pair, n2):
    return pl.ds(pl.multiple_of(pair * (2 * n2), 2 * n2), 2 * n2)


def _side_by_side(x, n2):
    return jnp.concatenate([x[:n2], x[n2:]], axis=1)


def _dft_level2(ar_s, ai_s, m3_ref, pair, n2):
    lo, hi = _block_rows(2 * pair, n2), _block_rows(2 * pair + 1, n2)
    blk = jnp.concatenate([jnp.concatenate([ar_s[lo, :], ar_s[hi, :]], axis=1),
                           jnp.concatenate([ai_s[lo, :], ai_s[hi, :]], axis=1)], axis=0)
    x = jnp.dot(m3_ref[...], _stack3(blk), preferred_element_type=F32)
    return x[:n2], x[n2:]


def _spectrum_kernel(hf_ref, hb_ref, fwd1_ref, m3_ref, o_ref, ar_s, ai_s, *, n1, n2):
    tc = o_ref.shape[2]
    for which, h_ref in enumerate((hf_ref, hb_ref)):
        _dft_level1(h_ref, fwd1_ref, ar_s, ai_s, n1, n2)

        def body(pair, carry):
            xr, xi = _dft_level2(ar_s, ai_s, m3_ref, pair, n2)
            for side in range(2):
                rows = pl.ds(pl.multiple_of((2 * pair + side) * n2, n2), n2)
                lanes = slice(side * tc, (side + 1) * tc)
                if which == 0:
                    o_ref[0, rows, :] = xr[:, lanes]
                    o_ref[1, rows, :] = xi[:, lanes]
                else:
                    o_ref[0, rows, :] += xr[:, lanes]
                    o_ref[1, rows, :] -= xi[:, lanes]
            return carry

        lax.fori_loop(0, n1 // 2, body, 0, unroll=_DFT_UNROLL // 2)


def _spectrum_call(filt, consts, tc=128):
    l, two_c = filt.shape
    cw = two_c // 2
    nb = cw // tc
    n1, n2 = _dft_sizes(l)
    fwd1, m3, _, _ = consts
    return pl.pallas_call(
        functools.partial(_spectrum_kernel, n1=n1, n2=n2),
        grid=(nb,),
        in_specs=[pl.BlockSpec((l, tc), lambda j: (0, j)),
                  pl.BlockSpec((l, tc), lambda j: (0, j + nb)),
                  pl.BlockSpec(fwd1.shape, lambda j: (0, 0, 0)),
                  pl.BlockSpec(m3.shape, lambda j: (0, 0))],
        out_specs=pl.BlockSpec((2, 2 * l, tc), lambda j: (0, 0, j)),
        out_shape=jax.ShapeDtypeStruct((2, 2 * l, cw), F32),
        scratch_shapes=[pltpu.VMEM((n1 * _dft_pitch(n2), tc), F32), pltpu.VMEM((n1 * _dft_pitch(n2), tc), F32)],
        compiler_params=_cparams("parallel"),
        name="hyena_spectrum",
    )(filt, filt, fwd1, m3)


def _longconv_kernel(u_ref, x0_ref, h_ref, skip_ref, fwd1_ref, m3_ref, mi3_ref, inv1_ref, o_ref,
                     ar_s, ai_s, y_s, *, l, n1, n2):
    half = n1 // 2
    _dft_level1(u_ref.at[0], fwd1_ref, ar_s, ai_s, n1, n2)

    tc = o_ref.shape[2]

    pitch = _dft_pitch(n2)

    def per_k1_pair(pair, carry):
        xr, xi = _dft_level2(ar_s, ai_s, m3_ref, pair, n2)
        rows = _pair_rows(pair, n2)
        hr = _side_by_side(h_ref[0, rows, :], n2)
        hi = _side_by_side(h_ref[1, rows, :], n2)
        y = jnp.concatenate([xr * hr - xi * hi, xr * hi + xi * hr], axis=0)
        b = jnp.dot(mi3_ref[...], _stack3(y), preferred_element_type=F32)
        for side in range(2):
            blk = _block_rows(2 * pair + side, n2)
            ar_s[blk, :] = b[:n2, side * tc:(side + 1) * tc]
            ai_s[blk, :] = b[n2:, side * tc:(side + 1) * tc]
        return carry

    lax.fori_loop(0, n1 // 2, per_k1_pair, 0, unroll=_DFT_UNROLL // 2)

    def per_n2(j, carry):
        br = ar_s[pl.ds(j, n1, stride=pitch), :]
        bi = ai_s[pl.ds(j, n1, stride=pitch), :]
        (brh, brl), (bih, bil) = _split_bf16(br), _split_bf16(bi)
        rhs = jnp.concatenate([brh, bih, brl, bil, brh, bih], axis=0)
        y_s[pl.ds(j, half, stride=pitch), :] = jnp.dot(inv1_ref[j], rhs, preferred_element_type=F32)
        return carry

    lax.fori_loop(0, n2, per_n2, 0, unroll=_DFT_UNROLL)

    def finish(i, carry):
        rows = pl.ds(pl.multiple_of(i * n2, n2), n2)
        u = u_ref[0, rows, :]
        y = y_s[_block_rows(i, n2), :]
        o_ref[0, rows, :] = (x0_ref[0, rows, :] * (y + skip_ref[...] * u)).astype(o_ref.dtype)
        return carry

    lax.fori_loop(0, half, finish, 0, unroll=_DFT_UNROLL)


def _longconv_call(u, x0, spec, skip, consts, tc=128):
    b, l, cw = u.shape
    n1, n2 = _dft_sizes(l)
    fwd1, m3, mi3, inv1 = consts
    tok = pl.BlockSpec((1, l, tc), lambda j, i: (i, 0, j))
    const = lambda a: pl.BlockSpec(a.shape, lambda j, i: (0,) * a.ndim)
    return pl.pallas_call(
        functools.partial(_longconv_kernel, l=l, n1=n1, n2=n2),
        grid=(cw // tc, b),
        in_specs=[tok, tok,
                  pl.BlockSpec((2, 2 * l, tc), lambda j, i: (0, 0, j)),
                  pl.BlockSpec((1, tc), lambda j, i: (0, j)),
                  const(fwd1), const(m3), const(mi3), const(inv1)],
        out_specs=tok,
        out_shape=jax.ShapeDtypeStruct((b, l, cw), BF16),
        scratch_shapes=[pltpu.VMEM((n1 * _dft_pitch(n2), tc), F32), pltpu.VMEM((n1 * _dft_pitch(n2), tc), F32),
                        pltpu.VMEM((n1 // 2 * _dft_pitch(n2), tc), F32)],
        compiler_params=_cparams("parallel", "parallel"),
        name="hyena_longconv",
    )(u, x0, spec, skip.reshape(1, cw).astype(F32), fwd1, m3, mi3, inv1)


def _weight_stacks(p):
    aw = p["w_br_a"].shape[1]
    heads = p["a_log"].shape[2]
    off_ba = 2 * aw
    off_q = off_ba + 4 * heads
    w_in = p["w_in"]
    cast = lambda w: w.astype(BF16)
    return {
        "w_main": cast(jnp.concatenate([w_in[:, :, :off_ba], w_in[:, :, off_q:]], axis=2)),
        "w_ba": cast(jnp.pad(w_in[:, :, off_ba:off_q], ((0, 0), (0, 0), (0, V7X_LANES - 4 * heads)))),
        "w_out": cast(p["w_out"]), "p_u": cast(p["p_u"]), "p_v": cast(p["p_v"]),
    }


def _layer_weights(i, p, stacks):
    aw, bw, cw = p["w_br_a"].shape[1], p["w_br_b"].shape[1], p["w_br_c"].shape[1]
    cast = lambda w: w.astype(BF16)
    col_og = 3 * aw
    col_b = col_og + aw
    col_c = col_b + 2 * bw
    col_gate = col_c + 3 * cw
    return {
        "layer": i, **stacks,
        "col_og": col_og, "col_b": col_b, "col_c": col_c, "col_gate": col_gate, "bw": bw,
        "w_br_a": cast(p["w_br_a"][i]), "w_br_b": cast(p["w_br_b"][i]), "w_br_c": cast(p["w_br_c"][i]),
        "a_conv": p["a_conv"][i].reshape(9, -1), "c_conv": p["c_conv"][i].reshape(9, -1),
        "a_log": p["a_log"][i], "a_dt_bias": p["a_dt_bias"][i], "a_norm": p["a_norm"][i],
        "b_ws": cast(p["b_ws"][i]), "b_bs": p["b_bs"][i], "c_skip": p["c_skip"][i],
        "filter": tuple(p[k][i] for k in ("c_fw1", "c_fb1", "c_freq1", "c_fw2", "c_fb2", "c_freq2", "c_fw3", "c_fb3")),
        "p_wq": cast(p["p_wq"][i]), "p_keys": cast(p["p_keys"][i]),
    }


def _delta_branch(xf, z, b, l, rows, cols, s0, lw):
    z3 = z.reshape(b, l, -1)
    z_ba = _matmul(xf, lw["w_ba"], layer=lw["layer"]).reshape(b, l, -1)
    act = _conv_a_call(z3, lw["a_conv"], rows, cols)
    gates = _gate_prep_call(z_ba, lw["a_log"], lw["a_dt_bias"])
    o_a, s_fin = _delta_call(act, z3, gates, s0, lw["a_norm"], og_col=lw["col_og"])
    return o_a.reshape(b * l, -1), s_fin


def _token_mixer(xn, rows, cols, s0, lw):
    b, l, d = xn.shape
    xf = xn.reshape(b * l, d)
    z = _matmul(xf, lw["w_main"], layer=lw["layer"])
    o_a, s_fin = _delta_branch(xf, z, b, l, rows, cols, s0, lw)
    o_b = _gmlp_call(z, lw["b_ws"], lw["b_bs"], lw["bw"], col0=lw["col_b"])
    x0, u = _conv_c_call(z.reshape(b, l, -1), lw["c_conv"], rows, cols, col0=lw["col_c"])
    consts = _dft_consts(l)
    spec = _spectrum_call(_filter_call(l, *lw["filter"]), consts)
    o_c = _longconv_call(u, x0, spec, lw["c_skip"], consts).reshape(b * l, -1)
    merged = _merge_call(o_a, o_b, o_c, lw["w_br_a"], lw["w_br_b"], lw["w_br_c"], z, lw["col_gate"])
    return _matmul(merged, lw["w_out"], layer=lw["layer"]).reshape(b, l, d), s_fin


def _peer_layer(xn, lw):
    b, l, d = xn.shape
    return _peer(xn.reshape(b * l, d), lw["p_wq"], lw["p_keys"], lw["p_u"], lw["p_v"],
                 layer=lw["layer"]).reshape(b, l, d)


def kernel(x, c, ctx, c_ctx, ada_w, ada_b, w_in, a_conv, a_log, a_dt_bias, a_norm, b_ws, b_bs, c_conv, c_fw1, c_fb1,
           c_freq1, c_fw2, c_fb2, c_freq2, c_fw3, c_fb3, c_skip, w_br_a, w_br_b, w_br_c, w_out, p_wq, p_keys, p_u,
           p_v, final_norm):
    params = dict(w_in=w_in, a_conv=a_conv, a_log=a_log, a_dt_bias=a_dt_bias, a_norm=a_norm, b_ws=b_ws, b_bs=b_bs,
                  c_conv=c_conv, c_fw1=c_fw1, c_fb1=c_fb1, c_freq1=c_freq1, c_fw2=c_fw2, c_fb2=c_fb2, c_freq2=c_freq2,
                  c_fw3=c_fw3, c_fb3=c_fb3, c_skip=c_skip, w_br_a=w_br_a, w_br_b=w_br_b, w_br_c=w_br_c, w_out=w_out,
                  p_wq=p_wq, p_keys=p_keys, p_u=p_u, p_v=p_v)
    stacks = _weight_stacks(params)
    b, l, d = x.shape
    depth = w_in.shape[0]
    rows = l // GRID_W
    l_ctx = ctx.shape[1]
    heads = a_log.shape[2]
    s_zero = jnp.zeros((b, heads, 2, A_HEAD_DIM, A_HEAD_DIM), F32)
    cvec = jnp.concatenate([c, c_ctx[None, :], jnp.zeros((V7X_SUBLANES - b - 1, d), F32)], axis=0)

    h, h_pending = x, None
    hc, hc_pending = ctx, None
    for i in range(depth):
        lw = _layer_weights(i, params, stacks)
        mod_all = _mod_call(cvec, ada_w, ada_b[i], i)
        mod = mod_all[:b].reshape(b, N_MOD, 1, d)
        mod_c = mod_all[b].reshape(N_MOD, 1, 1, d)

        def norm(stream, pending, scale, shift):
            if pending is None:
                return stream, _norm_mod_call(stream, scale, shift)[1]
            return _norm_mod_call(stream, scale, shift, delta=pending[0], gate=pending[1], emit_h=True)

        hc, xnc = norm(hc, hc_pending, mod_c[1], mod_c[0])
        if i == depth - 1:
            xcf = xnc.reshape(b * l_ctx, d)
            z_c = _matmul(xcf, lw["w_main"], ncols=lw["col_b"], layer=i)
            _, s_ctx = _delta_branch(xcf, z_c, b, l_ctx, 1, l_ctx, s_zero, lw)
        else:
            out_c, s_ctx = _token_mixer(xnc, 1, l_ctx, s_zero, lw)
            hc, xnc2 = norm(hc, (out_c, mod_c[2]), mod_c[4], mod_c[3])
            hc_pending = (_peer_layer(xnc2, lw), mod_c[5])
        h, xn = norm(h, h_pending, mod[:, 1], mod[:, 0])
        out, _ = _token_mixer(xn, rows, GRID_W, s_ctx, lw)
        h, xn2 = norm(h, (out, mod[:, 2]), mod[:, 4], mod[:, 3])
        h_pending = (_peer_layer(xn2, lw), mod[:, 5])
    return _norm_mod_call(h, final_norm.reshape(1, 1, d), delta=h_pending[0], gate=h_pending[1], out_dtype=F32)[1]
```

```python
import functools
import math

import numpy as np
import jax
import jax.numpy as jnp
from jax import lax
from jax.experimental import pallas as pl
from jax.experimental.pallas import tpu as pltpu

F32 = jnp.float32
BF16 = jnp.bfloat16

V7X_LANES = 128
V7X_SUBLANES = 8
V7X_VMEM_LIMIT_BYTES = 56 * 1024 * 1024

GRID_W = 64
RMS_EPS = 1e-6
N_MOD = 6
A_HEAD_DIM = 128
A_CHUNK = 64
B_CHUNK = 128
C_POS_BANDS = 16
C_MIN_DECAY = math.log(1e-2) / 1.5
C_MAX_DECAY = math.log(1e-2) / 0.3
N_BRANCH = 3
P_KEYS = 128
P_TOPK = 16


def _cparams(*sem):
    return pltpu.CompilerParams(dimension_semantics=sem, vmem_limit_bytes=V7X_VMEM_LIMIT_BYTES)


def _mod_kernel(c_ref, w_ref, b_ref, o_ref):
    c = c_ref[...]
    a = (c * jax.nn.sigmoid(c)).astype(BF16)
    o_ref[...] = jnp.dot(a, w_ref[...].astype(BF16), preferred_element_type=F32) + b_ref[...]


def _mod_call(cvec, ada_w, ada_b, layer):
    rows, d = cvec.shape
    n = ada_w.shape[-1]
    tn = 1024
    return pl.pallas_call(
        _mod_kernel,
        grid=(n // tn,),
        in_specs=[pl.BlockSpec((rows, d), lambda j: (0, 0)),
                  _layer_block((d, tn), lambda j: (0, j), ada_w, layer),
                  pl.BlockSpec((1, tn), lambda j: (0, j))],
        out_specs=pl.BlockSpec((rows, tn), lambda j: (0, j)),
        out_shape=jax.ShapeDtypeStruct((rows, n), F32),
        compiler_params=_cparams("parallel"),
        name="ada_mod",
    )(cvec, ada_w, ada_b.reshape(1, n))


def _norm_mod_kernel(*refs, with_delta, with_shift, emit_h):
    refs = list(refs)
    h = refs.pop(0)[0]
    if with_delta:
        d_ref, g_ref = refs.pop(0), refs.pop(0)
        h = h + g_ref[0] * d_ref[0]
    sc_ref = refs.pop(0)
    xn = h * lax.rsqrt(jnp.mean(h * h, axis=-1, keepdims=True) + RMS_EPS)
    if with_shift:
        xn = xn * (1.0 + sc_ref[0]) + refs.pop(0)[0]
    else:
        xn = xn * sc_ref[0]
    if emit_h:
        refs.pop(0)[0] = h
    refs.pop(0)[0] = xn.astype(refs[0].dtype)


def _norm_mod_call(h, scale, shift=None, delta=None, gate=None, emit_h=False, out_dtype=BF16, tl=256):
    b, l, d = h.shape
    tl = min(tl, l)
    tok = pl.BlockSpec((1, tl, d), lambda i, j: (i, j, 0))

    def vec(v):
        if v.shape[0] == 1:
            return pl.BlockSpec((1, 1, d), lambda i, j: (0, 0, 0))
        return pl.BlockSpec((1, 1, d), lambda i, j: (i, 0, 0))

    args, in_specs = [h], [tok]
    if delta is not None:
        args += [delta, gate]
        in_specs += [tok, vec(gate)]
    args.append(scale)
    in_specs.append(vec(scale))
    if shift is not None:
        args.append(shift)
        in_specs.append(vec(shift))
    out_shape, out_specs = [], []
    if emit_h:
        out_shape.append(jax.ShapeDtypeStruct((b, l, d), F32))
        out_specs.append(tok)
    out_shape.append(jax.ShapeDtypeStruct((b, l, d), out_dtype))
    out_specs.append(tok)
    res = pl.pallas_call(
        functools.partial(_norm_mod_kernel, with_delta=delta is not None, with_shift=shift is not None, emit_h=emit_h),
        grid=(b, l // tl),
        in_specs=in_specs, out_specs=tuple(out_specs), out_shape=tuple(out_shape),
        compiler_params=_cparams("parallel", "parallel"),
        name="norm_mod",
    )(*args)
    return (res[0], res[1]) if emit_h else (None, res[0])


def _mm_kernel(a_ref, w_ref, o_ref):
    o_ref[...] = jnp.dot(a_ref[...], w_ref[...], preferred_element_type=F32).astype(o_ref.dtype)


def _pick_tile(n, pref):
    t = min(pref, n)
    while n % t:
        t //= 2
    return t


def _layer_block(block, index_map, w, layer):
    if layer is None:
        return pl.BlockSpec(block, index_map)
    assert w.ndim == len(block) + 1
    return pl.BlockSpec((None,) + block, lambda *g: (layer,) + tuple(index_map(*g)))


def _matmul(a, w, out_dtype=F32, tm=512, tn=1024, ncols=None, layer=None):
    m, k = a.shape
    n = w.shape[-1] if ncols is None else ncols
    tm = _pick_tile(m, tm)
    tn = _pick_tile(n, tn)
    return pl.pallas_call(
        _mm_kernel,
        grid=(n // tn, m // tm),
        in_specs=[pl.BlockSpec((tm, k), lambda j, i: (i, 0)),
                  _layer_block((k, tn), lambda j, i: (0, j), w, layer)],
        out_specs=pl.BlockSpec((tm, tn), lambda j, i: (i, j)),
        out_shape=jax.ShapeDtypeStruct((m, n), out_dtype),
        compiler_params=_cparams("parallel", "parallel"),
        name="matmul",
    )(a, w)


def _merge_kernel(oa_ref, ob_ref, oc_ref, wa_ref, wb_ref, wc_ref, ga_ref, gb_ref, gc_ref, o_ref):
    acc = jax.nn.sigmoid(ga_ref[...]) * jnp.dot(oa_ref[...], wa_ref[...], preferred_element_type=F32)
    acc += jax.nn.sigmoid(gb_ref[...]) * jnp.dot(ob_ref[...], wb_ref[...], preferred_element_type=F32)
    acc += jax.nn.sigmoid(gc_ref[...]) * jnp.dot(oc_ref[...], wc_ref[...], preferred_element_type=F32)
    o_ref[...] = acc.astype(o_ref.dtype)


def _merge_call(o_a, o_b, o_c, w_a, w_b, w_c, z, gate_col, tm=512, tn=512):
    n_tok = o_a.shape[0]
    d = w_a.shape[1]
    tm = _pick_tile(n_tok, tm)
    tn = _pick_tile(math.gcd(d, gate_col) if gate_col else d, tn)
    nb = d // tn
    g0 = gate_col // tn

    def act(o):
        return pl.BlockSpec((tm, o.shape[1]), lambda j, i: (i, 0))

    def wgt(w):
        return pl.BlockSpec((w.shape[0], tn), lambda j, i: (0, j))

    def gate(br):
        return pl.BlockSpec((tm, tn), lambda j, i: (i, g0 + j + br * nb))

    return pl.pallas_call(
        _merge_kernel,
        grid=(nb, n_tok // tm),
        in_specs=[act(o_a), act(o_b), act(o_c), wgt(w_a), wgt(w_b), wgt(w_c), gate(0), gate(1), gate(2)],
        out_specs=pl.BlockSpec((tm, tn), lambda j, i: (i, j)),
        out_shape=jax.ShapeDtypeStruct((n_tok, d), BF16),
        compiler_params=_cparams("parallel", "parallel"),
        name="branch_merge",
    )(o_a, o_b, o_c, w_a, w_b, w_c, z, z, z)


def _topk_rows(s, payload, k):
    n, t = s.shape
    iota = lax.broadcasted_iota(jnp.int32, (n, t), 0).astype(F32)
    riota = lax.broadcasted_iota(jnp.int32, (k, t), 0)

    def body(r, carry):
        s, tv, tp = carry
        m = jnp.max(s, axis=0, keepdims=True)
        idx = jnp.min(jnp.where(s == m, iota, float(n)), axis=0, keepdims=True)
        sel = iota == idx
        p = idx if payload is None else jnp.max(jnp.where(sel, payload, -1.0), axis=0, keepdims=True)
        s = jnp.where(sel, -jnp.inf, s)
        tv = jnp.where(riota == r, m, tv)
        tp = jnp.where(riota == r, p, tp)
        return s, tv, tp

    _, tv, tp = lax.fori_loop(0, k, body, (s, jnp.zeros((k, t), F32), jnp.zeros((k, t), F32)))
    return tv, tp


def _peer_route_kernel(x_ref, wq_ref, keys_ref, e_ref, g_ref):
    t = x_ref.shape[0]
    q = jnp.dot(x_ref[...], wq_ref[...], preferred_element_type=F32).astype(BF16)
    half = q.shape[1] // 2
    tops = []
    for p in range(2):
        s_t = lax.dot_general(keys_ref[0, p], q[:, p * half:(p + 1) * half],
                              (((1,), (1,)), ((), ())), preferred_element_type=F32)
        tops.append(_topk_rows(s_t, None, P_TOPK))
    (s0, i0), (s1, i1) = tops
    cand, cid = [], []
    for a in range(P_TOPK):
        n_b = P_TOPK // (a + 1)
        rows = -(-n_b // V7X_SUBLANES) * V7X_SUBLANES
        keep = lax.broadcasted_iota(jnp.int32, (rows, t), 0) < n_b
        cand.append(jnp.where(keep, jnp.broadcast_to(s0[a:a + 1], (rows, t)) + s1[:rows], -jnp.inf))
        cid.append(jnp.broadcast_to(i0[a:a + 1], (rows, t)) * float(P_KEYS) + i1[:rows])
    best_s, best_e = _topk_rows(jnp.concatenate(cand, axis=0), jnp.concatenate(cid, axis=0), P_TOPK)
    ex = jnp.exp(best_s - jnp.max(best_s, axis=0, keepdims=True))
    g_ref[0] = ex / jnp.sum(ex, axis=0, keepdims=True)
    e_ref[0] = best_e.astype(jnp.int32)


def _peer_route_call(xn, wq, keys, tt=512):
    n_tok, d = xn.shape
    heads = keys.shape[0]
    qd = wq.shape[1] // heads
    tt = _pick_tile(n_tok, tt)
    return pl.pallas_call(
        _peer_route_kernel,
        grid=(n_tok // tt, heads),
        in_specs=[pl.BlockSpec((tt, d), lambda i, h: (i, 0)),
                  pl.BlockSpec((d, qd), lambda i, h: (0, h)),
                  pl.BlockSpec((1,) + keys.shape[1:], lambda i, h: (h, 0, 0, 0))],
        out_specs=(pl.BlockSpec((1, P_TOPK, tt), lambda i, h: (h, 0, i)),
                   pl.BlockSpec((1, P_TOPK, tt), lambda i, h: (h, 0, i))),
        out_shape=(jax.ShapeDtypeStruct((heads, P_TOPK, n_tok), jnp.int32),
                   jax.ShapeDtypeStruct((heads, P_TOPK, n_tok), F32)),
        compiler_params=_cparams("parallel", "parallel"),
        name="peer_route",
    )(xn, wq, keys)


_PEER_GROUP = 16
_PEER_STAGE_PITCH = P_KEYS + V7X_SUBLANES


def _peer_dense_kernel(x_ref, e_ref, g_ref, u_ref, v_ref, o_ref, gmat_ref, stage_ref, w_ref, *, eb, n_blocks):
    t = x_ref.shape[0]
    j = pl.program_id(1)
    cur = j % 2

    @pl.when(j == 0)
    def _build_gate_matrix():
        o_ref[...] = jnp.zeros_like(o_ref)
        w_ref[1] = jnp.zeros(w_ref.shape[1:], w_ref.dtype)
        sub = lax.broadcasted_iota(jnp.int32, (P_KEYS, e_ref.shape[1]), 0)

        def per_group(grp, carry):
            t0 = pl.multiple_of(grp * _PEER_GROUP, _PEER_GROUP)
            for s in range(_PEER_GROUP):
                e = e_ref[pl.ds(t0 + s, 1), :]
                g = g_ref[pl.ds(t0 + s, 1), :]
                hit_a = sub == (e >> 7)
                hit_b = sub == (e & (P_KEYS - 1))
                a_mat = jnp.where(hit_a, g, 0.0).astype(BF16)
                b_mat = jnp.where(hit_b, 1.0, 0.0).astype(BF16)
                stage_ref[s * _PEER_STAGE_PITCH:s * _PEER_STAGE_PITCH + P_KEYS, :] = _mm_nt(a_mat, b_mat)
            for i1 in range(P_KEYS):
                gmat_ref[pl.ds(t0, _PEER_GROUP), i1 * P_KEYS:(i1 + 1) * P_KEYS] = (
                    stage_ref[pl.ds(i1, _PEER_GROUP, stride=_PEER_STAGE_PITCH), :].astype(BF16))
            return carry

        lax.fori_loop(0, t // _PEER_GROUP, per_group, 0)

    o_ref[...] += jnp.dot(w_ref[1 - cur], v_ref[...], preferred_element_type=F32)
    blk = jnp.minimum(j, n_blocks - 1)
    gsel = gmat_ref[:, pl.ds(pl.multiple_of(blk * eb, eb), eb)]
    w_ref[cur] = (gsel.astype(F32) * jax.nn.gelu(_mm_nt(x_ref[...], u_ref[...]))).astype(BF16)


def _peer_dense_call(xn, e_nat, g_nat, u_tab, v_tab, tt=512, eb=256, layer=None):
    n_tok, d = xn.shape
    n_exp = u_tab.shape[-2]
    r = e_nat.shape[1]
    tt = _pick_tile(n_tok, tt)
    n_blocks = n_exp // eb
    return pl.pallas_call(
        functools.partial(_peer_dense_kernel, eb=eb, n_blocks=n_blocks),
        grid=(n_tok // tt, n_blocks + 1),
        in_specs=[pl.BlockSpec((tt, d), lambda i, j: (i, 0)),
                  pl.BlockSpec((tt, r), lambda i, j: (i, 0)),
                  pl.BlockSpec((tt, r), lambda i, j: (i, 0)),
                  _layer_block((eb, d), lambda i, j: (jnp.minimum(j, n_blocks - 1), 0), u_tab, layer),
                  _layer_block((eb, d), lambda i, j: (jnp.maximum(j - 1, 0), 0), v_tab, layer)],
        out_specs=pl.BlockSpec((tt, d), lambda i, j: (i, 0)),
        out_shape=jax.ShapeDtypeStruct((n_tok, d), F32),
        scratch_shapes=[pltpu.VMEM((tt, n_exp), BF16), pltpu.VMEM((_PEER_GROUP * _PEER_STAGE_PITCH, P_KEYS), F32),
                        pltpu.VMEM((2, tt, eb), BF16)],
        compiler_params=_cparams("parallel", "arbitrary"),
        name="peer_dense",
    )(xn, e_nat, g_nat, u_tab, v_tab)


def _peer(xn, wq, keys, u_tab, v_tab, layer=None):
    e_t, g_t = _peer_route_call(xn, wq, keys)
    heads = keys.shape[0]
    n_tok = xn.shape[0]
    e_nat = e_t.reshape(heads * P_TOPK, n_tok).T
    g_nat = g_t.reshape(heads * P_TOPK, n_tok).T
    return _peer_dense_call(xn, e_nat, g_nat, u_tab, v_tab, layer=layer)


def _split_bf16(x):
    hi = x.astype(BF16)
    lo = (x - hi.astype(F32)).astype(BF16)
    return hi, lo


def _mm(a, b):
    if a.ndim == 3:
        return lax.dot_general(a, b, (((2,), (1,)), ((0,), (0,))), preferred_element_type=F32)
    return jnp.dot(a, b, preferred_element_type=F32)


def _mm_nt(a, b):
    if a.ndim == 3:
        return lax.dot_general(a, b, (((2,), (2,)), ((0,), (0,))), preferred_element_type=F32)
    return lax.dot_general(a, b, (((1,), (1,)), ((), ())), preferred_element_type=F32)


def _dot3(a, b):
    ah, al = _split_bf16(a)
    bh, bl = _split_bf16(b)
    return _mm(ah, bh) + _mm(ah, bl) + _mm(al, bh)


_CONV_FILL_ROWS = 512


def _conv_pad_rows(cols):
    return -(-(cols + 1) // V7X_SUBLANES) * V7X_SUBLANES


def _conv_fill(x_ref, x0_s, xm_s, xq_s, *, l, cols, pad):
    tc = x0_s.shape[1]
    zeros = jnp.zeros((pad, tc), F32)
    for s in (x0_s, xm_s, xq_s):
        s[0:pad, :] = zeros
        s[pad + l:pad + l + pad, :] = zeros
    step = min(_CONV_FILL_ROWS, l)
    for s0 in range(0, l, step):
        x0_s[pad + s0:pad + s0 + step, :] = x_ref[s0:s0 + step, :]
    col = lax.broadcasted_iota(jnp.int32, (step, tc), 0) & (cols - 1)
    for s0 in range(0, l, step):
        xm_s[pad + s0:pad + s0 + step, :] = jnp.where(col >= 1, x0_s[pad + s0 - 1:pad + s0 - 1 + step, :], 0.0)
        xq_s[pad + s0:pad + s0 + step, :] = jnp.where(col <= cols - 2, x0_s[pad + s0 + 1:pad + s0 + 1 + step, :], 0.0)


def _conv_rows(x0_s, xm_s, xq_s, w_ref, t0, n, *, rows, cols, pad):
    acc = None
    for dr in ((-1, 0, 1) if rows > 1 else (0,)):
        base = pl.multiple_of(t0 + pad + dr * cols, V7X_SUBLANES)
        for dc, src in ((-1, xm_s), (0, x0_s), (1, xq_s)):
            tap = (dr + 1) * 3 + dc + 1
            term = src[pl.ds(base, n), :] * w_ref[tap:tap + 1, :]
            acc = term if acc is None else acc + term
    return acc


_CONV_CHUNK = 128


def _conv_a_kernel(z_ref, w_ref, o_ref, x0_s, xm_s, xq_s, *, l, rows, cols, pad, blocks_per_part):
    _conv_fill(z_ref.at[0], x0_s, xm_s, xq_s, l=l, cols=cols, pad=pad)
    part = pl.program_id(1) // blocks_per_part
    use_norm = part != 1
    post = jnp.where(part == 2, A_HEAD_DIM ** -0.5, 1.0)
    tc = o_ref.shape[2]

    def chunk(i, carry):
        t0 = pl.multiple_of(i * _CONV_CHUNK, _CONV_CHUNK)
        y = _conv_rows(x0_s, xm_s, xq_s, w_ref, t0, _CONV_CHUNK, rows=rows, cols=cols, pad=pad)
        y = y * jax.nn.sigmoid(y)
        outs = []
        for h0 in range(0, tc, A_HEAD_DIM):
            yh = y[:, h0:h0 + A_HEAD_DIM]
            inv = lax.rsqrt(jnp.sum(yh * yh, axis=-1, keepdims=True) + RMS_EPS) * post
            outs.append(yh * jnp.where(use_norm, inv, 1.0))
        o_ref[0, pl.ds(t0, _CONV_CHUNK), :] = jnp.concatenate(outs, axis=1) if len(outs) > 1 else outs[0]
        return carry

    lax.fori_loop(0, l // _CONV_CHUNK, chunk, 0, unroll=2)


def _conv_a_call(z_kvq, w9, rows, cols, tc=256):
    b, l, _ = z_kvq.shape
    ch = w9.shape[1]
    pad = _conv_pad_rows(cols)
    scr = pltpu.VMEM((l + 2 * pad, tc), F32)
    return pl.pallas_call(
        functools.partial(_conv_a_kernel, l=l, rows=rows, cols=cols, pad=pad, blocks_per_part=ch // 3 // tc),
        grid=(b, ch // tc),
        in_specs=[pl.BlockSpec((1, l, tc), lambda i, j: (i, 0, j)),
                  pl.BlockSpec((9, tc), lambda i, j: (0, j))],
        out_specs=pl.BlockSpec((1, l, tc), lambda i, j: (i, 0, j)),
        out_shape=jax.ShapeDtypeStruct((b, l, ch), F32),
        scratch_shapes=[scr, scr, scr],
        compiler_params=_cparams("parallel", "parallel"),
        name="conv_a",
    )(z_kvq, w9)


def _conv_c_kernel(z0_ref, z1_ref, z2_ref, w0_ref, w1_ref, w2_ref, x0_ref, u_ref, *scr, l, rows, cols, pad):
    parts = ((z0_ref, w0_ref, scr[0:3]), (z1_ref, w1_ref, scr[3:6]), (z2_ref, w2_ref, scr[6:9]))
    for z_ref, _, s in parts:
        _conv_fill(z_ref.at[0], *s, l=l, cols=cols, pad=pad)

    def chunk(i, carry):
        t0 = pl.multiple_of(i * _CONV_CHUNK, _CONV_CHUNK)
        y = [_conv_rows(*s, w_ref, t0, _CONV_CHUNK, rows=rows, cols=cols, pad=pad) for _, w_ref, s in parts]
        x0_ref[0, pl.ds(t0, _CONV_CHUNK), :] = y[0]
        u_ref[0, pl.ds(t0, _CONV_CHUNK), :] = y[1] * y[2]
        return carry

    lax.fori_loop(0, l // _CONV_CHUNK, chunk, 0, unroll=2)


def _conv_c_call(z_c, w9, rows, cols, col0=0, tc=128):
    b, l, _ = z_c.shape
    cw = w9.shape[1] // 3
    nb = cw // tc
    c0 = col0 // tc
    pad = _conv_pad_rows(cols)
    scr = pltpu.VMEM((l + 2 * pad, tc), F32)
    zspec = [pl.BlockSpec((1, l, tc), functools.partial(lambda i, j, p: (i, 0, c0 + j + p * nb), p=p)) for p in range(3)]
    wspec = [pl.BlockSpec((9, tc), functools.partial(lambda i, j, p: (0, j + p * nb), p=p)) for p in range(3)]
    ospec = pl.BlockSpec((1, l, tc), lambda i, j: (i, 0, j))
    return pl.pallas_call(
        functools.partial(_conv_c_kernel, l=l, rows=rows, cols=cols, pad=pad),
        grid=(b, nb),
        in_specs=zspec + wspec,
        out_specs=(ospec, ospec),
        out_shape=(jax.ShapeDtypeStruct((b, l, cw), F32), jax.ShapeDtypeStruct((b, l, cw), F32)),
        scratch_shapes=[scr] * 9,
        compiler_params=_cparams("parallel", "parallel"),
        name="conv_c",
    )(z_c, z_c, z_c, w9, w9, w9)


def _gate_prep_kernel(z_ref, alog_ref, dtb_ref, o_ref, *, l, heads):
    lanes = z_ref.shape[2]
    ii = lax.broadcasted_iota(jnp.int32, (A_CHUNK, A_CHUNK), 0)
    jj = lax.broadcasted_iota(jnp.int32, (A_CHUNK, A_CHUNK), 1)
    lower = jnp.where(ii >= jj, 1.0, 0.0)
    upper = jnp.where(ii <= jj, 1.0, 0.0)
    lane = lax.broadcasted_iota(jnp.int32, (A_CHUNK, lanes), 1)

    def chunk(n, carry):
        r0 = pl.multiple_of(n * A_CHUNK, A_CHUNK)
        z = z_ref[0, pl.ds(r0, A_CHUNK), :]
        beta = jax.nn.sigmoid(z)
        x = z + dtb_ref[...]
        softplus = jnp.maximum(x, 0.0) + jnp.log(1.0 + jnp.exp(-jnp.abs(x)))
        la = -jnp.exp(alog_ref[...]) * softplus
        pre = _dot3(lower, la)
        suf = _dot3(upper, la)
        o_ref[0, pl.ds(r0, A_CHUNK), :] = jnp.where(lane < 2 * heads, beta, jnp.where(lane < 3 * heads, pre, suf))
        return carry

    lax.fori_loop(0, l // A_CHUNK, chunk, 0)


def _gate_prep_call(z_ba, a_log, a_dt_bias):
    b, l, lanes = z_ba.shape
    heads = a_log.shape[1]
    pad = lambda p: jnp.pad(p.reshape(1, 2 * heads).astype(F32), ((0, 0), (2 * heads, lanes - 4 * heads)))
    return pl.pallas_call(
        functools.partial(_gate_prep_kernel, l=l, heads=heads),
        grid=(b,),
        in_specs=[pl.BlockSpec((1, l, lanes), lambda i: (i, 0, 0)),
                  pl.BlockSpec((1, lanes), lambda i: (0, 0)),
                  pl.BlockSpec((1, lanes), lambda i: (0, 0))],
        out_specs=pl.BlockSpec((1, l, lanes), lambda i: (i, 0, 0)),
        out_shape=jax.ShapeDtypeStruct((b, l, lanes), F32),
        compiler_params=_cparams("parallel"),
        name="gate_prep",
    )(z_ba, pad(a_log), pad(a_dt_bias))


_TRI_BASE_LOG2 = 3


def _dot1(a, b):
    return _mm(a.astype(BF16), b.astype(BF16))


def _unit_tri_inverse(a, dot):
    c = a.shape[-1]
    ii = lax.broadcasted_iota(jnp.int32, a.shape, a.ndim - 2)
    jj = lax.broadcasted_iota(jnp.int32, a.shape, a.ndim - 1)
    eye = jnp.where(ii == jj, 1.0, 0.0)
    p = -jnp.where((ii >> _TRI_BASE_LOG2) == (jj >> _TRI_BASE_LOG2), a, 0.0)
    t = eye + p
    span = 2
    while span < (1 << _TRI_BASE_LOG2):
        p = dot(p, p)
        t = t + dot(t, p)
        span *= 2
    log2 = _TRI_BASE_LOG2
    while (1 << log2) < c:
        pair = jnp.where((ii >> (log2 + 1)) == (jj >> (log2 + 1)), a, 0.0)
        cross = jnp.where((ii >> log2) == (jj >> log2), 0.0, pair)
        t = t - dot(t, dot(cross, t))
        log2 += 1
    return t


def _unit_tri_solve(a, rhs):
    t = _unit_tri_inverse(a, _dot1).astype(BF16)
    x0 = _mm(t, rhs.astype(BF16))
    resid = rhs - x0 - _dot3(a, x0)
    return x0 + _mm(t, resid.astype(BF16))


def _mm_tn(a, b):
    return lax.dot_general(a, b, (((1,), (1,)), ((0,), (0,))), preferred_element_type=F32)


_DELTA_GROUP = 16


def _delta_kernel(k_ref, v_ref, q_ref, col_ref, row_ref, s0_ref, og_ref, an_ref, o_ref, sfin_ref,
                  km_s, nm_s, p_s, r_s, eg_s, sall_s, *, l):
    c = A_CHUNK
    dk = A_HEAD_DIM
    n_chunks = l // c
    g = min(_DELTA_GROUP, n_chunks)
    shape = (2 * g, c, c)
    ii = lax.broadcasted_iota(jnp.int32, shape, 1)
    jj = lax.broadcasted_iota(jnp.int32, shape, 2)
    lag = jnp.where(lax.broadcasted_iota(jnp.int32, shape, 0) >= g, jj - ii, ii - jj)
    incl = lag >= 0
    strict = lag > 0
    two = lambda x: jnp.concatenate([x, x], axis=0)

    def pass1(gi, carry):
        rows_blk = pl.ds(pl.multiple_of(gi * (g * c), g * c), g * c)
        chunks = pl.ds(gi * g, g)
        kc = k_ref[0, rows_blk, :].reshape(g, c, dk)
        vc = v_ref[0, rows_blk, :].reshape(g, c, dk)
        qc = q_ref[0, rows_blk, :].reshape(g, c, dk)
        cols = col_ref[0, 0, rows_blk, :].reshape(g, c, 8)
        rws = row_ref[0, 0, chunks]
        beta = jnp.concatenate([cols[:, :, 0:1], cols[:, :, 1:2]], axis=0)
        gcol = jnp.concatenate([cols[:, :, 2:3], cols[:, :, 3:4]], axis=0)
        grow = jnp.concatenate([rws[:, 0:1, :], rws[:, 1:2, :]], axis=0)
        dec = jnp.where(incl, jnp.exp(jnp.where(incl, gcol - grow, 0.0)), 0.0)
        kb = kc.astype(BF16)
        kk = two(_mm_nt(kb, kb))
        qk = two(_mm_nt(qc.astype(BF16), kb))
        eg = jnp.exp(gcol)
        k2, v2, q2 = two(kc), two(vc), two(qc)
        sol = _unit_tri_solve(jnp.where(strict, kk * dec * beta, 0.0),
                              jnp.concatenate([v2 * beta, k2 * (beta * eg)], axis=2))
        ub = sol[:, :, :dk].astype(BF16)
        wb = sol[:, :, dk:].astype(BF16)
        glast = jnp.concatenate([gcol[:g, c - 1:c, :], gcol[g:, 0:1, :]], axis=0)
        ke = (k2 * jnp.exp(glast - gcol)).astype(BF16)
        qkd = (qk * dec).astype(BF16)
        km = _mm_tn(ke, wb)
        nm = _mm_tn(ke, ub)
        pm = q2 * eg - _mm(qkd, wb)
        rm = _mm(qkd, ub)
        eglast = jnp.exp(glast)
        for d in range(2):
            sl = slice(d * g, (d + 1) * g)
            km_s[d, chunks] = km[sl].astype(BF16)
            nm_s[d, chunks] = nm[sl]
            p_s[d, rows_blk, :] = pm[sl].reshape(g * c, dk).astype(BF16)
            r_s[d, rows_blk, :] = rm[sl].reshape(g * c, dk)
            eg_s[d, chunks] = jnp.broadcast_to(eglast[sl], (g, V7X_SUBLANES, dk))
        return carry

    lax.fori_loop(0, n_chunks // g, pass1, 0)

    def pass2(n, states):
        new_states = []
        for d, ch in ((0, n), (1, n_chunks - 1 - n)):
            s = states[d]
            sb = s.astype(BF16)
            sall_s[d, ch] = sb
            new_states.append(s * eg_s[d, ch][0:1, :] - jnp.dot(km_s[d, ch], sb, preferred_element_type=F32)
                              + nm_s[d, ch])
        return tuple(new_states)

    s_f, s_b = lax.fori_loop(0, n_chunks, pass2, (s0_ref[0, 0, 0], s0_ref[0, 0, 1]))
    sfin_ref[0, 0, 0] = s_f
    sfin_ref[0, 0, 1] = s_b

    def pass3(gi, carry):
        rows_blk = pl.ds(pl.multiple_of(gi * (g * c), g * c), g * c)
        chunks = pl.ds(gi * g, g)
        o = None
        for d in range(2):
            term = _mm(p_s[d, rows_blk, :].reshape(g, c, dk), sall_s[d, chunks]) + r_s[d, rows_blk, :].reshape(g, c, dk)
            o = term if o is None else o + term
        o = o.reshape(g * c, dk)
        o = o * lax.rsqrt(jnp.mean(o * o, axis=-1, keepdims=True) + RMS_EPS) * an_ref[...]
        og = og_ref[0, rows_blk, :]
        o_ref[0, rows_blk, :] = (o * (og * jax.nn.sigmoid(og))).astype(o_ref.dtype)
        return carry

    lax.fori_loop(0, n_chunks // g, pass3, 0)


def _delta_call(act, z_og, gates, s0, a_norm, og_col=0):
    b, l, ch = act.shape
    heads = ch // 3 // A_HEAD_DIM
    dk = A_HEAD_DIM
    c = A_CHUNK
    n_chunks = l // c
    g4 = jnp.stack([gates[..., i * heads:(i + 1) * heads] for i in range(4)], axis=-1)
    col = jnp.pad(jnp.transpose(g4, (0, 2, 1, 3)), ((0, 0), (0, 0), (0, 0), (0, 4)))
    row = jnp.transpose(g4[..., 2:4].reshape(b, n_chunks, c, heads, 2), (0, 3, 1, 4, 2))
    row = jnp.pad(row, ((0, 0), (0, 0), (0, 0), (0, 6), (0, 0)))
    tok = lambda off: pl.BlockSpec((1, l, dk), functools.partial(lambda i, h, off: (i, 0, h + off), off=off))
    return pl.pallas_call(
        functools.partial(_delta_kernel, l=l),
        grid=(b, heads),
        in_specs=[tok(0), tok(heads), tok(2 * heads),
                  pl.BlockSpec((1, 1, l, 8), lambda i, h: (i, h, 0, 0)),
                  pl.BlockSpec((1, 1, n_chunks, 8, c), lambda i, h: (i, h, 0, 0, 0)),
                  pl.BlockSpec((1, 1, 2, dk, dk), lambda i, h: (i, h, 0, 0, 0)),
                  tok(og_col // dk),
                  pl.BlockSpec((1, dk), lambda i, h: (0, 0))],
        out_specs=(tok(0), pl.BlockSpec((1, 1, 2, dk, dk), lambda i, h: (i, h, 0, 0, 0))),
        out_shape=(jax.ShapeDtypeStruct((b, l, heads * dk), BF16),
                   jax.ShapeDtypeStruct((b, heads, 2, dk, dk), F32)),
        scratch_shapes=[pltpu.VMEM((2, n_chunks, dk, dk), BF16), pltpu.VMEM((2, n_chunks, dk, dk), F32),
                        pltpu.VMEM((2, l, dk), BF16), pltpu.VMEM((2, l, dk), F32),
                        pltpu.VMEM((2, n_chunks, V7X_SUBLANES, dk), F32),
                        pltpu.VMEM((2, n_chunks, dk, dk), BF16)],
        compiler_params=_cparams("parallel", "parallel"),
        name="delta_rule",
    )(act, act, act, col, row, s0, z_og, a_norm.reshape(1, dk).astype(F32))


def _gmlp_kernel(zu_ref, zv_ref, ws_ref, bs_ref, o_ref):
    groups = ws_ref.shape[0]
    gd = zu_ref.shape[1] // groups
    u = jax.nn.gelu(zu_ref[...])
    v = jax.nn.gelu(zv_ref[...])
    mu = jnp.mean(v, axis=-1, keepdims=True)
    var = jnp.mean(jnp.square(v - mu), axis=-1, keepdims=True)
    vn = ((v - mu) * lax.rsqrt(var + RMS_EPS)).astype(BF16)
    mixed = [jnp.dot(ws_ref[g], vn[:, g * gd:(g + 1) * gd], preferred_element_type=F32) + bs_ref[:, g:g + 1]
             for g in range(groups)]
    o_ref[...] = (u * jnp.concatenate(mixed, axis=1)).astype(o_ref.dtype)


def _gmlp_call(z_b, w_s, b_s, bw, col0=0):
    n_tok = z_b.shape[0]
    groups, p, _ = w_s.shape
    c0 = col0 // bw
    return pl.pallas_call(
        _gmlp_kernel,
        grid=(n_tok // p,),
        in_specs=[pl.BlockSpec((p, bw), lambda i: (i, c0)),
                  pl.BlockSpec((p, bw), lambda i: (i, c0 + 1)),
                  pl.BlockSpec((groups, p, p), lambda i: (0, 0, 0)),
                  pl.BlockSpec((p, groups), lambda i: (0, 0))],
        out_specs=pl.BlockSpec((p, bw), lambda i: (i, 0)),
        out_shape=jax.ShapeDtypeStruct((n_tok, bw), BF16),
        compiler_params=_cparams("parallel"),
        name="gmlp",
    )(z_b, z_b, w_s, b_s.T.astype(F32))


_FEAT_PAD = 64


def _filter_feats(l):
    pos = np.arange(l, dtype=np.float32)
    t = np.linspace(0.0, 1.0, l, dtype=np.float32)[:, None]
    bands = np.linspace(1e-4, C_POS_BANDS - 1, C_POS_BANDS, dtype=np.float32)
    ang = np.float32(2.0 * math.pi / l) * pos[:, None] * bands
    feats = np.concatenate([t, np.cos(ang), -np.sin(ang)], axis=-1).astype(np.float32)
    return np.pad(feats, ((0, 0), (0, _FEAT_PAD - feats.shape[1])))


def _filter_kernel(feat_ref, w1_ref, b1_ref, f1_ref, w2_ref, b2_ref, f2_ref, w3_ref, b3_ref, rate_ref, o_ref):
    feats = feat_ref[...]
    hid = jnp.sin(f1_ref[...] * (_dot3(feats, w1_ref[...]) + b1_ref[...]))
    hid = jnp.sin(f2_ref[...] * (_dot3(hid, w2_ref[...]) + b2_ref[...]))
    filt = _dot3(hid, w3_ref[...]) + b3_ref[...]
    filt = filt * jnp.exp(-feats[:, 0:1] * rate_ref[...])
    o_ref[...] = filt * lax.rsqrt(jnp.sum(filt * filt, axis=0, keepdims=True) + RMS_EPS)


def _filter_call(l, fw1, fb1, freq1, fw2, fb2, freq2, fw3, fb3, tc=256):
    hidden = fw1.shape[1]
    two_c = fw3.shape[1]
    cw = two_c // 2
    feats = jnp.asarray(_filter_feats(l))
    w1 = jnp.pad(fw1.astype(F32), ((0, _FEAT_PAD - fw1.shape[0]), (0, 0)))
    rate = np.abs(np.linspace(C_MIN_DECAY, C_MAX_DECAY, cw, dtype=np.float32))
    rate = jnp.asarray(np.concatenate([rate, rate]).reshape(1, two_c))
    row = lambda v: v.reshape(1, -1).astype(F32)
    full = lambda shape: pl.BlockSpec(shape, lambda j: (0, 0))
    return pl.pallas_call(
        _filter_kernel,
        grid=(two_c // tc,),
        in_specs=[full((l, _FEAT_PAD)), full((_FEAT_PAD, hidden)), full((1, hidden)), full((1, hidden)),
                  full((hidden, hidden)), full((1, hidden)), full((1, hidden)),
                  pl.BlockSpec((hidden, tc), lambda j: (0, j)), pl.BlockSpec((1, tc), lambda j: (0, j)),
                  pl.BlockSpec((1, tc), lambda j: (0, j))],
        out_specs=pl.BlockSpec((l, tc), lambda j: (0, j)),
        out_shape=jax.ShapeDtypeStruct((l, two_c), F32),
        compiler_params=_cparams("parallel"),
        name="hyena_filter",
    )(feats, w1, row(fb1), row(freq1), fw2.astype(F32), row(fb2), row(freq2), fw3.astype(F32), row(fb3), rate)


def _dft_sizes(l):
    n1 = 64 if l >= 4096 else 32
    return n1, (2 * l) // n1


def _hi_lo(x):
    x = jnp.asarray(x, F32)
    hi = x.astype(BF16)
    return hi, (x - hi.astype(F32)).astype(BF16)


def _dft_consts(l):
    n1, n2 = _dft_sizes(l)
    n = n1 * n2
    half = n1 // 2
    k1 = np.arange(n1)[None, :, None]
    m1 = np.arange(half)[None, None, :]
    m2 = np.arange(n2)[:, None, None]
    f1 = np.exp(-2j * np.pi * (m1 * k1 / n1 + m2 * k1 / n))
    fwd = np.concatenate([f1.real, f1.imag], axis=1)
    fh, fl = _hi_lo(fwd)
    fwd1 = jnp.concatenate([fh, fh, fl], axis=2)
    rt = np.transpose(f1.real, (0, 2, 1)) / n
    it = np.transpose(f1.imag, (0, 2, 1)) / n
    (rh, rl), (ih, il) = _hi_lo(rt), _hi_lo(it)
    inv1 = jnp.concatenate([rh, ih, rh, ih, rl, il], axis=2)
    kk = np.arange(n2)
    f2 = np.exp(-2j * np.pi * np.outer(kk, kk) / n2)
    m = np.block([[f2.real, -f2.imag], [f2.imag, f2.real]])
    mi = np.block([[f2.real, f2.imag], [-f2.imag, f2.real]])
    (mh, ml), (mih, mil) = _hi_lo(m), _hi_lo(mi)
    return fwd1, jnp.concatenate([mh, mh, ml], axis=1), jnp.concatenate([mih, mih, mil], axis=1), inv1


_DFT_UNROLL = 8


def _stack3(x):
    hi, lo = _split_bf16(x)
    return jnp.concatenate([hi, lo, hi], axis=0)


def _dft_pitch(n2):
    return n2 + V7X_SUBLANES


def _dft_level1(x_ref, fwd1_ref, ar_s, ai_s, n1, n2):
    half = n1 // 2
    pitch = _dft_pitch(n2)

    def body(j, carry):
        xs = x_ref[pl.ds(j, half, stride=n2), :]
        a = jnp.dot(fwd1_ref[j], _stack3(xs), preferred_element_type=F32)
        ar_s[pl.ds(j, n1, stride=pitch), :] = a[:n1]
        ai_s[pl.ds(j, n1, stride=pitch), :] = a[n1:]
        return carry

    lax.fori_loop(0, n2, body, 0, unroll=_DFT_UNROLL)


def _block_rows(k1, n2):
    return pl.ds(pl.multiple_of(k1 * _dft_pitch(n2), V7X_SUBLANES), n2)


def _pair_rows(pair, n2):
    return pl.ds(pl.multiple_of(pair * (2 * n2), 2 * n2), 2 * n2)


def _side_by_side(x, n2):
    return jnp.concatenate([x[:n2], x[n2:]], axis=1)


def _dft_level2(ar_s, ai_s, m3_ref, pair, n2):
    lo, hi = _block_rows(2 * pair, n2), _block_rows(2 * pair + 1, n2)
    blk = jnp.concatenate([jnp.concatenate([ar_s[lo, :], ar_s[hi, :]], axis=1),
                           jnp.concatenate([ai_s[lo, :], ai_s[hi, :]], axis=1)], axis=0)
    x = jnp.dot(m3_ref[...], _stack3(blk), preferred_element_type=F32)
    return x[:n2], x[n2:]


def _spectrum_kernel(hf_ref, hb_ref, fwd1_ref, m3_ref, o_ref, ar_s, ai_s, *, n1, n2):
    tc = o_ref.shape[2]
    for which, h_ref in enumerate((hf_ref, hb_ref)):
        _dft_level1(h_ref, fwd1_ref, ar_s, ai_s, n1, n2)

        def body(pair, carry):
            xr, xi = _dft_level2(ar_s, ai_s, m3_ref, pair, n2)
            for side in range(2):
                rows = pl.ds(pl.multiple_of((2 * pair + side) * n2, n2), n2)
                lanes = slice(side * tc, (side + 1) * tc)
                if which == 0:
                    o_ref[0, rows, :] = xr[:, lanes]
                    o_ref[1, rows, :] = xi[:, lanes]
                else:
                    o_ref[0, rows, :] += xr[:, lanes]
                    o_ref[1, rows, :] -= xi[:, lanes]
            return carry

        lax.fori_loop(0, n1 // 2, body, 0, unroll=_DFT_UNROLL // 2)


def _spectrum_call(filt, consts, tc=128):
    l, two_c = filt.shape
    cw = two_c // 2
    nb = cw // tc
    n1, n2 = _dft_sizes(l)
    fwd1, m3, _, _ = consts
    return pl.pallas_call(
        functools.partial(_spectrum_kernel, n1=n1, n2=n2),
        grid=(nb,),
        in_specs=[pl.BlockSpec((l, tc), lambda j: (0, j)),
                  pl.BlockSpec((l, tc), lambda j: (0, j + nb)),
                  pl.BlockSpec(fwd1.shape, lambda j: (0, 0, 0)),
                  pl.BlockSpec(m3.shape, lambda j: (0, 0))],
        out_specs=pl.BlockSpec((2, 2 * l, tc), lambda j: (0, 0, j)),
        out_shape=jax.ShapeDtypeStruct((2, 2 * l, cw), F32),
        scratch_shapes=[pltpu.VMEM((n1 * _dft_pitch(n2), tc), F32), pltpu.VMEM((n1 * _dft_pitch(n2), tc), F32)],
        compiler_params=_cparams("parallel"),
        name="hyena_spectrum",
    )(filt, filt, fwd1, m3)


def _longconv_kernel(u_ref, x0_ref, h_ref, skip_ref, fwd1_ref, m3_ref, mi3_ref, inv1_ref, o_ref,
                     ar_s, ai_s, y_s, *, l, n1, n2):
    half = n1 // 2
    _dft_level1(u_ref.at[0], fwd1_ref, ar_s, ai_s, n1, n2)

    tc = o_ref.shape[2]

    pitch = _dft_pitch(n2)

    def per_k1_pair(pair, carry):
        xr, xi = _dft_level2(ar_s, ai_s, m3_ref, pair, n2)
        rows = _pair_rows(pair, n2)
        hr = _side_by_side(h_ref[0, rows, :], n2)
        hi = _side_by_side(h_ref[1, rows, :], n2)
        y = jnp.concatenate([xr * hr - xi * hi, xr * hi + xi * hr], axis=0)
        b = jnp.dot(mi3_ref[...], _stack3(y), preferred_element_type=F32)
        for side in range(2):
            blk = _block_rows(2 * pair + side, n2)
            ar_s[blk, :] = b[:n2, side * tc:(side + 1) * tc]
            ai_s[blk, :] = b[n2:, side * tc:(side + 1) * tc]
        return carry

    lax.fori_loop(0, n1 // 2, per_k1_pair, 0, unroll=_DFT_UNROLL // 2)

    def per_n2(j, carry):
        br = ar_s[pl.ds(j, n1, stride=pitch), :]
        bi = ai_s[pl.ds(j, n1, stride=pitch), :]
        (brh, brl), (bih, bil) = _split_bf16(br), _split_bf16(bi)
        rhs = jnp.concatenate([brh, bih, brl, bil, brh, bih], axis=0)
        y_s[pl.ds(j, half, stride=pitch), :] = jnp.dot(inv1_ref[j], rhs, preferred_element_type=F32)
        return carry

    lax.fori_loop(0, n2, per_n2, 0, unroll=_DFT_UNROLL)

    def finish(i, carry):
        rows = pl.ds(pl.multiple_of(i * n2, n2), n2)
        u = u_ref[0, rows, :]
        y = y_s[_block_rows(i, n2), :]
        o_ref[0, rows, :] = (x0_ref[0, rows, :] * (y + skip_ref[...] * u)).astype(o_ref.dtype)
        return carry

    lax.fori_loop(0, half, finish, 0, unroll=_DFT_UNROLL)


def _longconv_call(u, x0, spec, skip, consts, tc=128):
    b, l, cw = u.shape
    n1, n2 = _dft_sizes(l)
    fwd1, m3, mi3, inv1 = consts
    tok = pl.BlockSpec((1, l, tc), lambda j, i: (i, 0, j))
    const = lambda a: pl.BlockSpec(a.shape, lambda j, i: (0,) * a.ndim)
    return pl.pallas_call(
        functools.partial(_longconv_kernel, l=l, n1=n1, n2=n2),
        grid=(cw // tc, b),
        in_specs=[tok, tok,
                  pl.BlockSpec((2, 2 * l, tc), lambda j, i: (0, 0, j)),
                  pl.BlockSpec((1, tc), lambda j, i: (0, j)),
                  const(fwd1), const(m3), const(mi3), const(inv1)],
        out_specs=tok,
        out_shape=jax.ShapeDtypeStruct((b, l, cw), BF16),
        scratch_shapes=[pltpu.VMEM((n1 * _dft_pitch(n2), tc), F32), pltpu.VMEM((n1 * _dft_pitch(n2), tc), F32),
                        pltpu.VMEM((n1 // 2 * _dft_pitch(n2), tc), F32)],
        compiler_params=_cparams("parallel", "parallel"),
        name="hyena_longconv",
    )(u, x0, spec, skip.reshape(1, cw).astype(F32), fwd1, m3, mi3, inv1)


def _weight_stacks(p):
    aw = p["w_br_a"].shape[1]
    heads = p["a_log"].shape[2]
    off_ba = 2 * aw
    off_q = off_ba + 4 * heads
    w_in = p["w_in"]
    cast = lambda w: w.astype(BF16)
    return {
        "w_main": cast(jnp.concatenate([w_in[:, :, :off_ba], w_in[:, :, off_q:]], axis=2)),
        "w_ba": cast(jnp.pad(w_in[:, :, off_ba:off_q], ((0, 0), (0, 0), (0, V7X_LANES - 4 * heads)))),
        "w_out": cast(p["w_out"]), "p_u": cast(p["p_u"]), "p_v": cast(p["p_v"]),
    }


def _layer_weights(i, p, stacks):
    aw, bw, cw = p["w_br_a"].shape[1], p["w_br_b"].shape[1], p["w_br_c"].shape[1]
    cast = lambda w: w.astype(BF16)
    col_og = 3 * aw
    col_b = col_og + aw
    col_c = col_b + 2 * bw
    col_gate = col_c + 3 * cw
    return {
        "layer": i, **stacks,
        "col_og": col_og, "col_b": col_b, "col_c": col_c, "col_gate": col_gate, "bw": bw,
        "w_br_a": cast(p["w_br_a"][i]), "w_br_b": cast(p["w_br_b"][i]), "w_br_c": cast(p["w_br_c"][i]),
        "a_conv": p["a_conv"][i].reshape(9, -1), "c_conv": p["c_conv"][i].reshape(9, -1),
        "a_log": p["a_log"][i], "a_dt_bias": p["a_dt_bias"][i], "a_norm": p["a_norm"][i],
        "b_ws": cast(p["b_ws"][i]), "b_bs": p["b_bs"][i], "c_skip": p["c_skip"][i],
        "filter": tuple(p[k][i] for k in ("c_fw1", "c_fb1", "c_freq1", "c_fw2", "c_fb2", "c_freq2", "c_fw3", "c_fb3")),
        "p_wq": cast(p["p_wq"][i]), "p_keys": cast(p["p_keys"][i]),
    }


def _delta_branch(xf, z, b, l, rows, cols, s0, lw):
    z3 = z.reshape(b, l, -1)
    z_ba = _matmul(xf, lw["w_ba"], layer=lw["layer"]).reshape(b, l, -1)
    act = _conv_a_call(z3, lw["a_conv"], rows, cols)
    gates = _gate_prep_call(z_ba, lw["a_log"], lw["a_dt_bias"])
    o_a, s_fin = _delta_call(act, z3, gates, s0, lw["a_norm"], og_col=lw["col_og"])
    return o_a.reshape(b * l, -1), s_fin


def _token_mixer(xn, rows, cols, s0, lw):
    b, l, d = xn.shape
    xf = xn.reshape(b * l, d)
    z = _matmul(xf, lw["w_main"], tm=1024, layer=lw["layer"])
    o_a, s_fin = _delta_branch(xf, z, b, l, rows, cols, s0, lw)
    o_b = _gmlp_call(z, lw["b_ws"], lw["b_bs"], lw["bw"], col0=lw["col_b"])
    x0, u = _conv_c_call(z.reshape(b, l, -1), lw["c_conv"], rows, cols, col0=lw["col_c"])
    consts = _dft_consts(l)
    spec = _spectrum_call(_filter_call(l, *lw["filter"]), consts)
    o_c = _longconv_call(u, x0, spec, lw["c_skip"], consts).reshape(b * l, -1)
    merged = _merge_call(o_a, o_b, o_c, lw["w_br_a"], lw["w_br_b"], lw["w_br_c"], z, lw["col_gate"])
    return _matmul(merged, lw["w_out"], layer=lw["layer"]).reshape(b, l, d), s_fin


def _peer_layer(xn, lw):
    b, l, d = xn.shape
    return _peer(xn.reshape(b * l, d), lw["p_wq"], lw["p_keys"], lw["p_u"], lw["p_v"],
                 layer=lw["layer"]).reshape(b, l, d)


def kernel(x, c, ctx, c_ctx, ada_w, ada_b, w_in, a_conv, a_log, a_dt_bias, a_norm, b_ws, b_bs, c_conv, c_fw1, c_fb1,
           c_freq1, c_fw2, c_fb2, c_freq2, c_fw3, c_fb3, c_skip, w_br_a, w_br_b, w_br_c, w_out, p_wq, p_keys, p_u,
           p_v, final_norm):
    params = dict(w_in=w_in, a_conv=a_conv, a_log=a_log, a_dt_bias=a_dt_bias, a_norm=a_norm, b_ws=b_ws, b_bs=b_bs,
                  c_conv=c_conv, c_fw1=c_fw1, c_fb1=c_fb1, c_freq1=c_freq1, c_fw2=c_fw2, c_fb2=c_fb2, c_freq2=c_freq2,
                  c_fw3=c_fw3, c_fb3=c_fb3, c_skip=c_skip, w_br_a=w_br_a, w_br_b=w_br_b, w_br_c=w_br_c, w_out=w_out,
                  p_wq=p_wq, p_keys=p_keys, p_u=p_u, p_v=p_v)
    stacks = _weight_stacks(params)
    b, l, d = x.shape
    depth = w_in.shape[0]
    rows = l // GRID_W
    l_ctx = ctx.shape[1]
    heads = a_log.shape[2]
    s_zero = jnp.zeros((b, heads, 2, A_HEAD_DIM, A_HEAD_DIM), F32)
    cvec = jnp.concatenate([c, c_ctx[None, :], jnp.zeros((V7X_SUBLANES - b - 1, d), F32)], axis=0)

    h, h_pending = x, None
    hc, hc_pending = ctx, None
    for i in range(depth):
        lw = _layer_weights(i, params, stacks)
        mod_all = _mod_call(cvec, ada_w, ada_b[i], i)
        mod = mod_all[:b].reshape(b, N_MOD, 1, d)
        mod_c = mod_all[b].reshape(N_MOD, 1, 1, d)

        def norm(stream, pending, scale, shift):
            if pending is None:
                return stream, _norm_mod_call(stream, scale, shift)[1]
            return _norm_mod_call(stream, scale, shift, delta=pending[0], gate=pending[1], emit_h=True)

        hc, xnc = norm(hc, hc_pending, mod_c[1], mod_c[0])
        if i == depth - 1:
            xcf = xnc.reshape(b * l_ctx, d)
            z_c = _matmul(xcf, lw["w_main"], ncols=lw["col_b"], layer=i)
            _, s_ctx = _delta_branch(xcf, z_c, b, l_ctx, 1, l_ctx, s_zero, lw)
        else:
            out_c, s_ctx = _token_mixer(xnc, 1, l_ctx, s_zero, lw)
            hc, xnc2 = norm(hc, (out_c, mod_c[2]), mod_c[4], mod_c[3])
            hc_pending = (_peer_layer(xnc2, lw), mod_c[5])
        h, xn = norm(h, h_pending, mod[:, 1], mod[:, 0])
        out, _ = _token_mixer(xn, rows, GRID_W, s_ctx, lw)
        h, xn2 = norm(h, (out, mod[:, 2]), mod[:, 4], mod[:, 3])
        h_pending = (_peer_layer(xn2, lw), mod[:, 5])
    return _norm_mod_call(h, final_norm.reshape(1, 1, d), delta=h_pending[0], gate=h_pending[1], out_dtype=F32)[1]
```
